```python
import math
import jax, jax.numpy as jnp
from jax import lax
import numpy as np

D_MODEL = 4096
BATCH = 2
SEQ = 8192
DEPTH = 1

MIX_WIDTH = D_MODEL
GROUP_WIDTH = MIX_WIDTH // 2
GLA_HEADS = 4
GLA_DV = GROUP_WIDTH // GLA_HEADS
GLA_DK = GLA_DV // 2
GLA_GATE_RANK = 16
GLA_TAU = 16.0
GLA_CHUNK = 64
MLSTM_HEADS = 4
MLSTM_DV = GROUP_WIDTH // MLSTM_HEADS
MLSTM_DK = MLSTM_DV // 2
MLSTM_CHUNK = 64
CONV_WIDTH = 4
MEM_TOKENS = 256
XATTN_HEADS = 4
XATTN_HEAD_DIM = D_MODEL // XATTN_HEADS
FFN_HIDDEN = 256 * math.ceil(8 * D_MODEL / (3 * 256))
EPS = 1e-6

SPLIT_SIZES = (
    GLA_HEADS * GLA_DK,
    GLA_HEADS * GLA_DK,
    GLA_HEADS * GLA_DV,
    GLA_HEADS * GLA_DV,
    GLA_GATE_RANK,
    MLSTM_HEADS * MLSTM_DK,
    MLSTM_HEADS * MLSTM_DK,
    MLSTM_HEADS * MLSTM_DV,
    MLSTM_HEADS * MLSTM_DV,
    MLSTM_HEADS,
    MLSTM_HEADS,
)
IN_COLS = sum(SPLIT_SIZES)

kernel_name = "hybrid_gla_mlstm_parallel_heads"


def rmsnorm(x, g):
    xf = x.astype(jnp.float32)
    y = xf * lax.rsqrt(jnp.mean(xf * xf, axis=-1, keepdims=True) + EPS)
    return (y * g.astype(jnp.float32)).astype(x.dtype)


def head_rmsnorm(h, g):
    y = h * lax.rsqrt(jnp.mean(h * h, axis=-1, keepdims=True) + EPS)
    return y * g.astype(jnp.float32)


def to_chunks(t, c):
    b, s, h = t.shape[:3]
    t = t.reshape((b, s // c, c, h) + t.shape[3:])
    return jnp.moveaxis(t, (1, 3), (0, 2))


def from_chunks(t):
    t = jnp.moveaxis(t, (0, 2), (1, 3))
    b, n, c, h, d = t.shape
    return t.reshape(b, n * c, h, d)


def gla_chunked(q, k, v, log_a):
    bsz = q.shape[0]
    qc, kc, vc = to_chunks(q, GLA_CHUNK), to_chunks(k, GLA_CHUNK), to_chunks(v, GLA_CHUNK)
    bc = lax.cumsum(to_chunks(log_a, GLA_CHUNK), axis=3)
    mask = jnp.tril(jnp.ones((GLA_CHUNK, GLA_CHUNK), dtype=bool))

    def body(state, inp):
        qb, kb, vb, bb = inp
        diff = jnp.where(mask[None, None, :, :, None],
                         bb[:, :, :, None, :] - bb[:, :, None, :, :], -jnp.inf)
        attn = jnp.sum(qb[:, :, :, None, :] * kb[:, :, None, :, :] * jnp.exp(diff), axis=-1)
        out = (jnp.einsum('bhts,bhsv->bhtv', attn, vb)
               + jnp.einsum('bhtd,bhdv->bhtv', qb * jnp.exp(bb), state))
        b_last = bb[:, :, -1:, :]
        state = (jnp.exp(b_last[:, :, 0, :, None]) * state
                 + jnp.einsum('bhsd,bhsv->bhdv', kb * jnp.exp(b_last - bb), vb))
        return state, out

    state0 = jnp.zeros((bsz, GLA_HEADS, GLA_DK, GLA_DV), jnp.float32)
    _, out = lax.scan(body, state0, (qc, kc, vc, bc))
    return from_chunks(out)


def mlstm_chunked(q, k, v, i_pre, log_f):
    bsz = q.shape[0]
    qc, kc, vc = to_chunks(q, MLSTM_CHUNK), to_chunks(k, MLSTM_CHUNK), to_chunks(v, MLSTM_CHUNK)
    ic = to_chunks(i_pre, MLSTM_CHUNK)
    bc = lax.cumsum(to_chunks(log_f, MLSTM_CHUNK), axis=3)
    mask = jnp.tril(jnp.ones((MLSTM_CHUNK, MLSTM_CHUNK), dtype=bool))

    def body(carry, inp):
        c_prev, n_prev, m_prev = carry
        qb, kb, vb, ib, bb = inp
        log_d = jnp.where(mask, bb[..., :, None] - bb[..., None, :] + ib[..., None, :], -jnp.inf)
        inter_log = bb + m_prev[..., None]
        m = jnp.maximum(inter_log, jnp.max(log_d, axis=-1))
        g_inter = jnp.exp(inter_log - m)
        scores = jnp.einsum('bhtd,bhsd->bhts', qb, kb) * jnp.exp(log_d - m[..., None])
        num = (jnp.einsum('bhts,bhsv->bhtv', scores, vb)
               + g_inter[..., None] * jnp.einsum('bhtd,bhdv->bhtv', qb, c_prev))
        den = jnp.sum(scores, axis=-1) + g_inter * jnp.einsum('bhtd,bhd->bht', qb, n_prev)
        h = num / jnp.maximum(jnp.abs(den), jnp.exp(-m))[..., None]
        b_last = bb[..., -1]
        log_w = b_last[..., None] - bb + ib
        m_new = jnp.maximum(b_last + m_prev, jnp.max(log_w, axis=-1))
        w = jnp.exp(log_w - m_new[..., None])
        decay = jnp.exp(b_last + m_prev - m_new)
        c_new = decay[..., None, None] * c_prev + jnp.einsum('bhs,bhsd,bhsv->bhdv', w, kb, vb)
        n_new = decay[..., None] * n_prev + jnp.einsum('bhs,bhsd->bhd', w, kb)
        return (c_new, n_new, m_new), h

    carry0 = (jnp.zeros((bsz, MLSTM_HEADS, MLSTM_DK, MLSTM_DV), jnp.float32),
              jnp.zeros((bsz, MLSTM_HEADS, MLSTM_DK), jnp.float32),
              jnp.zeros((bsz, MLSTM_HEADS), jnp.float32))
    _, out = lax.scan(body, carry0, (qc, kc, vc, ic, bc))
    return from_chunks(out)


def causal_dwconv(x, w, b):
    y = lax.conv_general_dilated(
        x, w[:, None, :].astype(x.dtype), window_strides=(1,),
        padding=[(CONV_WIDTH - 1, 0)], dimension_numbers=('NWC', 'WIO', 'NWC'),
        feature_group_count=x.shape[-1])
    return y + b.astype(x.dtype)


def hybrid_mixer(n, w_in, gla_gate_w2, gla_gate_b, gla_norm_g, mlstm_conv_w, mlstm_conv_b,
                 mlstm_igate_b, mlstm_fgate_b, mlstm_norm_g, w_out):
    bsz, s, _ = n.shape
    f32 = jnp.float32
    proj = n @ w_in
    offsets = np.cumsum(SPLIT_SIZES)[:-1].tolist()
    qg, kg, vg, gg, ag, qm, km, vm, om, im, fm = jnp.split(proj, offsets, axis=-1)

    qg = qg.astype(f32).reshape(bsz, s, GLA_HEADS, GLA_DK) * (GLA_DK ** -0.5)
    kg = kg.astype(f32).reshape(bsz, s, GLA_HEADS, GLA_DK)
    vg = vg.astype(f32).reshape(bsz, s, GLA_HEADS, GLA_DV)
    gate_logit = (ag @ gla_gate_w2 + gla_gate_b).astype(f32)
    log_a = (jax.nn.log_sigmoid(gate_logit) / GLA_TAU).reshape(bsz, s, GLA_HEADS, GLA_DK)
    o_gla = gla_chunked(qg, kg, vg, log_a)
    o_gla = head_rmsnorm(o_gla, gla_norm_g) * jax.nn.silu(
        gg.astype(f32)).reshape(bsz, s, GLA_HEADS, GLA_DV)
    o_gla = o_gla.reshape(bsz, s, GROUP_WIDTH)

    qk = jax.nn.silu(causal_dwconv(jnp.concatenate([qm, km], axis=-1), mlstm_conv_w, mlstm_conv_b))
    qm, km = jnp.split(qk.astype(f32), 2, axis=-1)
    qm = qm.reshape(bsz, s, MLSTM_HEADS, MLSTM_DK)
    km = km.reshape(bsz, s, MLSTM_HEADS, MLSTM_DK) * (MLSTM_DK ** -0.5)
    vm = vm.astype(f32).reshape(bsz, s, MLSTM_HEADS, MLSTM_DV)
    i_pre = (im + mlstm_igate_b).astype(f32)
    log_f = jax.nn.log_sigmoid((fm + mlstm_fgate_b).astype(f32))
    h = mlstm_chunked(qm, km, vm, i_pre, log_f)
    o_ml = jax.nn.sigmoid(om.astype(f32)).reshape(bsz, s, MLSTM_HEADS, MLSTM_DV) * head_rmsnorm(h, mlstm_norm_g)
    o_ml = o_ml.reshape(bsz, s, GROUP_WIDTH)

    mixed = jnp.concatenate([o_gla, o_ml], axis=-1).astype(n.dtype)
    return mixed @ w_out


def cross_attention(n, mem_n, wq, wk, wv, wo):
    bsz, s, _ = n.shape
    m = mem_n.shape[1]
    q = (n @ wq).reshape(bsz, s, XATTN_HEADS, XATTN_HEAD_DIM)
    k = (mem_n @ wk).reshape(bsz, m, XATTN_HEADS, XATTN_HEAD_DIM)
    v = (mem_n @ wv).reshape(bsz, m, XATTN_HEADS, XATTN_HEAD_DIM)
    scores = jnp.einsum('bshd,bmhd->bhsm', q, k).astype(jnp.float32) * (XATTN_HEAD_DIM ** -0.5)
    p = jax.nn.softmax(scores, axis=-1).astype(v.dtype)
    o = jnp.einsum('bhsm,bmhd->bshd', p, v).reshape(bsz, s, D_MODEL)
    return o @ wo


def swiglu(n, w_gate, w_up, w_down):
    return (jax.nn.silu(n @ w_gate) * (n @ w_up)) @ w_down


def setup_inputs(seed: int = 0) -> dict:
    key = jax.random.key(seed)
    ks = jax.random.split(key, 26)
    f32 = jnp.float32

    def dense(k, shape, fan_in):
        return jax.random.normal(k, shape, f32) * (fan_in ** -0.5)

    def gain(k, shape):
        return 1.0 + 0.01 * jax.random.normal(k, shape, f32)

    def small(k, shape):
        return 0.01 * jax.random.normal(k, shape, f32)

    L = DEPTH
    qk_ch = 2 * MLSTM_HEADS * MLSTM_DK
    return {
        "x": jax.random.normal(ks[0], (BATCH, SEQ, D_MODEL), f32),
        "mem": jax.random.normal(ks[1], (BATCH, MEM_TOKENS, D_MODEL), f32),
        "norm_mix_g": gain(ks[2], (L, D_MODEL)),
        "w_in": dense(ks[3], (L, D_MODEL, IN_COLS), D_MODEL),
        "gla_gate_w2": dense(ks[4], (L, GLA_GATE_RANK, GLA_HEADS * GLA_DK), GLA_GATE_RANK),
        "gla_gate_b": small(ks[5], (L, GLA_HEADS * GLA_DK)),
        "gla_norm_g": gain(ks[6], (L, GLA_HEADS, GLA_DV)),
        "mlstm_conv_w": dense(ks[7], (L, CONV_WIDTH, qk_ch), CONV_WIDTH),
        "mlstm_conv_b": small(ks[8], (L, qk_ch)),
        "mlstm_igate_b": small(ks[9], (L, MLSTM_HEADS)),
        "mlstm_fgate_b": jnp.linspace(3.0, 6.0, MLSTM_HEADS, dtype=f32)[None, :]
                          + 0.1 * jax.random.normal(ks[10], (L, MLSTM_HEADS), f32),
        "mlstm_norm_g": gain(ks[11], (L, MLSTM_HEADS, MLSTM_DV)),
        "w_out": dense(ks[12], (L, MIX_WIDTH, D_MODEL), MIX_WIDTH),
        "norm_cross_g": gain(ks[13], (L, D_MODEL)),
        "norm_mem_g": gain(ks[14], (L, D_MODEL)),
        "wq_c": dense(ks[15], (L, D_MODEL, D_MODEL), D_MODEL),
        "wk_c": dense(ks[16], (L, D_MODEL, D_MODEL), D_MODEL),
        "wv_c": dense(ks[17], (L, D_MODEL, D_MODEL), D_MODEL),
        "wo_c": dense(ks[18], (L, D_MODEL, D_MODEL), D_MODEL),
        "norm_ffn_g": gain(ks[19], (L, D_MODEL)),
        "w_gate": dense(ks[20], (L, D_MODEL, FFN_HIDDEN), D_MODEL),
        "w_up": dense(ks[21], (L, D_MODEL, FFN_HIDDEN), D_MODEL),
        "w_down": dense(ks[22], (L, FFN_HIDDEN, D_MODEL), FFN_HIDDEN),
        "norm_final_g": gain(ks[23], (D_MODEL,)),
    }


def reference(x, mem, norm_mix_g, w_in, gla_gate_w2, gla_gate_b, gla_norm_g, mlstm_conv_w,
              mlstm_conv_b, mlstm_igate_b, mlstm_fgate_b, mlstm_norm_g, w_out, norm_cross_g,
              norm_mem_g, wq_c, wk_c, wv_c, wo_c, norm_ffn_g, w_gate, w_up, w_down, norm_final_g):
    h = x
    for l in range(DEPTH):
        h = h + hybrid_mixer(rmsnorm(h, norm_mix_g[l]), w_in[l], gla_gate_w2[l], gla_gate_b[l],
                             gla_norm_g[l], mlstm_conv_w[l], mlstm_conv_b[l], mlstm_igate_b[l],
                             mlstm_fgate_b[l], mlstm_norm_g[l], w_out[l])
        h = h + cross_attention(rmsnorm(h, norm_cross_g[l]), rmsnorm(mem, norm_mem_g[l]),
                                wq_c[l], wk_c[l], wv_c[l], wo_c[l])
        h = h + swiglu(rmsnorm(h, norm_ffn_g[l]), w_gate[l], w_up[l], w_down[l])
    return rmsnorm(h, norm_final_g)
```

```python
import functools
import math

import jax
import jax.numpy as jnp
from jax import lax
from jax.experimental import pallas as pl
from jax.experimental.pallas import tpu as pltpu

F32 = jnp.float32
BF16 = jnp.bfloat16

D_MODEL = 4096
GROUP_WIDTH = D_MODEL // 2
GLA_HEADS = 4
GLA_DV = GROUP_WIDTH // GLA_HEADS
GLA_DK = GLA_DV // 2
GLA_GATE_RANK = 16
GLA_TAU = 16.0
MLSTM_HEADS = 4
MLSTM_DV = GROUP_WIDTH // MLSTM_HEADS
MLSTM_DK = MLSTM_DV // 2
CONV_WIDTH = 4
XATTN_HEADS = 4
XATTN_HEAD_DIM = D_MODEL // XATTN_HEADS
FFN_HIDDEN = 256 * math.ceil(8 * D_MODEL / (3 * 256))
EPS = 1e-6

V7X_VMEM_LIMIT_BYTES = 56 * 1024 * 1024
LANES = 128
SUBLANES = 8

CHUNK = 64
GLA_SUB = 16
SEQ_BLOCK = 256
NORM_ROWS = 256
FFN_PAD = 11264
SMALL_COLS = LANES
IG_COL = GLA_GATE_RANK
FG_COL = GLA_GATE_RANK + MLSTM_HEADS


def _vmem_limit(nbytes):
    return int(min(V7X_VMEM_LIMIT_BYTES, max(32 * 1024 * 1024, nbytes * 5 // 4 + (4 << 20))))


def _dot(a, b):
    return jnp.dot(a, b, preferred_element_type=F32)


def _dot_nt(a, b):
    return lax.dot_general(a, b, (((1,), (1,)), ((), ())), preferred_element_type=F32)


def _dot_tn(a, b):
    return lax.dot_general(a, b, (((0,), (0,)), ((), ())), preferred_element_type=F32)


def _split3(x):
    hi = x.astype(BF16)
    r = x - hi.astype(F32)
    mid = r.astype(BF16)
    lo = (r - mid.astype(F32)).astype(BF16)
    return hi, mid, lo


def _dot_exact_lhs(m_bf16, x):
    hi, mid, lo = _split3(x)
    return _dot(m_bf16, hi) + _dot(m_bf16, mid) + _dot(m_bf16, lo)


def _dot_exact_rhs(x, m_bf16):
    hi, mid, lo = _split3(x)
    return _dot(hi, m_bf16) + _dot(mid, m_bf16) + _dot(lo, m_bf16)


def _log_sigmoid(z):
    return jnp.minimum(z, 0.0) - jnp.log(1.0 + jnp.exp(-jnp.abs(z)))


def _sigmoid(z):
    return 1.0 / (1.0 + jnp.exp(-z))


def _rmsnorm_kernel(x_ref, g_ref, o_ref):
    x = x_ref[...].astype(F32)
    ms = jnp.mean(x * x, axis=-1, keepdims=True)
    o_ref[...] = (x * lax.rsqrt(ms + EPS) * g_ref[...]).astype(o_ref.dtype)


def _rmsnorm(x, g, out_dtype):
    rows, d = x.shape
    tr = min(NORM_ROWS, rows)
    nbytes = 2 * tr * d * (x.dtype.itemsize + jnp.dtype(out_dtype).itemsize) + 3 * tr * d * 4
    return pl.pallas_call(
        _rmsnorm_kernel,
        out_shape=jax.ShapeDtypeStruct((rows, d), out_dtype),
        grid=(rows // tr,),
        in_specs=[pl.BlockSpec((tr, d), lambda i: (i, 0)),
                  pl.BlockSpec((1, d), lambda i: (0, 0))],
        out_specs=pl.BlockSpec((tr, d), lambda i: (i, 0)),
        compiler_params=pltpu.CompilerParams(
            dimension_semantics=("parallel",), vmem_limit_bytes=_vmem_limit(nbytes)),
        name="rmsnorm",
    )(x, g.reshape(1, d).astype(F32))


def _mm_kernel(*refs, n_pairs, nk, has_resid):
    a_refs = refs[:n_pairs]
    w_refs = refs[n_pairs:2 * n_pairs]
    pos = 2 * n_pairs
    r_ref = refs[pos] if has_resid else None
    pos += int(has_resid)
    o_ref = refs[pos]
    acc_ref = refs[pos + 1] if nk > 1 else None

    part = _dot(a_refs[0][...], w_refs[0][...])
    for a_ref, w_ref in zip(a_refs[1:], w_refs[1:]):
        part = part + _dot(a_ref[...], w_ref[...])

    def finish(acc):
        if has_resid:
            acc = r_ref[...] + acc
        o_ref[...] = acc.astype(o_ref.dtype)

    if nk == 1:
        finish(part)
    else:
        k = pl.program_id(2)

        @pl.when(k == 0)
        def _():
            acc_ref[...] = part

        @pl.when(k > 0)
        def _():
            acc_ref[...] += part

        @pl.when(k == nk - 1)
        def _():
            finish(acc_ref[...])


def _matmul(a_list, w_list, out_dtype, *, resid=None, tm, tn, tk=None, name):
    m = a_list[0].shape[0]
    n = w_list[0].shape[1]
    kdim = a_list[0].shape[1]
    tk = kdim if tk is None else tk
    nk = kdim // tk
    assert m % tm == 0 and n % tn == 0 and kdim % tk == 0
    n_pairs = len(a_list)
    in_specs = ([pl.BlockSpec((tm, tk), lambda i, j, k: (i, k)) for _ in a_list]
                + [pl.BlockSpec((tk, tn), lambda i, j, k: (k, j)) for _ in w_list])
    args = list(a_list) + list(w_list)
    nbytes = 2 * n_pairs * (tm * tk + tk * tn) * 2 + 2 * tm * tn * jnp.dtype(out_dtype).itemsize
    nbytes += 2 * tm * tn * 4
    if resid is not None:
        in_specs.append(pl.BlockSpec((tm, tn), lambda i, j, k: (i, j)))
        args.append(resid)
        nbytes += 2 * tm * tn * 4
    scratch = [pltpu.VMEM((tm, tn), F32)] if nk > 1 else []
    return pl.pallas_call(
        functools.partial(_mm_kernel, n_pairs=n_pairs, nk=nk, has_resid=resid is not None),
        out_shape=jax.ShapeDtypeStruct((m, n), out_dtype),
        grid=(m // tm, n // tn, nk),
        in_specs=in_specs,
        out_specs=pl.BlockSpec((tm, tn), lambda i, j, k: (i, j)),
        scratch_shapes=scratch,
        compiler_params=pltpu.CompilerParams(
            dimension_semantics=("parallel", "parallel", "arbitrary"),
            vmem_limit_bytes=_vmem_limit(nbytes)),
        name=name,
    )(*args)


def _swiglu_up_kernel(a_ref, wg_ref, wu_ref, o_ref):
    a = a_ref[...]
    g = _dot(a, wg_ref[...])
    u = _dot(a, wu_ref[...])
    o_ref[...] = (g * _sigmoid(g) * u).astype(o_ref.dtype)


def _swiglu_up(a, wg, wu, *, tm, tn):
    m, kdim = a.shape
    n = wg.shape[1]
    nbytes = 2 * (tm * kdim + 2 * kdim * tn + tm * tn) * 2 + 3 * tm * tn * 4
    return pl.pallas_call(
        _swiglu_up_kernel,
        out_shape=jax.ShapeDtypeStruct((m, n), BF16),
        grid=(m // tm, n // tn),
        in_specs=[pl.BlockSpec((tm, kdim), lambda i, j: (i, 0)),
                  pl.BlockSpec((kdim, tn), lambda i, j: (0, j)),
                  pl.BlockSpec((kdim, tn), lambda i, j: (0, j))],
        out_specs=pl.BlockSpec((tm, tn), lambda i, j: (i, j)),
        compiler_params=pltpu.CompilerParams(
            dimension_semantics=("parallel", "parallel"),
            vmem_limit_bytes=_vmem_limit(nbytes)),
        name="swiglu_up",
    )(a, wg, wu)


def _gla_kernel(q_ref, k_ref, v_ref, g_ref, sm_ref, w2_ref, gb_ref, ng_ref, o_ref, st_ref):
    c_len = CHUNK
    n_sub = c_len // GLA_SUB

    @pl.when(pl.program_id(2) == 0)
    def _():
        st_ref[...] = jnp.zeros_like(st_ref)

    w2_hi, w2_mid, _ = _split3(w2_ref[...])
    gate_b = gb_ref[...]
    norm_g = ng_ref[...]
    row_c = lax.broadcasted_iota(jnp.int32, (c_len, c_len), 0)
    col_c = lax.broadcasted_iota(jnp.int32, (c_len, c_len), 1)
    tril = (row_c >= col_c).astype(BF16)
    sub_row = lax.broadcasted_iota(jnp.int32, (GLA_SUB, GLA_DK), 0)
    pan_col = lax.broadcasted_iota(jnp.int32, (GLA_SUB, c_len), 1)
    row_k = lax.broadcasted_iota(jnp.int32, (c_len, GLA_DK), 0)

    def chunk(c, carry):
        r0 = pl.multiple_of(c * c_len, c_len)
        rows = pl.ds(r0, c_len)
        q = q_ref[rows, :].astype(F32) * (GLA_DK ** -0.5)
        k = k_ref[rows, :].astype(F32)
        v = v_ref[rows, :]
        ag_hi, ag_mid, _ = _split3(sm_ref[rows, :])
        z = _dot(ag_hi, w2_hi) + _dot(ag_mid, w2_hi) + _dot(ag_hi, w2_mid) + gate_b
        log_a = _log_sigmoid(z) * (1.0 / GLA_TAU)
        b = _dot_exact_lhs(tril, log_a)
        b_last = b[c_len - 1:c_len, :]

        panels = []
        for i in range(n_sub):
            lo = i * GLA_SUB
            bi = b[lo:lo + GLA_SUB, :]
            qi = q[lo:lo + GLA_SUB, :]
            ki = k[lo:lo + GLA_SUB, :]
            pan = jnp.zeros((GLA_SUB, c_len), F32)
            for s in range(GLA_SUB):
                diff = jnp.where(sub_row >= s, bi - bi[s:s + 1, :], -jnp.inf)
                colv = jnp.sum(qi * ki[s:s + 1, :] * jnp.exp(diff), axis=1, keepdims=True)
                pan = jnp.where(pan_col == lo + s, colv, pan)
            if i > 0:
                b_ref = b[lo - 1:lo, :]
                k_prev = k * jnp.exp(jnp.where(row_k < lo, b_ref - b, -jnp.inf))
                q_hat = qi * jnp.exp(bi - b_ref)
                pan = pan + _dot_nt(q_hat.astype(BF16), k_prev.astype(BF16))
            panels.append(pan)
        attn = jnp.concatenate(panels, axis=0)

        st = st_ref[...]
        q_dec = (q * jnp.exp(b)).astype(BF16)
        out = _dot(attn.astype(BF16), v) + _dot_nt(q_dec, st.astype(BF16))

        k_dec = (k * jnp.exp(b_last - b)).astype(BF16)
        st_ref[...] = st * jnp.exp(b_last) + _dot_tn(v, k_dec)

        ms = jnp.mean(out * out, axis=-1, keepdims=True)
        y = out * lax.rsqrt(ms + EPS) * norm_g
        gg = g_ref[rows, :].astype(F32)
        o_ref[rows, :] = (y * (gg * _sigmoid(gg))).astype(o_ref.dtype)
        return carry

    lax.fori_loop(0, SEQ_BLOCK // c_len, chunk, 0)


def _gla(proj, small, w2, gate_b, norm_g, *, batch, seq):
    t = proj.shape[0]
    lb = SEQ_BLOCK
    nsb = seq // lb
    h_ = GLA_HEADS
    qb = GLA_DK
    vb = GLA_DV

    def rows(b, h, i):
        return b * nsb + i

    in_specs = [
        pl.BlockSpec((lb, qb), lambda b, h, i: (rows(b, h, i), h)),
        pl.BlockSpec((lb, qb), lambda b, h, i: (rows(b, h, i), h_ + h)),
        pl.BlockSpec((lb, vb), lambda b, h, i: (rows(b, h, i), (2 * h_ * qb) // vb + h)),
        pl.BlockSpec((lb, vb), lambda b, h, i: (rows(b, h, i), (2 * h_ * qb) // vb + h_ + h)),
        pl.BlockSpec((lb, SMALL_COLS), lambda b, h, i: (rows(b, h, i), 0)),
        pl.BlockSpec((None, SMALL_COLS, qb), lambda b, h, i: (h, 0, 0)),
        pl.BlockSpec((None, 1, qb), lambda b, h, i: (h, 0, 0)),
        pl.BlockSpec((None, 1, vb), lambda b, h, i: (h, 0, 0)),
    ]
    return pl.pallas_call(
        _gla_kernel,
        out_shape=jax.ShapeDtypeStruct((t, h_ * vb), BF16),
        grid=(batch, h_, nsb),
        in_specs=in_specs,
        out_specs=pl.BlockSpec((lb, vb), lambda b, h, i: (rows(b, h, i), h)),
        scratch_shapes=[pltpu.VMEM((vb, qb), F32)],
        compiler_params=pltpu.CompilerParams(
            dimension_semantics=("parallel", "parallel", "arbitrary"),
            vmem_limit_bytes=_vmem_limit(16 << 20)),
        name="gla_scan",
    )(proj, proj, proj, proj, small, w2, gate_b, norm_g)


def _mlstm_kernel(gbias_ref, q_ref, k_ref, v_ref, og_ref, sm_ref, igt_ref, cwq_ref, cwk_ref,
                  cbq_ref, cbk_ref, ng_ref, o_ref,
                  ct_ref, n_ref, m_ref, xq_ref, xk_ref, qc_ref, kc_ref):
    c_len = CHUNK
    lb = SEQ_BLOCK
    pad = SUBLANES
    h = pl.program_id(1)

    @pl.when(pl.program_id(2) == 0)
    def _():
        ct_ref[...] = jnp.zeros_like(ct_ref)
        n_ref[...] = jnp.zeros_like(n_ref)
        m_ref[...] = jnp.zeros_like(m_ref)
        xq_ref[0:pad, :] = jnp.zeros((pad, MLSTM_DK), F32)
        xk_ref[0:pad, :] = jnp.zeros((pad, MLSTM_DK), F32)

    def conv_silu(x_ref, src_ref, w_ref, b_ref, dst_ref, scale):
        x_ref[pad:pad + lb, :] = src_ref[...].astype(F32)
        y = b_ref[...]
        for j in range(CONV_WIDTH):
            y = y + w_ref[j:j + 1, :] * x_ref[pl.ds(pad - (CONV_WIDTH - 1) + j, lb), :]
        x_ref[0:pad, :] = x_ref[lb:lb + pad, :]
        dst_ref[...] = y * _sigmoid(y) * scale

    conv_silu(xq_ref, q_ref, cwq_ref, cbq_ref, qc_ref, 1.0)
    conv_silu(xk_ref, k_ref, cwk_ref, cbk_ref, kc_ref, MLSTM_DK ** -0.5)

    ib = gbias_ref[h]
    fb = gbias_ref[MLSTM_HEADS + h]
    norm_g = ng_ref[...]
    row_c = lax.broadcasted_iota(jnp.int32, (c_len, c_len), 0)
    col_c = lax.broadcasted_iota(jnp.int32, (c_len, c_len), 1)
    causal = row_c >= col_c
    tril = causal.astype(BF16)
    triu = (row_c <= col_c).astype(BF16)
    lane_s = lax.broadcasted_iota(jnp.int32, (c_len, SMALL_COLS), 1)

    def chunk(c, carry):
        n_prev, m_prev = carry
        r0 = pl.multiple_of(c * c_len, c_len)
        rows = pl.ds(r0, c_len)
        q = qc_ref[rows, :]
        k = kc_ref[rows, :]
        v = v_ref[rows, :]
        q16 = q.astype(BF16)

        sm = sm_ref[rows, :]
        i_col = jnp.sum(jnp.where(lane_s == IG_COL + h, sm, 0.0), axis=1, keepdims=True) + ib
        f_col = jnp.sum(jnp.where(lane_s == FG_COL + h, sm, 0.0), axis=1, keepdims=True) + fb
        i_row = igt_ref[c, pl.ds(h, 1), :] + ib
        f_row = igt_ref[c, pl.ds(MLSTM_HEADS + h, 1), :] + fb
        lf_col = _log_sigmoid(f_col)
        lf_row = _log_sigmoid(f_row)
        bb_t = _dot_exact_lhs(tril, jnp.broadcast_to(lf_col, (c_len, c_len)))
        bb_s = _dot_exact_rhs(jnp.broadcast_to(lf_row, (c_len, c_len)), triu)
        bb_col = bb_t[:, 0:1]
        b_last = bb_t[c_len - 1:c_len, 0:1]

        log_d = jnp.where(causal, bb_t - bb_s + i_row, -jnp.inf)
        inter = bb_col + m_prev
        m = jnp.maximum(inter, jnp.max(log_d, axis=1, keepdims=True))
        g_inter = jnp.exp(inter - m)
        scores = _dot_nt(q16, k.astype(BF16)) * jnp.exp(log_d - m)
        ct = ct_ref[...]
        num = _dot(scores.astype(BF16), v) + g_inter * _dot_nt(q16, ct.astype(BF16))
        den = (jnp.sum(scores, axis=1, keepdims=True)
               + g_inter * jnp.sum(q * n_prev, axis=1, keepdims=True))
        hid = num / jnp.maximum(jnp.abs(den), jnp.exp(-m))

        log_w = b_last - bb_col + i_col
        m_new = jnp.maximum(b_last + m_prev, jnp.max(log_w, axis=0, keepdims=True))
        w = jnp.exp(log_w - m_new)
        decay = jnp.exp(b_last + m_prev - m_new)
        kw = k * w
        ct_ref[...] = decay * ct + _dot_tn(v, kw.astype(BF16))
        n_new = decay * n_prev + jnp.sum(kw, axis=0, keepdims=True)

        ms = jnp.mean(hid * hid, axis=-1, keepdims=True)
        y = hid * lax.rsqrt(ms + EPS) * norm_g
        og = og_ref[rows, :].astype(F32)
        o_ref[rows, :] = (_sigmoid(og) * y).astype(o_ref.dtype)
        return n_new, m_new

    n_fin, m_fin = lax.fori_loop(0, lb // c_len, chunk, (n_ref[...], m_ref[...]))
    n_ref[...] = n_fin
    m_ref[...] = m_fin


def _mlstm(proj, small, ig_t, gate_bias, conv_w, conv_b, norm_g, *, batch, seq, col0):
    t = proj.shape[0]
    lb = SEQ_BLOCK
    nsb = seq // lb
    h_ = MLSTM_HEADS
    qb = MLSTM_DK
    vb = MLSTM_DV
    q0 = col0 // qb
    v0 = (col0 + 2 * h_ * qb) // vb

    def rows(b, h, i):
        return b * nsb + i

    in_specs = [
        pl.BlockSpec(memory_space=pltpu.SMEM),
        pl.BlockSpec((lb, qb), lambda b, h, i: (rows(b, h, i), q0 + h)),
        pl.BlockSpec((lb, qb), lambda b, h, i: (rows(b, h, i), q0 + h_ + h)),
        pl.BlockSpec((lb, vb), lambda b, h, i: (rows(b, h, i), v0 + h)),
        pl.BlockSpec((lb, vb), lambda b, h, i: (rows(b, h, i), v0 + h_ + h)),
        pl.BlockSpec((lb, SMALL_COLS), lambda b, h, i: (rows(b, h, i), 0)),
        pl.BlockSpec((lb // CHUNK, 2 * h_, CHUNK), lambda b, h, i: (rows(b, h, i), 0, 0)),
        pl.BlockSpec((CONV_WIDTH, qb), lambda b, h, i: (0, h)),
        pl.BlockSpec((CONV_WIDTH, qb), lambda b, h, i: (0, h_ + h)),
        pl.BlockSpec((1, qb), lambda b, h, i: (0, h)),
        pl.BlockSpec((1, qb), lambda b, h, i: (0, h_ + h)),
        pl.BlockSpec((None, 1, vb), lambda b, h, i: (h, 0, 0)),
    ]
    scratch = [
        pltpu.VMEM((vb, qb), F32),
        pltpu.VMEM((1, qb), F32),
        pltpu.VMEM((1, 1), F32),
        pltpu.VMEM((lb + SUBLANES, qb), F32),
        pltpu.VMEM((lb + SUBLANES, qb), F32),
        pltpu.VMEM((lb, qb), F32),
        pltpu.VMEM((lb, qb), F32),
    ]
    return pl.pallas_call(
        _mlstm_kernel,
        out_shape=jax.ShapeDtypeStruct((t, h_ * vb), BF16),
        grid=(batch, h_, nsb),
        in_specs=in_specs,
        out_specs=pl.BlockSpec((lb, vb), lambda b, h, i: (rows(b, h, i), h)),
        scratch_shapes=scratch,
        compiler_params=pltpu.CompilerParams(
            dimension_semantics=("parallel", "parallel", "arbitrary"),
            vmem_limit_bytes=_vmem_limit(16 << 20)),
        name="mlstm_scan",
    )(gate_bias, proj, proj, proj, proj, small, ig_t, conv_w, conv_w, conv_b, conv_b, norm_g)


def _xattn_kernel(q_ref, k_ref, v_ref, o_ref):
    s = _dot_nt(q_ref[...], k_ref[...]) * (XATTN_HEAD_DIM ** -0.5)
    p = jnp.exp(s - jnp.max(s, axis=-1, keepdims=True))
    p = p / jnp.sum(p, axis=-1, keepdims=True)
    o_ref[...] = _dot(p.astype(BF16), v_ref[...]).astype(o_ref.dtype)


def _xattn(q, k, v, *, batch, seq, tq):
    t, d = q.shape
    mem = k.shape[0] // batch
    hd = XATTN_HEAD_DIM
    nq = seq // tq
    return pl.pallas_call(
        _xattn_kernel,
        out_shape=jax.ShapeDtypeStruct((t, d), BF16),
        grid=(batch, nq, XATTN_HEADS),
        in_specs=[pl.BlockSpec((tq, hd), lambda b, i, h: (b * nq + i, h)),
                  pl.BlockSpec((mem, hd), lambda b, i, h: (b, h)),
                  pl.BlockSpec((mem, hd), lambda b, i, h: (b, h))],
        out_specs=pl.BlockSpec((tq, hd), lambda b, i, h: (b * nq + i, h)),
        compiler_params=pltpu.CompilerParams(
            dimension_semantics=("parallel", "parallel", "parallel"),
            vmem_limit_bytes=_vmem_limit(24 << 20)),
        name="cross_attention",
    )(q, k, v)


def kernel(x, mem, norm_mix_g, w_in, gla_gate_w2, gla_gate_b, gla_norm_g, mlstm_conv_w, mlstm_conv_b, mlstm_igate_b, mlstm_fgate_b, mlstm_norm_g, w_out, norm_cross_g, norm_mem_g, wq_c, wk_c, wv_c, wo_c, norm_ffn_g, w_gate, w_up, w_down, norm_final_g):
    batch, seq, d = x.shape
    mem_tokens = mem.shape[1]
    t = batch * seq
    depth = w_in.shape[0]
    gla_cols = 2 * GLA_HEADS * GLA_DK + 2 * GLA_HEADS * GLA_DV
    ml_cols = 2 * MLSTM_HEADS * MLSTM_DK + 2 * MLSTM_HEADS * MLSTM_DV
    ml_start = gla_cols + GLA_GATE_RANK

    h = x.reshape(t, d)
    mem2 = mem.reshape(batch * mem_tokens, d)
    for l in range(depth):
        w_main = jnp.concatenate(
            [w_in[l][:, :gla_cols], w_in[l][:, ml_start:ml_start + ml_cols]], axis=1).astype(BF16)
        w_small = jnp.concatenate(
            [w_in[l][:, gla_cols:ml_start], w_in[l][:, ml_start + ml_cols:]], axis=1)
        w_small = jnp.pad(w_small, ((0, 0), (0, SMALL_COLS - w_small.shape[1]))).astype(BF16)
        w2 = gla_gate_w2[l].reshape(GLA_GATE_RANK, GLA_HEADS, GLA_DK).transpose(1, 0, 2)
        w2 = jnp.pad(w2, ((0, 0), (0, SMALL_COLS - GLA_GATE_RANK), (0, 0)))
        gate_b = gla_gate_b[l].reshape(GLA_HEADS, 1, GLA_DK)
        gla_g = gla_norm_g[l].reshape(GLA_HEADS, 1, GLA_DV)
        ml_g = mlstm_norm_g[l].reshape(MLSTM_HEADS, 1, MLSTM_DV)
        gate_bias = jnp.concatenate([mlstm_igate_b[l], mlstm_fgate_b[l]]).astype(F32)
        conv_b = mlstm_conv_b[l].reshape(1, -1)
        w_o = w_out[l].astype(BF16)
        pad_f = FFN_PAD - FFN_HIDDEN
        w_g = jnp.pad(w_gate[l], ((0, 0), (0, pad_f))).astype(BF16)
        w_u = jnp.pad(w_up[l], ((0, 0), (0, pad_f))).astype(BF16)
        w_d = jnp.pad(w_down[l], ((0, pad_f), (0, 0))).astype(BF16)

        n1 = _rmsnorm(h, norm_mix_g[l], BF16)
        proj = _matmul([n1], [w_main], BF16, tm=1024, tn=1024, name="in_proj")
        small = _matmul([n1], [w_small], F32, tm=1024, tn=SMALL_COLS, name="in_proj_small")
        ig_t = small[:, IG_COL:IG_COL + 2 * MLSTM_HEADS].reshape(
            t // CHUNK, CHUNK, 2 * MLSTM_HEADS).transpose(0, 2, 1)
        o_gla = _gla(proj, small, w2, gate_b, gla_g, batch=batch, seq=seq)
        o_ml = _mlstm(proj, small, ig_t, gate_bias, mlstm_conv_w[l], conv_b, ml_g,
                      batch=batch, seq=seq, col0=gla_cols)
        h = _matmul([o_gla, o_ml], [w_o[:GROUP_WIDTH], w_o[GROUP_WIDTH:]], F32, resid=h,
                    tm=1024, tn=512, name="out_proj")

        n2 = _rmsnorm(h, norm_cross_g[l], BF16)
        mem_n = _rmsnorm(mem2, norm_mem_g[l], BF16)
        q = _matmul([n2], [wq_c[l].astype(BF16)], BF16, tm=1024, tn=1024, name="xattn_q")
        k_mem = _matmul([mem_n], [wk_c[l].astype(BF16)], BF16, tm=512, tn=1024, name="xattn_k")
        v_mem = _matmul([mem_n], [wv_c[l].astype(BF16)], BF16, tm=512, tn=1024, name="xattn_v")
        attn = _xattn(q, k_mem, v_mem, batch=batch, seq=seq, tq=1024)
        h = _matmul([attn], [wo_c[l].astype(BF16)], F32, resid=h, tm=1024, tn=512, name="xattn_o")

        n3 = _rmsnorm(h, norm_ffn_g[l], BF16)
        hid = _swiglu_up(n3, w_g, w_u, tm=1024, tn=512)
        h = _matmul([hid], [w_d], F32, resid=h, tm=1024, tn=1024, tk=FFN_PAD // 4, name="ffn_down")

    y = _rmsnorm(h, norm_final_g, x.dtype)
    return y.reshape(batch, seq, d)
```

```python
import functools
import math

import jax
import jax.numpy as jnp
import numpy as np
from jax import lax
from jax.experimental import pallas as pl
from jax.experimental.pallas import tpu as pltpu

F32 = jnp.float32
BF16 = jnp.bfloat16

D_MODEL = 4096
GROUP_WIDTH = D_MODEL // 2
GLA_HEADS = 4
GLA_DV = GROUP_WIDTH // GLA_HEADS
GLA_DK = GLA_DV // 2
GLA_GATE_RANK = 16
GLA_TAU = 16.0
MLSTM_HEADS = 4
MLSTM_DV = GROUP_WIDTH // MLSTM_HEADS
MLSTM_DK = MLSTM_DV // 2
CONV_WIDTH = 4
XATTN_HEADS = 4
XATTN_HEAD_DIM = D_MODEL // XATTN_HEADS
FFN_HIDDEN = 256 * math.ceil(8 * D_MODEL / (3 * 256))
EPS = 1e-6

V7X_VMEM_LIMIT_BYTES = 56 * 1024 * 1024
LANES = 128
SUBLANES = 8

CHUNK = 64
GLA_LEVELS = (32, 16, 8, 4, 2, 1)
LOG2_E = math.log2(math.e)
SEQ_BLOCK = 256
NORM_ROWS = 256
FFN_TN = 256
SMALL_COLS = LANES
IG_COL = GLA_GATE_RANK
FG_COL = GLA_GATE_RANK + MLSTM_HEADS


def _vmem_limit(nbytes):
    return int(min(V7X_VMEM_LIMIT_BYTES, max(32 * 1024 * 1024, nbytes * 5 // 4 + (4 << 20))))


def _dot(a, b):
    return jnp.dot(a, b, preferred_element_type=F32)


def _dot_nt(a, b):
    return lax.dot_general(a, b, (((1,), (1,)), ((), ())), preferred_element_type=F32)


def _dot_tn(a, b):
    return lax.dot_general(a, b, (((0,), (0,)), ((), ())), preferred_element_type=F32)


def _split3(x):
    hi = x.astype(BF16)
    r = x - hi.astype(F32)
    mid = r.astype(BF16)
    lo = (r - mid.astype(F32)).astype(BF16)
    return hi, mid, lo


def _dot_exact_lhs(m_bf16, x):
    hi, mid, lo = _split3(x)
    return _dot(m_bf16, hi) + _dot(m_bf16, mid) + _dot(m_bf16, lo)


def _dot_exact_rhs(x, m_bf16):
    hi, mid, lo = _split3(x)
    return _dot(hi, m_bf16) + _dot(mid, m_bf16) + _dot(lo, m_bf16)


def _log_sigmoid(z):
    return jnp.minimum(z, 0.0) - jnp.log(1.0 + jnp.exp(-jnp.abs(z)))


def _sigmoid(z):
    return 1.0 / (1.0 + jnp.exp(-z))


def _rmsnorm_kernel(x_ref, g_ref, o_ref):
    x = x_ref[...].astype(F32)
    ms = jnp.mean(x * x, axis=-1, keepdims=True)
    o_ref[...] = (x * lax.rsqrt(ms + EPS) * g_ref[...]).astype(o_ref.dtype)


def _rmsnorm(x, g, out_dtype):
    rows, d = x.shape
    tr = min(NORM_ROWS, rows)
    nbytes = 2 * tr * d * (x.dtype.itemsize + jnp.dtype(out_dtype).itemsize) + 3 * tr * d * 4
    return pl.pallas_call(
        _rmsnorm_kernel,
        out_shape=jax.ShapeDtypeStruct((rows, d), out_dtype),
        grid=(rows // tr,),
        in_specs=[pl.BlockSpec((tr, d), lambda i: (i, 0)),
                  pl.BlockSpec((1, d), lambda i: (0, 0))],
        out_specs=pl.BlockSpec((tr, d), lambda i: (i, 0)),
        compiler_params=pltpu.CompilerParams(
            dimension_semantics=("parallel",), vmem_limit_bytes=_vmem_limit(nbytes)),
        name="rmsnorm",
    )(x, g.reshape(1, d).astype(F32))


def _mm_kernel(*refs, n_pairs, nk, has_resid):
    a_refs = refs[:n_pairs]
    w_refs = refs[n_pairs:2 * n_pairs]
    pos = 2 * n_pairs
    r_ref = refs[pos] if has_resid else None
    pos += int(has_resid)
    o_ref = refs[pos]
    acc_ref = refs[pos + 1] if nk > 1 else None

    part = _dot(a_refs[0][...], w_refs[0][...])
    for a_ref, w_ref in zip(a_refs[1:], w_refs[1:]):
        part = part + _dot(a_ref[...], w_ref[...])

    def finish(acc):
        if has_resid:
            acc = r_ref[...] + acc
        o_ref[...] = acc.astype(o_ref.dtype)

    if nk == 1:
        finish(part)
    else:
        k = pl.program_id(2)

        @pl.when(k == 0)
        def _():
            acc_ref[...] = part

        @pl.when(k > 0)
        def _():
            acc_ref[...] += part

        @pl.when(k == nk - 1)
        def _():
            finish(acc_ref[...])


def _matmul(a_list, w_list, out_dtype, *, resid=None, tm, tn, tk=None, name):
    m = a_list[0].shape[0]
    n = w_list[0].shape[1]
    kdim = a_list[0].shape[1]
    tk = kdim if tk is None else tk
    nk = kdim // tk
    assert m % tm == 0 and n % tn == 0 and kdim % tk == 0
    n_pairs = len(a_list)
    in_specs = ([pl.BlockSpec((tm, tk), lambda i, j, k: (i, k)) for _ in a_list]
                + [pl.BlockSpec((tk, tn), lambda i, j, k: (k, j)) for _ in w_list])
    args = list(a_list) + list(w_list)
    nbytes = 2 * n_pairs * (tm * tk + tk * tn) * 2 + 2 * tm * tn * jnp.dtype(out_dtype).itemsize
    nbytes += 2 * tm * tn * 4
    if resid is not None:
        in_specs.append(pl.BlockSpec((tm, tn), lambda i, j, k: (i, j)))
        args.append(resid)
        nbytes += 2 * tm * tn * 4
    scratch = [pltpu.VMEM((tm, tn), F32)] if nk > 1 else []
    return pl.pallas_call(
        functools.partial(_mm_kernel, n_pairs=n_pairs, nk=nk, has_resid=resid is not None),
        out_shape=jax.ShapeDtypeStruct((m, n), out_dtype),
        grid=(m // tm, n // tn, nk),
        in_specs=in_specs,
        out_specs=pl.BlockSpec((tm, tn), lambda i, j, k: (i, j)),
        scratch_shapes=scratch,
        compiler_params=pltpu.CompilerParams(
            dimension_semantics=("parallel", "parallel", "arbitrary"),
            vmem_limit_bytes=_vmem_limit(nbytes)),
        name=name,
    )(*args)


def _swiglu_up_kernel(a_ref, wg_ref, wu_ref, o_ref):
    a = a_ref[...]
    g = _dot(a, wg_ref[...])
    u = _dot(a, wu_ref[...])
    o_ref[...] = (g * _sigmoid(g) * u).astype(o_ref.dtype)


def _swiglu_up(a, wg, wu, *, tm, tn):
    m, kdim = a.shape
    n = wg.shape[1]
    nbytes = 2 * (tm * kdim + 2 * kdim * tn + tm * tn) * 2 + 3 * tm * tn * 4
    return pl.pallas_call(
        _swiglu_up_kernel,
        out_shape=jax.ShapeDtypeStruct((m, n), BF16),
        grid=(m // tm, n // tn),
        in_specs=[pl.BlockSpec((tm, kdim), lambda i, j: (i, 0)),
                  pl.BlockSpec((kdim, tn), lambda i, j: (0, j)),
                  pl.BlockSpec((kdim, tn), lambda i, j: (0, j))],
        out_specs=pl.BlockSpec((tm, tn), lambda i, j: (i, j)),
        compiler_params=pltpu.CompilerParams(
            dimension_semantics=("parallel", "parallel"),
            vmem_limit_bytes=_vmem_limit(nbytes)),
        name="swiglu_up",
    )(a, wg, wu)


def _gla_tables():
    c = CHUNK
    row = np.arange(c)[:, None]
    u = np.arange(c)[None, :]
    sel = [u <= row, u > row]
    masks = [row == u]
    for bsz in GLA_LEVELS:
        r = (row // (2 * bsz)) * (2 * bsz) + bsz - 1
        sel.append(np.where(row <= r, (u > row) & (u <= r), (u > r) & (u <= row)))
        same_pair = (row // (2 * bsz)) == (u // (2 * bsz))
        masks.append(same_pair & ((row % (2 * bsz)) >= bsz) & ((u % (2 * bsz)) < bsz))
    sel = np.concatenate(sel, axis=0).astype(np.float32)
    return (jnp.asarray(np.concatenate([sel] * 3, axis=1), BF16),
            jnp.asarray(np.stack(masks).astype(np.float32)))


def _gla_kernel(q_ref, k_ref, v_ref, g_ref, sm_ref, w2_ref, gb_ref, ng_ref, sel_ref, mask_ref,
                o_ref, st_ref):
    c_len = CHUNK

    @pl.when(pl.program_id(2) == 0)
    def _():
        st_ref[...] = jnp.zeros_like(st_ref)

    norm_g = ng_ref[...]
    z = _dot(sm_ref[...].astype(BF16), w2_ref[...].astype(BF16)) + gb_ref[...]
    la_hi, la_mid, la_lo = _split3(_log_sigmoid(z) * (LOG2_E / GLA_TAU))
    row_k = lax.broadcasted_iota(jnp.int32, (c_len, GLA_DK), 0)

    st = st_ref[...]
    for c in range(SEQ_BLOCK // c_len):
        rows = slice(c * c_len, (c + 1) * c_len)
        q = q_ref[rows, :].astype(F32) * (GLA_DK ** -0.5)
        k = k_ref[rows, :].astype(F32)
        v = v_ref[rows, :]
        la3 = jnp.concatenate([la_hi[rows, :], la_mid[rows, :], la_lo[rows, :]], axis=0)
        e = jnp.exp2(_dot(sel_ref[...], la3))
        q_dec = (q * e[0:c_len, :]).astype(BF16)
        k_dec = (k * e[c_len:2 * c_len, :]).astype(BF16)
        decay = e[c_len - 1:c_len, :]

        attn = _dot_nt(q.astype(BF16), k.astype(BF16)) * mask_ref[0]
        for lv, bsz in enumerate(GLA_LEVELS):
            in_right = (row_k & bsz) != 0
            x = (jnp.where(in_right, q, k) * e[(2 + lv) * c_len:(3 + lv) * c_len, :]).astype(BF16)
            attn = attn + _dot_nt(x, x) * mask_ref[1 + lv]

        out = _dot(attn.astype(BF16), v) + _dot_nt(q_dec, st.astype(BF16))
        st = st * decay + _dot_tn(v, k_dec)

        ms = jnp.mean(out * out, axis=-1, keepdims=True)
        y = out * lax.rsqrt(ms + EPS) * norm_g
        gg = g_ref[rows, :].astype(F32)
        o_ref[rows, :] = (y * (gg * _sigmoid(gg))).astype(o_ref.dtype)
    st_ref[...] = st


def _gla(proj, small, w2, gate_b, norm_g, *, batch, seq):
    decay_sel, pair_mask = _gla_tables()
    t = proj.shape[0]
    lb = SEQ_BLOCK
    nsb = seq // lb
    h_ = GLA_HEADS
    qb = GLA_DK
    vb = GLA_DV

    def rows(b, h, i):
        return b * nsb + i

    in_specs = [
        pl.BlockSpec((lb, qb), lambda b, h, i: (rows(b, h, i), h)),
        pl.BlockSpec((lb, qb), lambda b, h, i: (rows(b, h, i), h_ + h)),
        pl.BlockSpec((lb, vb), lambda b, h, i: (rows(b, h, i), (2 * h_ * qb) // vb + h)),
        pl.BlockSpec((lb, vb), lambda b, h, i: (rows(b, h, i), (2 * h_ * qb) // vb + h_ + h)),
        pl.BlockSpec((lb, SMALL_COLS), lambda b, h, i: (rows(b, h, i), 0)),
        pl.BlockSpec((None, SMALL_COLS, qb), lambda b, h, i: (h, 0, 0)),
        pl.BlockSpec((None, 1, qb), lambda b, h, i: (h, 0, 0)),
        pl.BlockSpec((None, 1, vb), lambda b, h, i: (h, 0, 0)),
        pl.BlockSpec(decay_sel.shape, lambda b, h, i: (0, 0)),
        pl.BlockSpec(pair_mask.shape, lambda b, h, i: (0, 0, 0)),
    ]
    return pl.pallas_call(
        _gla_kernel,
        out_shape=jax.ShapeDtypeStruct((t, h_ * vb), BF16),
        grid=(batch, h_, nsb),
        in_specs=in_specs,
        out_specs=pl.BlockSpec((lb, vb), lambda b, h, i: (rows(b, h, i), h)),
        scratch_shapes=[pltpu.VMEM((vb, qb), F32)],
        compiler_params=pltpu.CompilerParams(
            dimension_semantics=("parallel", "parallel", "arbitrary"),
            vmem_limit_bytes=_vmem_limit(16 << 20)),
        name="gla_scan",
    )(proj, proj, proj, proj, small, w2, gate_b, norm_g, decay_sel, pair_mask)


def _mlstm_tables():
    lb, c = SEQ_BLOCK, CHUNK
    t = np.arange(lb)[:, None]
    u = np.arange(lb)[None, :]
    shift = np.concatenate([(u == t - d) for d in range(1, CONV_WIDTH)], axis=0).astype(np.float32)
    tc = np.arange(c)[:, None]
    uc = np.arange(c)[None, :]
    cum = np.concatenate([uc <= tc, uc > tc], axis=0).astype(np.float32)
    col = np.concatenate([tc > uc, np.ones((c, c), bool)], axis=1).astype(np.float32)
    return (jnp.asarray(shift, BF16), jnp.asarray(np.concatenate([cum] * 3, axis=1), BF16),
            jnp.asarray(col))


def _mlstm_kernel(gbias_ref, q_ref, k_ref, v_ref, og_ref, sm_ref, igt_ref, cwq_ref, cwk_ref,
                  cbq_ref, cbk_ref, ng_ref, shift_ref, cum_ref, col_ref, o_ref,
                  ct_ref, n_ref, m_ref, hist_ref):
    c_len = CHUNK
    lb = SEQ_BLOCK
    taps = CONV_WIDTH
    h = pl.program_id(1)

    @pl.when(pl.program_id(2) == 0)
    def _():
        ct_ref[...] = jnp.zeros_like(ct_ref)
        n_ref[...] = jnp.zeros_like(n_ref)
        m_ref[...] = jnp.zeros_like(m_ref)
        hist_ref[...] = jnp.zeros_like(hist_ref)

    x16 = jnp.concatenate([q_ref[...], k_ref[...]], axis=1)
    x = x16.astype(F32)
    shifted = _dot(shift_ref[...], x16)
    head = jnp.concatenate([hist_ref[...], x[0:SUBLANES, :]], axis=0)
    w = jnp.concatenate([cwq_ref[...], cwk_ref[...]], axis=1)
    y = jnp.concatenate([cbq_ref[...], cbk_ref[...]], axis=1) + w[taps - 1:taps, :] * x
    for d in range(1, taps):
        sh = jnp.concatenate([head[SUBLANES - d:2 * SUBLANES - d, :],
                              shifted[(d - 1) * lb + SUBLANES:d * lb, :]], axis=0)
        y = y + w[taps - 1 - d:taps - d, :] * sh
    hist_ref[...] = x[lb - SUBLANES:lb, :]
    act = y * _sigmoid(y)
    q_all = act[:, 0:MLSTM_DK]
    k_all = act[:, MLSTM_DK:2 * MLSTM_DK] * (MLSTM_DK ** -0.5)

    ib = gbias_ref[h]
    fb = gbias_ref[MLSTM_HEADS + h]
    sm = sm_ref[...]
    lane_s = lax.broadcasted_iota(jnp.int32, (lb, SMALL_COLS), 1)
    i_col_all = jnp.sum(jnp.where(lane_s == IG_COL + h, sm, 0.0), axis=1, keepdims=True) + ib
    f_col_all = jnp.sum(jnp.where(lane_s == FG_COL + h, sm, 0.0), axis=1, keepdims=True) + fb
    lf_col_all = _log_sigmoid(f_col_all)

    norm_g = ng_ref[...]
    row_c = lax.broadcasted_iota(jnp.int32, (c_len, c_len), 0)
    col_c = lax.broadcasted_iota(jnp.int32, (c_len, c_len), 1)
    causal = row_c >= col_c

    ct = ct_ref[...]
    n_prev = n_ref[...]
    m_prev = m_ref[...]
    for c in range(lb // c_len):
        rows = slice(c * c_len, (c + 1) * c_len)
        q = q_all[rows, :]
        k = k_all[rows, :]
        v = v_ref[rows, :]
        q16 = q.astype(BF16)
        i_col = i_col_all[rows, :]
        i_row = igt_ref[c, pl.ds(h, 1), :] + ib

        g_hi, g_mid, g_lo = _split3(lf_col_all[rows, :] * col_ref[...])
        sums = _dot(cum_ref[...], jnp.concatenate([g_hi, g_mid, g_lo], axis=0))
        d0 = sums[0:c_len, 0:c_len]
        bb_col = sums[0:c_len, c_len:c_len + 1]
        rem_col = sums[c_len:2 * c_len, c_len:c_len + 1]
        b_last = bb_col[c_len - 1:c_len, :]

        log_d = jnp.where(causal, d0 + i_row, -jnp.inf)
        inter = bb_col + m_prev
        m = jnp.maximum(inter, jnp.max(log_d, axis=1, keepdims=True))
        g_inter = jnp.exp(inter - m)
        scores = _dot_nt(q16, k.astype(BF16)) * jnp.exp(log_d - m)
        num = _dot(scores.astype(BF16), v) + g_inter * _dot_nt(q16, ct.astype(BF16))
        den = (jnp.sum(scores, axis=1, keepdims=True)
               + g_inter * jnp.sum(q * n_prev, axis=1, keepdims=True))
        hid = num / jnp.maximum(jnp.abs(den), jnp.exp(-m))

        log_w = rem_col + i_col
        m_new = jnp.maximum(b_last + m_prev, jnp.max(log_w, axis=0, keepdims=True))
        w_col = jnp.exp(log_w - m_new)
        decay = jnp.exp(b_last + m_prev - m_new)
        kw = k * w_col
        ct = decay * ct + _dot_tn(v, kw.astype(BF16))
        n_prev = decay * n_prev + jnp.sum(kw, axis=0, keepdims=True)
        m_prev = m_new

        ms = jnp.mean(hid * hid, axis=-1, keepdims=True)
        y_out = hid * lax.rsqrt(ms + EPS) * norm_g
        og = og_ref[rows, :].astype(F32)
        o_ref[rows, :] = (_sigmoid(og) * y_out).astype(o_ref.dtype)

    ct_ref[...] = ct
    n_ref[...] = n_prev
    m_ref[...] = m_prev


def _mlstm(proj, small, ig_t, gate_bias, conv_w, conv_b, norm_g, *, batch, seq, col0):
    shift, cum_sel, col_sel = _mlstm_tables()
    t = proj.shape[0]
    lb = SEQ_BLOCK
    nsb = seq // lb
    h_ = MLSTM_HEADS
    qb = MLSTM_DK
    vb = MLSTM_DV
    q0 = col0 // qb
    v0 = (col0 + 2 * h_ * qb) // vb

    def rows(b, h, i):
        return b * nsb + i

    in_specs = [
        pl.BlockSpec(memory_space=pltpu.SMEM),
        pl.BlockSpec((lb, qb), lambda b, h, i: (rows(b, h, i), q0 + h)),
        pl.BlockSpec((lb, qb), lambda b, h, i: (rows(b, h, i), q0 + h_ + h)),
        pl.BlockSpec((lb, vb), lambda b, h, i: (rows(b, h, i), v0 + h)),
        pl.BlockSpec((lb, vb), lambda b, h, i: (rows(b, h, i), v0 + h_ + h)),
        pl.BlockSpec((lb, SMALL_COLS), lambda b, h, i: (rows(b, h, i), 0)),
        pl.BlockSpec((lb // CHUNK, 2 * h_, CHUNK), lambda b, h, i: (rows(b, h, i), 0, 0)),
        pl.BlockSpec((CONV_WIDTH, qb), lambda b, h, i: (0, h)),
        pl.BlockSpec((CONV_WIDTH, qb), lambda b, h, i: (0, h_ + h)),
        pl.BlockSpec((1, qb), lambda b, h, i: (0, h)),
        pl.BlockSpec((1, qb), lambda b, h, i: (0, h_ + h)),
        pl.BlockSpec((None, 1, vb), lambda b, h, i: (h, 0, 0)),
        pl.BlockSpec(shift.shape, lambda b, h, i: (0, 0)),
        pl.BlockSpec(cum_sel.shape, lambda b, h, i: (0, 0)),
        pl.BlockSpec(col_sel.shape, lambda b, h, i: (0, 0)),
    ]
    scratch = [
        pltpu.VMEM((vb, qb), F32),
        pltpu.VMEM((1, qb), F32),
        pltpu.VMEM((1, 1), F32),
        pltpu.VMEM((SUBLANES, 2 * qb), F32),
    ]
    return pl.pallas_call(
        _mlstm_kernel,
        out_shape=jax.ShapeDtypeStruct((t, h_ * vb), BF16),
        grid=(batch, h_, nsb),
        in_specs=in_specs,
        out_specs=pl.BlockSpec((lb, vb), lambda b, h, i: (rows(b, h, i), h)),
        scratch_shapes=scratch,
        compiler_params=pltpu.CompilerParams(
            dimension_semantics=("parallel", "parallel", "arbitrary"),
            vmem_limit_bytes=_vmem_limit(16 << 20)),
        name="mlstm_scan",
    )(gate_bias, proj, proj, proj, proj, small, ig_t, conv_w, conv_w, conv_b, conv_b, norm_g,
      shift, cum_sel, col_sel)


def _xattn_kernel(q_ref, k_ref, v_ref, o_ref):
    s = _dot_nt(q_ref[...], k_ref[...]) * (XATTN_HEAD_DIM ** -0.5)
    p = jnp.exp(s - jnp.max(s, axis=-1, keepdims=True))
    p = p / jnp.sum(p, axis=-1, keepdims=True)
    o_ref[...] = _dot(p.astype(BF16), v_ref[...]).astype(o_ref.dtype)


def _xattn(q, k, v, *, batch, seq, tq):
    t, d = q.shape
    mem = k.shape[0] // batch
    hd = XATTN_HEAD_DIM
    nq = seq // tq
    return pl.pallas_call(
        _xattn_kernel,
        out_shape=jax.ShapeDtypeStruct((t, d), BF16),
        grid=(batch, nq, XATTN_HEADS),
        in_specs=[pl.BlockSpec((tq, hd), lambda b, i, h: (b * nq + i, h)),
                  pl.BlockSpec((mem, hd), lambda b, i, h: (b, h)),
                  pl.BlockSpec((mem, hd), lambda b, i, h: (b, h))],
        out_specs=pl.BlockSpec((tq, hd), lambda b, i, h: (b * nq + i, h)),
        compiler_params=pltpu.CompilerParams(
            dimension_semantics=("parallel", "parallel", "parallel"),
            vmem_limit_bytes=_vmem_limit(24 << 20)),
        name="cross_attention",
    )(q, k, v)


def kernel(x, mem, norm_mix_g, w_in, gla_gate_w2, gla_gate_b, gla_norm_g, mlstm_conv_w, mlstm_conv_b, mlstm_igate_b, mlstm_fgate_b, mlstm_norm_g, w_out, norm_cross_g, norm_mem_g, wq_c, wk_c, wv_c, wo_c, norm_ffn_g, w_gate, w_up, w_down, norm_final_g):
    batch, seq, d = x.shape
    mem_tokens = mem.shape[1]
    t = batch * seq
    depth = w_in.shape[0]
    gla_cols = 2 * GLA_HEADS * GLA_DK + 2 * GLA_HEADS * GLA_DV
    ml_cols = 2 * MLSTM_HEADS * MLSTM_DK + 2 * MLSTM_HEADS * MLSTM_DV
    ml_start = gla_cols + GLA_GATE_RANK

    h = x.reshape(t, d)
    mem2 = mem.reshape(batch * mem_tokens, d)
    for l in range(depth):
        w_main = jnp.concatenate(
            [w_in[l][:, :gla_cols], w_in[l][:, ml_start:ml_start + ml_cols]], axis=1).astype(BF16)
        w_small = jnp.concatenate(
            [w_in[l][:, gla_cols:ml_start], w_in[l][:, ml_start + ml_cols:]], axis=1)
        w_small = jnp.pad(w_small, ((0, 0), (0, SMALL_COLS - w_small.shape[1]))).astype(BF16)
        w2 = gla_gate_w2[l].reshape(GLA_GATE_RANK, GLA_HEADS, GLA_DK).transpose(1, 0, 2)
        w2 = jnp.pad(w2, ((0, 0), (0, SMALL_COLS - GLA_GATE_RANK), (0, 0)))
        gate_b = gla_gate_b[l].reshape(GLA_HEADS, 1, GLA_DK)
        gla_g = gla_norm_g[l].reshape(GLA_HEADS, 1, GLA_DV)
        ml_g = mlstm_norm_g[l].reshape(MLSTM_HEADS, 1, MLSTM_DV)
        gate_bias = jnp.concatenate([mlstm_igate_b[l], mlstm_fgate_b[l]]).astype(F32)
        conv_b = mlstm_conv_b[l].reshape(1, -1)
        w_o = w_out[l].astype(BF16)
        w_g = w_gate[l].astype(BF16)
        w_u = w_up[l].astype(BF16)
        w_d = w_down[l].astype(BF16)

        n1 = _rmsnorm(h, norm_mix_g[l], BF16)
        proj = _matmul([n1], [w_main], BF16, tm=1024, tn=1024, name="in_proj")
        small = _matmul([n1], [w_small], F32, tm=1024, tn=SMALL_COLS, name="in_proj_small")
        ig_t = small[:, IG_COL:IG_COL + 2 * MLSTM_HEADS].reshape(
            t // CHUNK, CHUNK, 2 * MLSTM_HEADS).transpose(0, 2, 1)
        o_gla = _gla(proj, small, w2, gate_b, gla_g, batch=batch, seq=seq)
        o_ml = _mlstm(proj, small, ig_t, gate_bias, mlstm_conv_w[l], conv_b, ml_g,
                      batch=batch, seq=seq, col0=gla_cols)
        h = _matmul([o_gla, o_ml], [w_o[:GROUP_WIDTH], w_o[GROUP_WIDTH:]], F32, resid=h,
                    tm=1024, tn=512, name="out_proj")

        n2 = _rmsnorm(h, norm_cross_g[l], BF16)
        mem_n = _rmsnorm(mem2, norm_mem_g[l], BF16)
        q = _matmul([n2], [wq_c[l].astype(BF16)], BF16, tm=1024, tn=1024, name="xattn_q")
        k_mem = _matmul([mem_n], [wk_c[l].astype(BF16)], BF16, tm=512, tn=1024, name="xattn_k")
        v_mem = _matmul([mem_n], [wv_c[l].astype(BF16)], BF16, tm=512, tn=1024, name="xattn_v")
        attn = _xattn(q, k_mem, v_mem, batch=batch, seq=seq, tq=1024)
        h = _matmul([attn], [wo_c[l].astype(BF16)], F32, resid=h, tm=1024, tn=512, name="xattn_o")

        n3 = _rmsnorm(h, norm_ffn_g[l], BF16)
        hid = _swiglu_up(n3, w_g, w_u, tm=1024, tn=FFN_TN)
        h = _matmul([hid], [w_d], F32, resid=h, tm=512, tn=512, name="ffn_down")

    y = _rmsnorm(h, norm_final_g, x.dtype)
    return y.reshape(batch, seq, d)
```

```python
import functools
import math

import jax
import jax.numpy as jnp
import numpy as np
from jax import lax
from jax.experimental import pallas as pl
from jax.experimental.pallas import tpu as pltpu

F32 = jnp.float32
BF16 = jnp.bfloat16

D_MODEL = 4096
GROUP_WIDTH = D_MODEL // 2
GLA_HEADS = 4
GLA_DV = GROUP_WIDTH // GLA_HEADS
GLA_DK = GLA_DV // 2
GLA_GATE_RANK = 16
GLA_TAU = 16.0
MLSTM_HEADS = 4
MLSTM_DV = GROUP_WIDTH // MLSTM_HEADS
MLSTM_DK = MLSTM_DV // 2
CONV_WIDTH = 4
XATTN_HEADS = 4
XATTN_HEAD_DIM = D_MODEL // XATTN_HEADS
FFN_HIDDEN = 256 * math.ceil(8 * D_MODEL / (3 * 256))
EPS = 1e-6

V7X_VMEM_LIMIT_BYTES = 56 * 1024 * 1024
LANES = 128
SUBLANES = 8

CHUNK = 64
GLA_LEVELS = (32, 16, 8, 4, 2, 1)
LOG2_E = math.log2(math.e)
SEQ_BLOCK = 256
NORM_ROWS = 256
FFN_TN = 256
SMALL_COLS = LANES
IG_COL = GLA_GATE_RANK
FG_COL = GLA_GATE_RANK + MLSTM_HEADS


def _vmem_limit(nbytes):
    return int(min(V7X_VMEM_LIMIT_BYTES, max(32 * 1024 * 1024, nbytes * 5 // 4 + (4 << 20))))


def _dot(a, b):
    return jnp.dot(a, b, preferred_element_type=F32)


def _dot_nt(a, b):
    return lax.dot_general(a, b, (((1,), (1,)), ((), ())), preferred_element_type=F32)


def _dot_tn(a, b):
    return lax.dot_general(a, b, (((0,), (0,)), ((), ())), preferred_element_type=F32)


def _split3(x):
    hi = x.astype(BF16)
    r = x - hi.astype(F32)
    mid = r.astype(BF16)
    lo = (r - mid.astype(F32)).astype(BF16)
    return hi, mid, lo


def _dot_exact_lhs(m_bf16, x):
    hi, mid, lo = _split3(x)
    return _dot(m_bf16, hi) + _dot(m_bf16, mid) + _dot(m_bf16, lo)


def _dot_exact_rhs(x, m_bf16):
    hi, mid, lo = _split3(x)
    return _dot(hi, m_bf16) + _dot(mid, m_bf16) + _dot(lo, m_bf16)


def _log_sigmoid(z):
    return jnp.minimum(z, 0.0) - jnp.log(1.0 + jnp.exp(-jnp.abs(z)))


def _sigmoid(z):
    return 1.0 / (1.0 + jnp.exp(-z))


def _rmsnorm_kernel(x_ref, g_ref, o_ref):
    x = x_ref[...].astype(F32)
    ms = jnp.mean(x * x, axis=-1, keepdims=True)
    o_ref[...] = (x * lax.rsqrt(ms + EPS) * g_ref[...]).astype(o_ref.dtype)


def _rmsnorm(x, g, out_dtype):
    rows, d = x.shape
    tr = min(NORM_ROWS, rows)
    nbytes = 2 * tr * d * (x.dtype.itemsize + jnp.dtype(out_dtype).itemsize) + 3 * tr * d * 4
    return pl.pallas_call(
        _rmsnorm_kernel,
        out_shape=jax.ShapeDtypeStruct((rows, d), out_dtype),
        grid=(rows // tr,),
        in_specs=[pl.BlockSpec((tr, d), lambda i: (i, 0)),
                  pl.BlockSpec((1, d), lambda i: (0, 0))],
        out_specs=pl.BlockSpec((tr, d), lambda i: (i, 0)),
        compiler_params=pltpu.CompilerParams(
            dimension_semantics=("parallel",), vmem_limit_bytes=_vmem_limit(nbytes)),
        name="rmsnorm",
    )(x, g.reshape(1, d).astype(F32))


def _mm_kernel(*refs, n_a, has_resid, has_norm):
    a_refs = refs[:n_a]
    w_ref = refs[n_a]
    pos = n_a + 1
    r_ref = refs[pos] if has_resid else None
    pos += int(has_resid)
    g_ref = refs[pos] if has_norm else None
    pos += int(has_norm)
    o_ref = refs[pos]

    kp = a_refs[0].shape[1]
    acc = _dot(a_refs[0][...], w_ref[0:kp, :])
    for p in range(1, n_a):
        acc = acc + _dot(a_refs[p][...], w_ref[p * kp:(p + 1) * kp, :])
    if has_resid:
        acc = r_ref[...] + acc
    o_ref[...] = acc.astype(o_ref.dtype)

    if has_norm:
        hb_ref, ssq_ref = refs[pos + 1], refs[pos + 2]
        hb_ref[...] = (acc * g_ref[...]).astype(hb_ref.dtype)
        part = jnp.broadcast_to(jnp.sum(acc * acc, axis=1, keepdims=True), ssq_ref.shape)
        j = pl.program_id(1)

        @pl.when(j == 0)
        def _():
            ssq_ref[...] = part

        @pl.when(j > 0)
        def _():
            ssq_ref[...] += part


def _matmul(a_list, w, out_dtype, *, resid=None, next_gain=None, tm, tn, name):
    m, kp = a_list[0].shape
    kdim, n = w.shape
    n_a = len(a_list)
    assert m % tm == 0 and n % tn == 0 and kdim == n_a * kp
    in_specs = ([pl.BlockSpec((tm, kp), lambda i, j: (i, 0)) for _ in a_list]
                + [pl.BlockSpec((kdim, tn), lambda i, j: (0, j))])
    args = list(a_list) + [w]
    out_shape = [jax.ShapeDtypeStruct((m, n), out_dtype)]
    out_specs = [pl.BlockSpec((tm, tn), lambda i, j: (i, j))]
    nbytes = 2 * (tm * kdim + kdim * tn) * 2 + 2 * tm * tn * jnp.dtype(out_dtype).itemsize + 2 * tm * tn * 4
    if resid is not None:
        in_specs.append(pl.BlockSpec((tm, tn), lambda i, j: (i, j)))
        args.append(resid)
        nbytes += 2 * tm * tn * 4
    if next_gain is not None:
        in_specs.append(pl.BlockSpec((1, tn), lambda i, j: (0, j)))
        args.append(next_gain.reshape(1, n).astype(F32))
        out_shape += [jax.ShapeDtypeStruct((m, n), BF16), jax.ShapeDtypeStruct((m, LANES), F32)]
        out_specs += [pl.BlockSpec((tm, tn), lambda i, j: (i, j)),
                      pl.BlockSpec((tm, LANES), lambda i, j: (i, 0))]
        nbytes += 2 * tm * tn * 2 + 2 * tm * LANES * 4 + tm * tn * 4
    outs = pl.pallas_call(
        functools.partial(_mm_kernel, n_a=n_a, has_resid=resid is not None,
                          has_norm=next_gain is not None),
        out_shape=out_shape,
        grid=(m // tm, n // tn),
        in_specs=in_specs,
        out_specs=out_specs,
        compiler_params=pltpu.CompilerParams(
            dimension_semantics=("parallel", "arbitrary"),
            vmem_limit_bytes=_vmem_limit(nbytes)),
        name=name,
    )(*args)
    return outs if next_gain is not None else outs[0]


def _norm_proj_kernel(a_ref, ssq_ref, *refs, n_w):
    w_refs = refs[:n_w]
    o_ref = refs[n_w]
    wb_refs = refs[n_w + 1:]

    @pl.when(pl.program_id(1) == 0)
    def _():
        for w_ref, wb_ref in zip(w_refs, wb_refs):
            wb_ref[...] = w_ref[...].astype(BF16)

    a = a_ref[...]
    r = lax.rsqrt(ssq_ref[:, 0:1] * (1.0 / a.shape[1]) + EPS)
    first = r * _dot(a, wb_refs[0][...])
    if n_w == 1:
        o_ref[...] = first.astype(o_ref.dtype)
    else:
        up = r * _dot(a, wb_refs[1][...])
        o_ref[...] = (first * _sigmoid(first) * up).astype(o_ref.dtype)


def _norm_proj(a, ssq, w_list, *, tm, tn, name):
    m, kdim = a.shape
    n = w_list[0].shape[1]
    n_w = len(w_list)
    assert m % tm == 0 and n % tn == 0
    nbytes = (2 * tm * kdim * 2 + n_w * kdim * tn * (2 * 4 + 2) + 2 * tm * tn * 2
              + 2 * tm * LANES * 4 + 3 * tm * tn * 4)
    return pl.pallas_call(
        functools.partial(_norm_proj_kernel, n_w=n_w),
        out_shape=jax.ShapeDtypeStruct((m, n), BF16),
        grid=(n // tn, m // tm),
        in_specs=([pl.BlockSpec((tm, kdim), lambda j, i: (i, 0)),
                   pl.BlockSpec((tm, LANES), lambda j, i: (i, 0))]
                  + [pl.BlockSpec((kdim, tn), lambda j, i: (0, j)) for _ in w_list]),
        out_specs=pl.BlockSpec((tm, tn), lambda j, i: (i, j)),
        scratch_shapes=[pltpu.VMEM((kdim, tn), BF16) for _ in w_list],
        compiler_params=pltpu.CompilerParams(
            dimension_semantics=("parallel", "arbitrary"),
            vmem_limit_bytes=_vmem_limit(nbytes)),
        name=name,
    )(a, ssq, *w_list)


def _gla_tables():
    c = CHUNK
    row = np.arange(c)[:, None]
    u = np.arange(c)[None, :]
    sel = [u <= row, u > row]
    masks = [row == u]
    for bsz in GLA_LEVELS:
        r = (row // (2 * bsz)) * (2 * bsz) + bsz - 1
        sel.append(np.where(row <= r, (u > row) & (u <= r), (u > r) & (u <= row)))
        same_pair = (row // (2 * bsz)) == (u // (2 * bsz))
        masks.append(same_pair & ((row % (2 * bsz)) >= bsz) & ((u % (2 * bsz)) < bsz))
    sel = np.concatenate(sel, axis=0).astype(np.float32)
    return (jnp.asarray(np.concatenate([sel] * 3, axis=1), BF16),
            jnp.asarray(np.stack(masks).astype(np.float32)))


def _gla_kernel(q_ref, k_ref, v_ref, g_ref, sm_ref, w2_ref, gb_ref, ng_ref, sel_ref, mask_ref,
                o_ref, st_ref):
    c_len = CHUNK

    @pl.when(pl.program_id(2) == 0)
    def _():
        st_ref[...] = jnp.zeros_like(st_ref)

    norm_g = ng_ref[...]
    z = _dot(sm_ref[...].astype(BF16), w2_ref[...].astype(BF16)) + gb_ref[...]
    la_hi, la_mid, la_lo = _split3(_log_sigmoid(z) * (LOG2_E / GLA_TAU))
    row_k = lax.broadcasted_iota(jnp.int32, (c_len, GLA_DK), 0)

    st = st_ref[...]
    for c in range(SEQ_BLOCK // c_len):
        rows = slice(c * c_len, (c + 1) * c_len)
        q = q_ref[rows, :].astype(F32) * (GLA_DK ** -0.5)
        k = k_ref[rows, :].astype(F32)
        v = v_ref[rows, :]
        la3 = jnp.concatenate([la_hi[rows, :], la_mid[rows, :], la_lo[rows, :]], axis=0)
        e = jnp.exp2(_dot(sel_ref[...], la3))
        q_dec = (q * e[0:c_len, :]).astype(BF16)
        k_dec = (k * e[c_len:2 * c_len, :]).astype(BF16)
        decay = e[c_len - 1:c_len, :]

        attn = _dot_nt(q.astype(BF16), k.astype(BF16)) * mask_ref[0]
        for lv, bsz in enumerate(GLA_LEVELS):
            in_right = (row_k & bsz) != 0
            x = (jnp.where(in_right, q, k) * e[(2 + lv) * c_len:(3 + lv) * c_len, :]).astype(BF16)
            attn = attn + _dot_nt(x, x) * mask_ref[1 + lv]

        out = _dot(attn.astype(BF16), v) + _dot_nt(q_dec, st.astype(BF16))
        st = st * decay + _dot_tn(v, k_dec)

        ms = jnp.mean(out * out, axis=-1, keepdims=True)
        y = out * lax.rsqrt(ms + EPS) * norm_g
        gg = g_ref[rows, :].astype(F32)
        o_ref[rows, :] = (y * (gg * _sigmoid(gg))).astype(o_ref.dtype)
    st_ref[...] = st


def _gla(proj, small, w2, gate_b, norm_g, *, batch, seq):
    decay_sel, pair_mask = _gla_tables()
    t = proj.shape[0]
    lb = SEQ_BLOCK
    nsb = seq // lb
    h_ = GLA_HEADS
    qb = GLA_DK
    vb = GLA_DV

    def rows(b, h, i):
        return b * nsb + i

    in_specs = [
        pl.BlockSpec((lb, qb), lambda b, h, i: (rows(b, h, i), h)),
        pl.BlockSpec((lb, qb), lambda b, h, i: (rows(b, h, i), h_ + h)),
        pl.BlockSpec((lb, vb), lambda b, h, i: (rows(b, h, i), (2 * h_ * qb) // vb + h)),
        pl.BlockSpec((lb, vb), lambda b, h, i: (rows(b, h, i), (2 * h_ * qb) // vb + h_ + h)),
        pl.BlockSpec((lb, SMALL_COLS), lambda b, h, i: (rows(b, h, i), 0)),
        pl.BlockSpec((None, SMALL_COLS, qb), lambda b, h, i: (h, 0, 0)),
        pl.BlockSpec((None, 1, qb), lambda b, h, i: (h, 0, 0)),
        pl.BlockSpec((None, 1, vb), lambda b, h, i: (h, 0, 0)),
        pl.BlockSpec(decay_sel.shape, lambda b, h, i: (0, 0)),
        pl.BlockSpec(pair_mask.shape, lambda b, h, i: (0, 0, 0)),
    ]
    return pl.pallas_call(
        _gla_kernel,
        out_shape=jax.ShapeDtypeStruct((t, h_ * vb), BF16),
        grid=(batch, h_, nsb),
        in_specs=in_specs,
        out_specs=pl.BlockSpec((lb, vb), lambda b, h, i: (rows(b, h, i), h)),
        scratch_shapes=[pltpu.VMEM((vb, qb), F32)],
        compiler_params=pltpu.CompilerParams(
            dimension_semantics=("parallel", "parallel", "arbitrary"),
            vmem_limit_bytes=_vmem_limit(16 << 20)),
        name="gla_scan",
    )(proj, proj, proj, proj, small, w2, gate_b, norm_g, decay_sel, pair_mask)


def _mlstm_tables():
    lb, c = SEQ_BLOCK, CHUNK
    t = np.arange(lb)[:, None]
    u = np.arange(lb)[None, :]
    shift = np.concatenate([(u == t - d) for d in range(1, CONV_WIDTH)], axis=0).astype(np.float32)
    tc = np.arange(c)[:, None]
    uc = np.arange(c)[None, :]
    cum = np.concatenate([uc <= tc, uc > tc], axis=0).astype(np.float32)
    col = np.concatenate([tc > uc, np.ones((c, c), bool)], axis=1).astype(np.float32)
    return (jnp.asarray(shift, BF16), jnp.asarray(np.concatenate([cum] * 3, axis=1), BF16),
            jnp.asarray(col))


def _mlstm_kernel(gbias_ref, q_ref, k_ref, v_ref, og_ref, sm_ref, igt_ref, cwq_ref, cwk_ref,
                  cbq_ref, cbk_ref, ng_ref, shift_ref, cum_ref, col_ref, o_ref,
                  ct_ref, n_ref, m_ref, hist_ref):
    c_len = CHUNK
    lb = SEQ_BLOCK
    taps = CONV_WIDTH
    h = pl.program_id(1)

    @pl.when(pl.program_id(2) == 0)
    def _():
        ct_ref[...] = jnp.zeros_like(ct_ref)
        n_ref[...] = jnp.zeros_like(n_ref)
        m_ref[...] = jnp.zeros_like(m_ref)
        hist_ref[...] = jnp.zeros_like(hist_ref)

    x16 = jnp.concatenate([q_ref[...], k_ref[...]], axis=1)
    x = x16.astype(F32)
    shifted = _dot(shift_ref[...], x16)
    head = jnp.concatenate([hist_ref[...], x[0:SUBLANES, :]], axis=0)
    w = jnp.concatenate([cwq_ref[...], cwk_ref[...]], axis=1)
    y = jnp.concatenate([cbq_ref[...], cbk_ref[...]], axis=1) + w[taps - 1:taps, :] * x
    for d in range(1, taps):
        sh = jnp.concatenate([head[SUBLANES - d:2 * SUBLANES - d, :],
                              shifted[(d - 1) * lb + SUBLANES:d * lb, :]], axis=0)
        y = y + w[taps - 1 - d:taps - d, :] * sh
    hist_ref[...] = x[lb - SUBLANES:lb, :]
    act = y * _sigmoid(y)
    q_all = act[:, 0:MLSTM_DK]
    k_all = act[:, MLSTM_DK:2 * MLSTM_DK] * (MLSTM_DK ** -0.5)

    ib = gbias_ref[h]
    fb = gbias_ref[MLSTM_HEADS + h]
    sm = sm_ref[...]
    lane_s = lax.broadcasted_iota(jnp.int32, (lb, SMALL_COLS), 1)
    i_col_all = jnp.sum(jnp.where(lane_s == IG_COL + h, sm, 0.0), axis=1, keepdims=True) + ib
    f_col_all = jnp.sum(jnp.where(lane_s == FG_COL + h, sm, 0.0), axis=1, keepdims=True) + fb
    lf_col_all = _log_sigmoid(f_col_all)

    norm_g = ng_ref[...]
    row_c = lax.broadcasted_iota(jnp.int32, (c_len, c_len), 0)
    col_c = lax.broadcasted_iota(jnp.int32, (c_len, c_len), 1)
    causal = row_c >= col_c

    ct = ct_ref[...]
    n_prev = n_ref[...]
    m_prev = m_ref[...]
    for c in range(lb // c_len):
        rows = slice(c * c_len, (c + 1) * c_len)
        q = q_all[rows, :]
        k = k_all[rows, :]
        v = v_ref[rows, :]
        q16 = q.astype(BF16)
        i_col = i_col_all[rows, :]
        i_row = igt_ref[c, pl.ds(h, 1), :] + ib

        g_hi, g_mid, g_lo = _split3(lf_col_all[rows, :] * col_ref[...])
        sums = _dot(cum_ref[...], jnp.concatenate([g_hi, g_mid, g_lo], axis=0))
        d0 = sums[0:c_len, 0:c_len]
        bb_col = sums[0:c_len, c_len:c_len + 1]
        rem_col = sums[c_len:2 * c_len, c_len:c_len + 1]
        b_last = bb_col[c_len - 1:c_len, :]

        log_d = jnp.where(causal, d0 + i_row, -jnp.inf)
        inter = bb_col + m_prev
        m = jnp.maximum(inter, jnp.max(log_d, axis=1, keepdims=True))
        g_inter = jnp.exp(inter - m)
        scores = _dot_nt(q16, k.astype(BF16)) * jnp.exp(log_d - m)
        num = _dot(scores.astype(BF16), v) + g_inter * _dot_nt(q16, ct.astype(BF16))
        den = (jnp.sum(scores, axis=1, keepdims=True)
               + g_inter * jnp.sum(q * n_prev, axis=1, keepdims=True))
        hid = num / jnp.maximum(jnp.abs(den), jnp.exp(-m))

        log_w = rem_col + i_col
        m_new = jnp.maximum(b_last + m_prev, jnp.max(log_w, axis=0, keepdims=True))
        w_col = jnp.exp(log_w - m_new)
        decay = jnp.exp(b_last + m_prev - m_new)
        kw = k * w_col
        ct = decay * ct + _dot_tn(v, kw.astype(BF16))
        n_prev = decay * n_prev + jnp.sum(kw, axis=0, keepdims=True)
        m_prev = m_new

        ms = jnp.mean(hid * hid, axis=-1, keepdims=True)
        y_out = hid * lax.rsqrt(ms + EPS) * norm_g
        og = og_ref[rows, :].astype(F32)
        o_ref[rows, :] = (_sigmoid(og) * y_out).astype(o_ref.dtype)

    ct_ref[...] = ct
    n_ref[...] = n_prev
    m_ref[...] = m_prev


def _mlstm(proj, small, ig_t, gate_bias, conv_w, conv_b, norm_g, *, batch, seq):
    shift, cum_sel, col_sel = _mlstm_tables()
    t = proj.shape[0]
    lb = SEQ_BLOCK
    nsb = seq // lb
    h_ = MLSTM_HEADS
    qb = MLSTM_DK
    vb = MLSTM_DV
    q0 = 0
    v0 = (2 * h_ * qb) // vb

    def rows(b, h, i):
        return b * nsb + i

    in_specs = [
        pl.BlockSpec(memory_space=pltpu.SMEM),
        pl.BlockSpec((lb, qb), lambda b, h, i: (rows(b, h, i), q0 + h)),
        pl.BlockSpec((lb, qb), lambda b, h, i: (rows(b, h, i), q0 + h_ + h)),
        pl.BlockSpec((lb, vb), lambda b, h, i: (rows(b, h, i), v0 + h)),
        pl.BlockSpec((lb, vb), lambda b, h, i: (rows(b, h, i), v0 + h_ + h)),
        pl.BlockSpec((lb, SMALL_COLS), lambda b, h, i: (rows(b, h, i), 0)),
        pl.BlockSpec((lb // CHUNK, 2 * h_, CHUNK), lambda b, h, i: (rows(b, h, i), 0, 0)),
        pl.BlockSpec((CONV_WIDTH, qb), lambda b, h, i: (0, h)),
        pl.BlockSpec((CONV_WIDTH, qb), lambda b, h, i: (0, h_ + h)),
        pl.BlockSpec((1, qb), lambda b, h, i: (0, h)),
        pl.BlockSpec((1, qb), lambda b, h, i: (0, h_ + h)),
        pl.BlockSpec((None, 1, vb), lambda b, h, i: (h, 0, 0)),
        pl.BlockSpec(shift.shape, lambda b, h, i: (0, 0)),
        pl.BlockSpec(cum_sel.shape, lambda b, h, i: (0, 0)),
        pl.BlockSpec(col_sel.shape, lambda b, h, i: (0, 0)),
    ]
    scratch = [
        pltpu.VMEM((vb, qb), F32),
        pltpu.VMEM((1, qb), F32),
        pltpu.VMEM((1, 1), F32),
        pltpu.VMEM((SUBLANES, 2 * qb), F32),
    ]
    return pl.pallas_call(
        _mlstm_kernel,
        out_shape=jax.ShapeDtypeStruct((t, h_ * vb), BF16),
        grid=(batch, h_, nsb),
        in_specs=in_specs,
        out_specs=pl.BlockSpec((lb, vb), lambda b, h, i: (rows(b, h, i), h)),
        scratch_shapes=scratch,
        compiler_params=pltpu.CompilerParams(
            dimension_semantics=("parallel", "parallel", "arbitrary"),
            vmem_limit_bytes=_vmem_limit(16 << 20)),
        name="mlstm_scan",
    )(gate_bias, proj, proj, proj, proj, small, ig_t, conv_w, conv_w, conv_b, conv_b, norm_g,
      shift, cum_sel, col_sel)


def _xattn_kernel(q_ref, k_ref, v_ref, o_ref):
    s = _dot_nt(q_ref[...], k_ref[...]) * (XATTN_HEAD_DIM ** -0.5)
    p = jnp.exp(s - jnp.max(s, axis=-1, keepdims=True))
    p = p / jnp.sum(p, axis=-1, keepdims=True)
    o_ref[...] = _dot(p.astype(BF16), v_ref[...]).astype(o_ref.dtype)


def _xattn(q, k, v, *, batch, seq, tq):
    t, d = q.shape
    mem = k.shape[0] // batch
    hd = XATTN_HEAD_DIM
    nq = seq // tq
    return pl.pallas_call(
        _xattn_kernel,
        out_shape=jax.ShapeDtypeStruct((t, d), BF16),
        grid=(batch, nq, XATTN_HEADS),
        in_specs=[pl.BlockSpec((tq, hd), lambda b, i, h: (b * nq + i, h)),
                  pl.BlockSpec((mem, hd), lambda b, i, h: (b, h)),
                  pl.BlockSpec((mem, hd), lambda b, i, h: (b, h))],
        out_specs=pl.BlockSpec((tq, hd), lambda b, i, h: (b * nq + i, h)),
        compiler_params=pltpu.CompilerParams(
            dimension_semantics=("parallel", "parallel", "parallel"),
            vmem_limit_bytes=_vmem_limit(24 << 20)),
        name="cross_attention",
    )(q, k, v)


def kernel(x, mem, norm_mix_g, w_in, gla_gate_w2, gla_gate_b, gla_norm_g, mlstm_conv_w, mlstm_conv_b, mlstm_igate_b, mlstm_fgate_b, mlstm_norm_g, w_out, norm_cross_g, norm_mem_g, wq_c, wk_c, wv_c, wo_c, norm_ffn_g, w_gate, w_up, w_down, norm_final_g):
    batch, seq, d = x.shape
    mem_tokens = mem.shape[1]
    t = batch * seq
    depth = w_in.shape[0]
    gla_cols = 2 * GLA_HEADS * GLA_DK + 2 * GLA_HEADS * GLA_DV
    ml_cols = 2 * MLSTM_HEADS * MLSTM_DK + 2 * MLSTM_HEADS * MLSTM_DV
    ml_start = gla_cols + GLA_GATE_RANK

    h = x.reshape(t, d)
    mem2 = mem.reshape(batch * mem_tokens, d)
    for l in range(depth):
        w_gla = w_in[l][:, :gla_cols].astype(BF16)
        w_ml = w_in[l][:, ml_start:ml_start + ml_cols].astype(BF16)
        w_small = jnp.concatenate(
            [w_in[l][:, gla_cols:ml_start], w_in[l][:, ml_start + ml_cols:]], axis=1)
        w_small = jnp.pad(w_small, ((0, 0), (0, SMALL_COLS - w_small.shape[1]))).astype(BF16)
        w2 = gla_gate_w2[l].reshape(GLA_GATE_RANK, GLA_HEADS, GLA_DK).transpose(1, 0, 2)
        w2 = jnp.pad(w2, ((0, 0), (0, SMALL_COLS - GLA_GATE_RANK), (0, 0)))
        gate_b = gla_gate_b[l].reshape(GLA_HEADS, 1, GLA_DK)
        gla_g = gla_norm_g[l].reshape(GLA_HEADS, 1, GLA_DV)
        ml_g = mlstm_norm_g[l].reshape(MLSTM_HEADS, 1, MLSTM_DV)
        gate_bias = jnp.concatenate([mlstm_igate_b[l], mlstm_fgate_b[l]]).astype(F32)
        conv_b = mlstm_conv_b[l].reshape(1, -1)
        n1 = _rmsnorm(h, norm_mix_g[l], BF16)
        proj_gla = _matmul([n1], w_gla, BF16, tm=1024, tn=1024, name="in_proj_gla")
        proj_ml = _matmul([n1], w_ml, BF16, tm=1024, tn=1024, name="in_proj_mlstm")
        small = _matmul([n1], w_small, F32, tm=1024, tn=SMALL_COLS, name="in_proj_small")
        ig_t = small[:, IG_COL:IG_COL + 2 * MLSTM_HEADS].reshape(
            t // CHUNK, CHUNK, 2 * MLSTM_HEADS).transpose(0, 2, 1)
        o_gla = _gla(proj_gla, small, w2, gate_b, gla_g, batch=batch, seq=seq)
        o_ml = _mlstm(proj_ml, small, ig_t, gate_bias, mlstm_conv_w[l], conv_b, ml_g,
                      batch=batch, seq=seq)
        h, hb, ssq = _matmul([o_gla, o_ml], w_out[l].astype(BF16), F32, resid=h,
                             next_gain=norm_cross_g[l], tm=1024, tn=512, name="out_proj")

        q = _norm_proj(hb, ssq, [wq_c[l]], tm=1024, tn=512, name="xattn_q")
        mem_n = _rmsnorm(mem2, norm_mem_g[l], BF16)
        k_mem = _matmul([mem_n], wk_c[l].astype(BF16), BF16, tm=512, tn=1024, name="xattn_k")
        v_mem = _matmul([mem_n], wv_c[l].astype(BF16), BF16, tm=512, tn=1024, name="xattn_v")
        attn = _xattn(q, k_mem, v_mem, batch=batch, seq=seq, tq=1024)
        h, hb, ssq = _matmul([attn], wo_c[l].astype(BF16), F32, resid=h,
                             next_gain=norm_ffn_g[l], tm=1024, tn=512, name="xattn_o")

        hid = _norm_proj(hb, ssq, [w_gate[l], w_up[l]], tm=1024, tn=FFN_TN, name="swiglu_up")
        h = _matmul([hid], w_down[l].astype(BF16), F32, resid=h, tm=512, tn=512, name="ffn_down")

    y = _rmsnorm(h, norm_final_g, x.dtype)
    return y.reshape(batch, seq, d)
```

```python
import functools
import math

import jax
import jax.numpy as jnp
import numpy as np
from jax import lax
from jax.experimental import pallas as pl
from jax.experimental.pallas import tpu as pltpu

F32 = jnp.float32
BF16 = jnp.bfloat16

D_MODEL = 4096
GROUP_WIDTH = D_MODEL // 2
GLA_HEADS = 4
GLA_DV = GROUP_WIDTH // GLA_HEADS
GLA_DK = GLA_DV // 2
GLA_GATE_RANK = 16
GLA_TAU = 16.0
MLSTM_HEADS = 4
MLSTM_DV = GROUP_WIDTH // MLSTM_HEADS
MLSTM_DK = MLSTM_DV // 2
CONV_WIDTH = 4
XATTN_HEADS = 4
XATTN_HEAD_DIM = D_MODEL // XATTN_HEADS
FFN_HIDDEN = 256 * math.ceil(8 * D_MODEL / (3 * 256))
EPS = 1e-6

V7X_VMEM_LIMIT_BYTES = 56 * 1024 * 1024
LANES = 128
SUBLANES = 8

CHUNK = 64
GLA_LEVELS = (32, 16, 8, 4, 2, 1)
LOG2_E = math.log2(math.e)
SEQ_BLOCK = 256
NORM_ROWS = 256
FFN_TN = 256
SMALL_COLS = LANES
IG_COL = GLA_GATE_RANK
FG_COL = GLA_GATE_RANK + MLSTM_HEADS


def _vmem_limit(nbytes):
    return int(min(V7X_VMEM_LIMIT_BYTES, max(32 * 1024 * 1024, nbytes * 5 // 4 + (4 << 20))))


def _dot(a, b):
    return jnp.dot(a, b, preferred_element_type=F32)


def _dot_nt(a, b):
    return lax.dot_general(a, b, (((1,), (1,)), ((), ())), preferred_element_type=F32)


def _dot_tn(a, b):
    return lax.dot_general(a, b, (((0,), (0,)), ((), ())), preferred_element_type=F32)


def _split3(x):
    hi = x.astype(BF16)
    r = x - hi.astype(F32)
    mid = r.astype(BF16)
    lo = (r - mid.astype(F32)).astype(BF16)
    return hi, mid, lo


def _dot_exact_lhs(m_bf16, x):
    hi, mid, lo = _split3(x)
    return _dot(m_bf16, hi) + _dot(m_bf16, mid) + _dot(m_bf16, lo)


def _dot_exact_rhs(x, m_bf16):
    hi, mid, lo = _split3(x)
    return _dot(hi, m_bf16) + _dot(mid, m_bf16) + _dot(lo, m_bf16)


def _log_sigmoid(z):
    return jnp.minimum(z, 0.0) - jnp.log(1.0 + jnp.exp(-jnp.abs(z)))


def _sigmoid(z):
    return 1.0 / (1.0 + jnp.exp(-z))


def _rmsnorm_kernel(x_ref, g_ref, o_ref):
    x = x_ref[...].astype(F32)
    ms = jnp.mean(x * x, axis=-1, keepdims=True)
    o_ref[...] = (x * lax.rsqrt(ms + EPS) * g_ref[...]).astype(o_ref.dtype)


def _rmsnorm(x, g, out_dtype):
    rows, d = x.shape
    tr = min(NORM_ROWS, rows)
    nbytes = 2 * tr * d * (x.dtype.itemsize + jnp.dtype(out_dtype).itemsize) + 3 * tr * d * 4
    return pl.pallas_call(
        _rmsnorm_kernel,
        out_shape=jax.ShapeDtypeStruct((rows, d), out_dtype),
        grid=(rows // tr,),
        in_specs=[pl.BlockSpec((tr, d), lambda i: (i, 0)),
                  pl.BlockSpec((1, d), lambda i: (0, 0))],
        out_specs=pl.BlockSpec((tr, d), lambda i: (i, 0)),
        compiler_params=pltpu.CompilerParams(
            dimension_semantics=("parallel",), vmem_limit_bytes=_vmem_limit(nbytes)),
        name="rmsnorm",
    )(x, g.reshape(1, d).astype(F32))


def _mm_kernel(*refs, n_a, has_resid, has_norm):
    a_refs = refs[:n_a]
    w_ref = refs[n_a]
    pos = n_a + 1
    r_ref = refs[pos] if has_resid else None
    pos += int(has_resid)
    g_ref = refs[pos] if has_norm else None
    pos += int(has_norm)
    o_ref = refs[pos]

    kp = a_refs[0].shape[1]
    acc = _dot(a_refs[0][...], w_ref[0:kp, :])
    for p in range(1, n_a):
        acc = acc + _dot(a_refs[p][...], w_ref[p * kp:(p + 1) * kp, :])
    if has_resid:
        acc = r_ref[...] + acc
    o_ref[...] = acc.astype(o_ref.dtype)

    if has_norm:
        hb_ref, ssq_ref = refs[pos + 1], refs[pos + 2]
        hb_ref[...] = (acc * g_ref[...]).astype(hb_ref.dtype)
        part = jnp.broadcast_to(jnp.sum(acc * acc, axis=1, keepdims=True), ssq_ref.shape)
        j = pl.program_id(1)

        @pl.when(j == 0)
        def _():
            ssq_ref[...] = part

        @pl.when(j > 0)
        def _():
            ssq_ref[...] += part


def _matmul(a_list, w, out_dtype, *, resid=None, next_gain=None, tm, tn, name):
    m, kp = a_list[0].shape
    kdim, n = w.shape
    n_a = len(a_list)
    assert m % tm == 0 and n % tn == 0 and kdim == n_a * kp
    in_specs = ([pl.BlockSpec((tm, kp), lambda i, j: (i, 0)) for _ in a_list]
                + [pl.BlockSpec((kdim, tn), lambda i, j: (0, j))])
    args = list(a_list) + [w]
    out_shape = [jax.ShapeDtypeStruct((m, n), out_dtype)]
    out_specs = [pl.BlockSpec((tm, tn), lambda i, j: (i, j))]
    nbytes = 2 * (tm * kdim + kdim * tn) * 2 + 2 * tm * tn * jnp.dtype(out_dtype).itemsize + 2 * tm * tn * 4
    if resid is not None:
        in_specs.append(pl.BlockSpec((tm, tn), lambda i, j: (i, j)))
        args.append(resid)
        nbytes += 2 * tm * tn * 4
    if next_gain is not None:
        in_specs.append(pl.BlockSpec((1, tn), lambda i, j: (0, j)))
        args.append(next_gain.reshape(1, n).astype(F32))
        out_shape += [jax.ShapeDtypeStruct((m, n), BF16), jax.ShapeDtypeStruct((m, LANES), F32)]
        out_specs += [pl.BlockSpec((tm, tn), lambda i, j: (i, j)),
                      pl.BlockSpec((tm, LANES), lambda i, j: (i, 0))]
        nbytes += 2 * tm * tn * 2 + 2 * tm * LANES * 4 + tm * tn * 4
    outs = pl.pallas_call(
        functools.partial(_mm_kernel, n_a=n_a, has_resid=resid is not None,
                          has_norm=next_gain is not None),
        out_shape=out_shape,
        grid=(m // tm, n // tn),
        in_specs=in_specs,
        out_specs=out_specs,
        compiler_params=pltpu.CompilerParams(
            dimension_semantics=("parallel", "arbitrary"),
            vmem_limit_bytes=_vmem_limit(nbytes)),
        name=name,
    )(*args)
    return outs if next_gain is not None else outs[0]


def _norm_proj_kernel(a_ref, ssq_ref, *refs, n_w):
    w_refs = refs[:n_w]
    o_ref = refs[n_w]
    wb_refs = refs[n_w + 1:]

    @pl.when(pl.program_id(1) == 0)
    def _():
        for w_ref, wb_ref in zip(w_refs, wb_refs):
            wb_ref[...] = w_ref[...].astype(BF16)

    a = a_ref[...]
    r = lax.rsqrt(ssq_ref[:, 0:1] * (1.0 / a.shape[1]) + EPS)
    first = r * _dot(a, wb_refs[0][...])
    if n_w == 1:
        o_ref[...] = first.astype(o_ref.dtype)
    else:
        up = r * _dot(a, wb_refs[1][...])
        o_ref[...] = (first * _sigmoid(first) * up).astype(o_ref.dtype)


def _norm_proj(a, ssq, w_list, *, tm, tn, name):
    m, kdim = a.shape
    n = w_list[0].shape[1]
    n_w = len(w_list)
    assert m % tm == 0 and n % tn == 0
    nbytes = (2 * tm * kdim * 2 + n_w * kdim * tn * (2 * 4 + 2) + 2 * tm * tn * 2
              + 2 * tm * LANES * 4 + 3 * tm * tn * 4)
    return pl.pallas_call(
        functools.partial(_norm_proj_kernel, n_w=n_w),
        out_shape=jax.ShapeDtypeStruct((m, n), BF16),
        grid=(n // tn, m // tm),
        in_specs=([pl.BlockSpec((tm, kdim), lambda j, i: (i, 0)),
                   pl.BlockSpec((tm, LANES), lambda j, i: (i, 0))]
                  + [pl.BlockSpec((kdim, tn), lambda j, i: (0, j)) for _ in w_list]),
        out_specs=pl.BlockSpec((tm, tn), lambda j, i: (i, j)),
        scratch_shapes=[pltpu.VMEM((kdim, tn), BF16) for _ in w_list],
        compiler_params=pltpu.CompilerParams(
            dimension_semantics=("parallel", "arbitrary"),
            vmem_limit_bytes=_vmem_limit(nbytes)),
        name=name,
    )(a, ssq, *w_list)


def _gla_tables():
    c = CHUNK
    row = np.arange(c)[:, None]
    u = np.arange(c)[None, :]
    sel = [u <= row, u > row]
    masks = [row == u]
    for bsz in GLA_LEVELS:
        r = (row // (2 * bsz)) * (2 * bsz) + bsz - 1
        sel.append(np.where(row <= r, (u > row) & (u <= r), (u > r) & (u <= row)))
        same_pair = (row // (2 * bsz)) == (u // (2 * bsz))
        masks.append(same_pair & ((row % (2 * bsz)) >= bsz) & ((u % (2 * bsz)) < bsz))
    sel = np.concatenate(sel, axis=0).astype(np.float32)
    return (jnp.asarray(np.concatenate([sel] * 3, axis=1), BF16),
            jnp.asarray(np.stack(masks).astype(np.float32)))


def _gla_stream(q_ref, k_ref, v_ref, g_ref, sm_ref, w2_ref, gb_ref, ng_ref, sel_ref, mask_ref,
                o_ref, st_ref):
    c_len = CHUNK
    norm_g = ng_ref[...]
    z = _dot(sm_ref[...].astype(BF16), w2_ref[...].astype(BF16)) + gb_ref[...]
    la_hi, la_mid, la_lo = _split3(_log_sigmoid(z) * (LOG2_E / GLA_TAU))
    row_k = lax.broadcasted_iota(jnp.int32, (c_len, GLA_DK), 0)
    state = [st_ref[...]]

    def chunk(c):
        st = state[0]
        rows = slice(c * c_len, (c + 1) * c_len)
        q = q_ref[rows, :].astype(F32) * (GLA_DK ** -0.5)
        k = k_ref[rows, :].astype(F32)
        v = v_ref[rows, :]
        la3 = jnp.concatenate([la_hi[rows, :], la_mid[rows, :], la_lo[rows, :]], axis=0)
        e = jnp.exp2(_dot(sel_ref[...], la3))
        q_dec = (q * e[0:c_len, :]).astype(BF16)
        k_dec = (k * e[c_len:2 * c_len, :]).astype(BF16)
        decay = e[c_len - 1:c_len, :]

        attn = _dot_nt(q.astype(BF16), k.astype(BF16)) * mask_ref[0]
        for lv, bsz in enumerate(GLA_LEVELS):
            in_right = (row_k & bsz) != 0
            x = (jnp.where(in_right, q, k) * e[(2 + lv) * c_len:(3 + lv) * c_len, :]).astype(BF16)
            attn = attn + _dot_nt(x, x) * mask_ref[1 + lv]

        out = _dot(attn.astype(BF16), v) + _dot_nt(q_dec, st.astype(BF16))
        state[0] = st * decay + _dot_tn(v, k_dec)

        ms = jnp.mean(out * out, axis=-1, keepdims=True)
        y = out * lax.rsqrt(ms + EPS) * norm_g
        gg = g_ref[rows, :].astype(F32)
        o_ref[rows, :] = (y * (gg * _sigmoid(gg))).astype(o_ref.dtype)

    def finish():
        st_ref[...] = state[0]

    return chunk, finish


def _gla(proj, small, w2, gate_b, norm_g, *, batch, seq):
    decay_sel, pair_mask = _gla_tables()
    t = proj.shape[0]
    lb = SEQ_BLOCK
    nsb = seq // lb
    h_ = GLA_HEADS
    qb = GLA_DK
    vb = GLA_DV

    def rows(b, h, i):
        return b * nsb + i

    in_specs = [
        pl.BlockSpec((lb, qb), lambda b, h, i: (rows(b, h, i), h)),
        pl.BlockSpec((lb, qb), lambda b, h, i: (rows(b, h, i), h_ + h)),
        pl.BlockSpec((lb, vb), lambda b, h, i: (rows(b, h, i), (2 * h_ * qb) // vb + h)),
        pl.BlockSpec((lb, vb), lambda b, h, i: (rows(b, h, i), (2 * h_ * qb) // vb + h_ + h)),
        pl.BlockSpec((lb, SMALL_COLS), lambda b, h, i: (rows(b, h, i), 0)),
        pl.BlockSpec((None, SMALL_COLS, qb), lambda b, h, i: (h, 0, 0)),
        pl.BlockSpec((None, 1, qb), lambda b, h, i: (h, 0, 0)),
        pl.BlockSpec((None, 1, vb), lambda b, h, i: (h, 0, 0)),
        pl.BlockSpec(decay_sel.shape, lambda b, h, i: (0, 0)),
        pl.BlockSpec(pair_mask.shape, lambda b, h, i: (0, 0, 0)),
    ]
    return dict(
        in_specs=in_specs,
        args=[proj, proj, proj, proj, small, w2, gate_b, norm_g, decay_sel, pair_mask],
        out_shape=jax.ShapeDtypeStruct((t, h_ * vb), BF16),
        out_spec=pl.BlockSpec((lb, vb), lambda b, h, i: (rows(b, h, i), h)),
        scratch=[pltpu.VMEM((vb, qb), F32)])


def _mlstm_tables():
    lb, c = SEQ_BLOCK, CHUNK
    t = np.arange(lb)[:, None]
    u = np.arange(lb)[None, :]
    shift = np.concatenate([(u == t - d) for d in range(1, CONV_WIDTH)], axis=0).astype(np.float32)
    tc = np.arange(c)[:, None]
    uc = np.arange(c)[None, :]
    cum = np.concatenate([uc <= tc, uc > tc], axis=0).astype(np.float32)
    col = np.concatenate([tc > uc, np.ones((c, c), bool)], axis=1).astype(np.float32)
    return (jnp.asarray(shift, BF16), jnp.asarray(np.concatenate([cum] * 3, axis=1), BF16),
            jnp.asarray(col))


def _mlstm_stream(gbias_ref, q_ref, k_ref, v_ref, og_ref, sm_ref, igt_ref, cwq_ref, cwk_ref,
                  cbq_ref, cbk_ref, ng_ref, shift_ref, cum_ref, col_ref, o_ref,
                  ct_ref, n_ref, m_ref, hist_ref):
    c_len = CHUNK
    lb = SEQ_BLOCK
    taps = CONV_WIDTH
    h = pl.program_id(1)

    x16 = jnp.concatenate([q_ref[...], k_ref[...]], axis=1)
    x = x16.astype(F32)
    shifted = _dot(shift_ref[...], x16)
    head = jnp.concatenate([hist_ref[...], x[0:SUBLANES, :]], axis=0)
    w = jnp.concatenate([cwq_ref[...], cwk_ref[...]], axis=1)
    y = jnp.concatenate([cbq_ref[...], cbk_ref[...]], axis=1) + w[taps - 1:taps, :] * x
    for d in range(1, taps):
        sh = jnp.concatenate([head[SUBLANES - d:2 * SUBLANES - d, :],
                              shifted[(d - 1) * lb + SUBLANES:d * lb, :]], axis=0)
        y = y + w[taps - 1 - d:taps - d, :] * sh
    hist_ref[...] = x[lb - SUBLANES:lb, :]
    act = y * _sigmoid(y)
    q_all = act[:, 0:MLSTM_DK]
    k_all = act[:, MLSTM_DK:2 * MLSTM_DK] * (MLSTM_DK ** -0.5)

    ib = gbias_ref[h]
    fb = gbias_ref[MLSTM_HEADS + h]
    sm = sm_ref[...]
    lane_s = lax.broadcasted_iota(jnp.int32, (lb, SMALL_COLS), 1)
    i_col_all = jnp.sum(jnp.where(lane_s == IG_COL + h, sm, 0.0), axis=1, keepdims=True) + ib
    f_col_all = jnp.sum(jnp.where(lane_s == FG_COL + h, sm, 0.0), axis=1, keepdims=True) + fb
    lf_col_all = _log_sigmoid(f_col_all)

    norm_g = ng_ref[...]
    row_c = lax.broadcasted_iota(jnp.int32, (c_len, c_len), 0)
    col_c = lax.broadcasted_iota(jnp.int32, (c_len, c_len), 1)
    causal = row_c >= col_c

    state = [ct_ref[...], n_ref[...], m_ref[...]]

    def chunk(c):
        ct, n_prev, m_prev = state
        rows = slice(c * c_len, (c + 1) * c_len)
        q = q_all[rows, :]
        k = k_all[rows, :]
        v = v_ref[rows, :]
        q16 = q.astype(BF16)
        i_col = i_col_all[rows, :]
        i_row = igt_ref[c, pl.ds(h, 1), :] + ib

        g_hi, g_mid, g_lo = _split3(lf_col_all[rows, :] * col_ref[...])
        sums = _dot(cum_ref[...], jnp.concatenate([g_hi, g_mid, g_lo], axis=0))
        d0 = sums[0:c_len, 0:c_len]
        bb_col = sums[0:c_len, c_len:c_len + 1]
        rem_col = sums[c_len:2 * c_len, c_len:c_len + 1]
        b_last = bb_col[c_len - 1:c_len, :]

        log_d = jnp.where(causal, d0 + i_row, -jnp.inf)
        inter = bb_col + m_prev
        m = jnp.maximum(inter, jnp.max(log_d, axis=1, keepdims=True))
        g_inter = jnp.exp(inter - m)
        scores = _dot_nt(q16, k.astype(BF16)) * jnp.exp(log_d - m)
        num = _dot(scores.astype(BF16), v) + g_inter * _dot_nt(q16, ct.astype(BF16))
        den = (jnp.sum(scores, axis=1, keepdims=True)
               + g_inter * jnp.sum(q * n_prev, axis=1, keepdims=True))
        hid = num / jnp.maximum(jnp.abs(den), jnp.exp(-m))

        log_w = rem_col + i_col
        m_new = jnp.maximum(b_last + m_prev, jnp.max(log_w, axis=0, keepdims=True))
        w_col = jnp.exp(log_w - m_new)
        decay = jnp.exp(b_last + m_prev - m_new)
        kw = k * w_col
        state[0] = decay * ct + _dot_tn(v, kw.astype(BF16))
        state[1] = decay * n_prev + jnp.sum(kw, axis=0, keepdims=True)
        state[2] = m_new

        ms = jnp.mean(hid * hid, axis=-1, keepdims=True)
        y_out = hid * lax.rsqrt(ms + EPS) * norm_g
        og = og_ref[rows, :].astype(F32)
        o_ref[rows, :] = (_sigmoid(og) * y_out).astype(o_ref.dtype)

    def finish():
        ct_ref[...], n_ref[...], m_ref[...] = state

    return chunk, finish


def _mlstm(proj, small, ig_t, gate_bias, conv_w, conv_b, norm_g, *, batch, seq):
    shift, cum_sel, col_sel = _mlstm_tables()
    t = proj.shape[0]
    lb = SEQ_BLOCK
    nsb = seq // lb
    h_ = MLSTM_HEADS
    qb = MLSTM_DK
    vb = MLSTM_DV
    q0 = 0
    v0 = (2 * h_ * qb) // vb

    def rows(b, h, i):
        return b * nsb + i

    in_specs = [
        pl.BlockSpec(memory_space=pltpu.SMEM),
        pl.BlockSpec((lb, qb), lambda b, h, i: (rows(b, h, i), q0 + h)),
        pl.BlockSpec((lb, qb), lambda b, h, i: (rows(b, h, i), q0 + h_ + h)),
        pl.BlockSpec((lb, vb), lambda b, h, i: (rows(b, h, i), v0 + h)),
        pl.BlockSpec((lb, vb), lambda b, h, i: (rows(b, h, i), v0 + h_ + h)),
        pl.BlockSpec((lb, SMALL_COLS), lambda b, h, i: (rows(b, h, i), 0)),
        pl.BlockSpec((lb // CHUNK, 2 * h_, CHUNK), lambda b, h, i: (rows(b, h, i), 0, 0)),
        pl.BlockSpec((CONV_WIDTH, qb), lambda b, h, i: (0, h)),
        pl.BlockSpec((CONV_WIDTH, qb), lambda b, h, i: (0, h_ + h)),
        pl.BlockSpec((1, qb), lambda b, h, i: (0, h)),
        pl.BlockSpec((1, qb), lambda b, h, i: (0, h_ + h)),
        pl.BlockSpec((None, 1, vb), lambda b, h, i: (h, 0, 0)),
        pl.BlockSpec(shift.shape, lambda b, h, i: (0, 0)),
        pl.BlockSpec(cum_sel.shape, lambda b, h, i: (0, 0)),
        pl.BlockSpec(col_sel.shape, lambda b, h, i: (0, 0)),
    ]
    scratch = [
        pltpu.VMEM((vb, qb), F32),
        pltpu.VMEM((1, qb), F32),
        pltpu.VMEM((1, 1), F32),
        pltpu.VMEM((SUBLANES, 2 * qb), F32),
    ]
    return dict(
        in_specs=in_specs,
        args=[gate_bias, proj, proj, proj, proj, small, ig_t, conv_w, conv_w, conv_b, conv_b, norm_g,
              shift, cum_sel, col_sel],
        out_shape=jax.ShapeDtypeStruct((t, h_ * vb), BF16),
        out_spec=pl.BlockSpec((lb, vb), lambda b, h, i: (rows(b, h, i), h)),
        scratch=scratch)


def _mixer_scan_kernel(*refs, n_gla_in, n_ml_in, n_gla_scratch):
    gla_in = refs[:n_gla_in]
    ml_in = refs[n_gla_in:n_gla_in + n_ml_in]
    o_gla, o_ml = refs[n_gla_in + n_ml_in:n_gla_in + n_ml_in + 2]
    scratch = refs[n_gla_in + n_ml_in + 2:]

    @pl.when(pl.program_id(2) == 0)
    def _():
        for ref in scratch:
            ref[...] = jnp.zeros_like(ref)

    gla_chunk, gla_finish = _gla_stream(*gla_in, o_gla, *scratch[:n_gla_scratch])
    ml_chunk, ml_finish = _mlstm_stream(*ml_in, o_ml, *scratch[n_gla_scratch:])
    for c in range(SEQ_BLOCK // CHUNK):
        gla_chunk(c)
        ml_chunk(c)
    gla_finish()
    ml_finish()


def _mixer_scan(gla, ml, *, batch, seq):
    assert GLA_HEADS == MLSTM_HEADS
    outs = pl.pallas_call(
        functools.partial(_mixer_scan_kernel, n_gla_in=len(gla["args"]), n_ml_in=len(ml["args"]),
                          n_gla_scratch=len(gla["scratch"])),
        out_shape=[gla["out_shape"], ml["out_shape"]],
        grid=(batch, GLA_HEADS, seq // SEQ_BLOCK),
        in_specs=gla["in_specs"] + ml["in_specs"],
        out_specs=[gla["out_spec"], ml["out_spec"]],
        scratch_shapes=gla["scratch"] + ml["scratch"],
        compiler_params=pltpu.CompilerParams(
            dimension_semantics=("parallel", "parallel", "arbitrary"),
            vmem_limit_bytes=_vmem_limit(24 << 20)),
        name="mixer_scan",
    )(*gla["args"], *ml["args"])
    return outs[0], outs[1]


def _xattn_kernel(q_ref, k_ref, v_ref, o_ref):
    s = _dot_nt(q_ref[...], k_ref[...]) * (XATTN_HEAD_DIM ** -0.5)
    p = jnp.exp(s - jnp.max(s, axis=-1, keepdims=True))
    p = p / jnp.sum(p, axis=-1, keepdims=True)
    o_ref[...] = _dot(p.astype(BF16), v_ref[...]).astype(o_ref.dtype)


def _xattn(q, k, v, *, batch, seq, tq):
    t, d = q.shape
    mem = k.shape[0] // batch
    hd = XATTN_HEAD_DIM
    nq = seq // tq
    return pl.pallas_call(
        _xattn_kernel,
        out_shape=jax.ShapeDtypeStruct((t, d), BF16),
        grid=(batch, nq, XATTN_HEADS),
        in_specs=[pl.BlockSpec((tq, hd), lambda b, i, h: (b * nq + i, h)),
                  pl.BlockSpec((mem, hd), lambda b, i, h: (b, h)),
                  pl.BlockSpec((mem, hd), lambda b, i, h: (b, h))],
        out_specs=pl.BlockSpec((tq, hd), lambda b, i, h: (b * nq + i, h)),
        compiler_params=pltpu.CompilerParams(
            dimension_semantics=("parallel", "parallel", "parallel"),
            vmem_limit_bytes=_vmem_limit(24 << 20)),
        name="cross_attention",
    )(q, k, v)


def kernel(x, mem, norm_mix_g, w_in, gla_gate_w2, gla_gate_b, gla_norm_g, mlstm_conv_w, mlstm_conv_b, mlstm_igate_b, mlstm_fgate_b, mlstm_norm_g, w_out, norm_cross_g, norm_mem_g, wq_c, wk_c, wv_c, wo_c, norm_ffn_g, w_gate, w_up, w_down, norm_final_g):
    batch, seq, d = x.shape
    mem_tokens = mem.shape[1]
    t = batch * seq
    depth = w_in.shape[0]
    gla_cols = 2 * GLA_HEADS * GLA_DK + 2 * GLA_HEADS * GLA_DV
    ml_cols = 2 * MLSTM_HEADS * MLSTM_DK + 2 * MLSTM_HEADS * MLSTM_DV
    ml_start = gla_cols + GLA_GATE_RANK

    h = x.reshape(t, d)
    mem2 = mem.reshape(batch * mem_tokens, d)
    for l in range(depth):
        w_gla = w_in[l][:, :gla_cols].astype(BF16)
        w_ml = w_in[l][:, ml_start:ml_start + ml_cols].astype(BF16)
        w_small = jnp.concatenate(
            [w_in[l][:, gla_cols:ml_start], w_in[l][:, ml_start + ml_cols:]], axis=1)
        w_small = jnp.pad(w_small, ((0, 0), (0, SMALL_COLS - w_small.shape[1]))).astype(BF16)
        w2 = gla_gate_w2[l].reshape(GLA_GATE_RANK, GLA_HEADS, GLA_DK).transpose(1, 0, 2)
        w2 = jnp.pad(w2, ((0, 0), (0, SMALL_COLS - GLA_GATE_RANK), (0, 0)))
        gate_b = gla_gate_b[l].reshape(GLA_HEADS, 1, GLA_DK)
        gla_g = gla_norm_g[l].reshape(GLA_HEADS, 1, GLA_DV)
        ml_g = mlstm_norm_g[l].reshape(MLSTM_HEADS, 1, MLSTM_DV)
        gate_bias = jnp.concatenate([mlstm_igate_b[l], mlstm_fgate_b[l]]).astype(F32)
        conv_b = mlstm_conv_b[l].reshape(1, -1)
        n1 = _rmsnorm(h, norm_mix_g[l], BF16)
        proj_gla = _matmul([n1], w_gla, BF16, tm=1024, tn=1024, name="in_proj_gla")
        proj_ml = _matmul([n1], w_ml, BF16, tm=1024, tn=1024, name="in_proj_mlstm")
        small = _matmul([n1], w_small, F32, tm=1024, tn=SMALL_COLS, name="in_proj_small")
        ig_t = small[:, IG_COL:IG_COL + 2 * MLSTM_HEADS].reshape(
            t // CHUNK, CHUNK, 2 * MLSTM_HEADS).transpose(0, 2, 1)
        o_gla, o_ml = _mixer_scan(
            _gla(proj_gla, small, w2, gate_b, gla_g, batch=batch, seq=seq),
            _mlstm(proj_ml, small, ig_t, gate_bias, mlstm_conv_w[l], conv_b, ml_g, batch=batch, seq=seq),
            batch=batch, seq=seq)
        h, hb, ssq = _matmul([o_gla, o_ml], w_out[l].astype(BF16), F32, resid=h,
                             next_gain=norm_cross_g[l], tm=1024, tn=512, name="out_proj")

        q = _norm_proj(hb, ssq, [wq_c[l]], tm=1024, tn=512, name="xattn_q")
        mem_n = _rmsnorm(mem2, norm_mem_g[l], BF16)
        k_mem = _matmul([mem_n], wk_c[l].astype(BF16), BF16, tm=512, tn=1024, name="xattn_k")
        v_mem = _matmul([mem_n], wv_c[l].astype(BF16), BF16, tm=512, tn=1024, name="xattn_v")
        attn = _xattn(q, k_mem, v_mem, batch=batch, seq=seq, tq=1024)
        h, hb, ssq = _matmul([attn], wo_c[l].astype(BF16), F32, resid=h,
                             next_gain=norm_ffn_g[l], tm=1024, tn=512, name="xattn_o")

        hid = _norm_proj(hb, ssq, [w_gate[l], w_up[l]], tm=1024, tn=FFN_TN, name="swiglu_up")
        h = _matmul([hid], w_down[l].astype(BF16), F32, resid=h, tm=512, tn=512, name="ffn_down")

    y = _rmsnorm(h, norm_final_g, x.dtype)
    return y.reshape(batch, seq, d)
```

```python
import functools
import math

import jax
import jax.numpy as jnp
import numpy as np
from jax import lax
from jax.experimental import pallas as pl
from jax.experimental.pallas import tpu as pltpu

F32 = jnp.float32
BF16 = jnp.bfloat16

D_MODEL = 4096
GROUP_WIDTH = D_MODEL // 2
GLA_HEADS = 4
GLA_DV = GROUP_WIDTH // GLA_HEADS
GLA_DK = GLA_DV // 2
GLA_GATE_RANK = 16
GLA_TAU = 16.0
MLSTM_HEADS = 4
MLSTM_DV = GROUP_WIDTH // MLSTM_HEADS
MLSTM_DK = MLSTM_DV // 2
CONV_WIDTH = 4
XATTN_HEADS = 4
XATTN_HEAD_DIM = D_MODEL // XATTN_HEADS
FFN_HIDDEN = 256 * math.ceil(8 * D_MODEL / (3 * 256))
EPS = 1e-6

V7X_VMEM_LIMIT_BYTES = 56 * 1024 * 1024
LANES = 128
SUBLANES = 8

CHUNK = 64
GLA_LEVELS = (32, 16, 8, 4, 2, 1)
LOG2_E = math.log2(math.e)
SEQ_BLOCK = 256
NORM_ROWS = 256
FFN_TN = 256
XATTN_ROWS = 256
SMALL_COLS = LANES
IG_COL = GLA_GATE_RANK
FG_COL = GLA_GATE_RANK + MLSTM_HEADS


def _vmem_limit(nbytes):
    return int(min(V7X_VMEM_LIMIT_BYTES, max(32 * 1024 * 1024, nbytes * 5 // 4 + (4 << 20))))


def _dot(a, b):
    return jnp.dot(a, b, preferred_element_type=F32)


def _dot_nt(a, b):
    return lax.dot_general(a, b, (((1,), (1,)), ((), ())), preferred_element_type=F32)


def _dot_tn(a, b):
    return lax.dot_general(a, b, (((0,), (0,)), ((), ())), preferred_element_type=F32)


def _split3(x):
    hi = x.astype(BF16)
    r = x - hi.astype(F32)
    mid = r.astype(BF16)
    lo = (r - mid.astype(F32)).astype(BF16)
    return hi, mid, lo


def _log_sigmoid(z):
    return jnp.minimum(z, 0.0) - jnp.log(1.0 + jnp.exp(-jnp.abs(z)))


def _sigmoid(z):
    return 1.0 / (1.0 + jnp.exp(-z))


def _rmsnorm_kernel(x_ref, g_ref, o_ref):
    x = x_ref[...].astype(F32)
    ms = jnp.mean(x * x, axis=-1, keepdims=True)
    o_ref[...] = (x * lax.rsqrt(ms + EPS) * g_ref[...]).astype(o_ref.dtype)


def _rmsnorm(x, g, out_dtype):
    rows, d = x.shape
    tr = min(NORM_ROWS, rows)
    nbytes = 2 * tr * d * (x.dtype.itemsize + jnp.dtype(out_dtype).itemsize) + 3 * tr * d * 4
    return pl.pallas_call(
        _rmsnorm_kernel,
        out_shape=jax.ShapeDtypeStruct((rows, d), out_dtype),
        grid=(rows // tr,),
        in_specs=[pl.BlockSpec((tr, d), lambda i: (i, 0)),
                  pl.BlockSpec((1, d), lambda i: (0, 0))],
        out_specs=pl.BlockSpec((tr, d), lambda i: (i, 0)),
        compiler_params=pltpu.CompilerParams(
            dimension_semantics=("parallel",), vmem_limit_bytes=_vmem_limit(nbytes)),
        name="rmsnorm",
    )(x, g.reshape(1, d).astype(F32))


def _mm_kernel(*refs, n_a, has_resid, has_norm):
    a_refs = refs[:n_a]
    w_ref = refs[n_a]
    pos = n_a + 1
    r_ref = refs[pos] if has_resid else None
    pos += int(has_resid)
    g_ref = refs[pos] if has_norm else None
    pos += int(has_norm)
    o_ref = refs[pos]

    kp = a_refs[0].shape[1]
    acc = _dot(a_refs[0][...], w_ref[0:kp, :])
    for p in range(1, n_a):
        acc = acc + _dot(a_refs[p][...], w_ref[p * kp:(p + 1) * kp, :])
    if has_resid:
        acc = r_ref[...] + acc
    o_ref[...] = acc.astype(o_ref.dtype)

    if has_norm:
        hb_ref, ssq_ref = refs[pos + 1], refs[pos + 2]
        hb_ref[...] = (acc * g_ref[...]).astype(hb_ref.dtype)
        part = jnp.broadcast_to(jnp.sum(acc * acc, axis=1, keepdims=True), ssq_ref.shape)
        j = pl.program_id(1)

        @pl.when(j == 0)
        def _():
            ssq_ref[...] = part

        @pl.when(j > 0)
        def _():
            ssq_ref[...] += part


def _matmul(a_list, w, out_dtype, *, resid=None, next_gain=None, n_cols=None, tm, tn, name):
    m, kp = a_list[0].shape
    kdim = w.shape[0]
    n = w.shape[1] if n_cols is None else n_cols
    n_a = len(a_list)
    assert m % tm == 0 and n % tn == 0 and kdim == n_a * kp
    in_specs = ([pl.BlockSpec((tm, kp), lambda i, j: (i, 0)) for _ in a_list]
                + [pl.BlockSpec((kdim, tn), lambda i, j: (0, j))])
    args = list(a_list) + [w]
    out_shape = [jax.ShapeDtypeStruct((m, n), out_dtype)]
    out_specs = [pl.BlockSpec((tm, tn), lambda i, j: (i, j))]
    nbytes = 2 * (tm * kdim + kdim * tn) * 2 + 2 * tm * tn * jnp.dtype(out_dtype).itemsize + 2 * tm * tn * 4
    if resid is not None:
        in_specs.append(pl.BlockSpec((tm, tn), lambda i, j: (i, j)))
        args.append(resid)
        nbytes += 2 * tm * tn * 4
    if next_gain is not None:
        in_specs.append(pl.BlockSpec((1, tn), lambda i, j: (0, j)))
        args.append(next_gain.reshape(1, n).astype(F32))
        out_shape += [jax.ShapeDtypeStruct((m, n), BF16), jax.ShapeDtypeStruct((m, LANES), F32)]
        out_specs += [pl.BlockSpec((tm, tn), lambda i, j: (i, j)),
                      pl.BlockSpec((tm, LANES), lambda i, j: (i, 0))]
        nbytes += 2 * tm * tn * 2 + 2 * tm * LANES * 4 + tm * tn * 4
    outs = pl.pallas_call(
        functools.partial(_mm_kernel, n_a=n_a, has_resid=resid is not None,
                          has_norm=next_gain is not None),
        out_shape=out_shape,
        grid=(m // tm, n // tn),
        in_specs=in_specs,
        out_specs=out_specs,
        compiler_params=pltpu.CompilerParams(
            dimension_semantics=("parallel", "arbitrary"),
            vmem_limit_bytes=_vmem_limit(nbytes)),
        name=name,
    )(*args)
    return outs if next_gain is not None else outs[0]


def _norm_proj_kernel(a_ref, ssq_ref, *refs, n_w):
    w_refs = refs[:n_w]
    o_ref = refs[n_w]
    wb_refs = refs[n_w + 1:]

    @pl.when(pl.program_id(1) == 0)
    def _():
        for w_ref, wb_ref in zip(w_refs, wb_refs):
            wb_ref[...] = w_ref[...].astype(BF16)

    a = a_ref[...]
    r = lax.rsqrt(ssq_ref[:, 0:1] * (1.0 / a.shape[1]) + EPS)
    first = r * _dot(a, wb_refs[0][...])
    if n_w == 1:
        o_ref[...] = first.astype(o_ref.dtype)
    else:
        up = r * _dot(a, wb_refs[1][...])
        o_ref[...] = (first * _sigmoid(first) * up).astype(o_ref.dtype)


def _norm_proj(a, ssq, w_list, *, tm, tn, name):
    m, kdim = a.shape
    n = w_list[0].shape[1]
    n_w = len(w_list)
    assert m % tm == 0 and n % tn == 0
    nbytes = (2 * tm * kdim * 2 + n_w * kdim * tn * (2 * 4 + 2) + 2 * tm * tn * 2
              + 2 * tm * LANES * 4 + 3 * tm * tn * 4)
    return pl.pallas_call(
        functools.partial(_norm_proj_kernel, n_w=n_w),
        out_shape=jax.ShapeDtypeStruct((m, n), BF16),
        grid=(n // tn, m // tm),
        in_specs=([pl.BlockSpec((tm, kdim), lambda j, i: (i, 0)),
                   pl.BlockSpec((tm, LANES), lambda j, i: (i, 0))]
                  + [pl.BlockSpec((kdim, tn), lambda j, i: (0, j)) for _ in w_list]),
        out_specs=pl.BlockSpec((tm, tn), lambda j, i: (i, j)),
        scratch_shapes=[pltpu.VMEM((kdim, tn), BF16) for _ in w_list],
        compiler_params=pltpu.CompilerParams(
            dimension_semantics=("parallel", "arbitrary"),
            vmem_limit_bytes=_vmem_limit(nbytes)),
        name=name,
    )(a, ssq, *w_list)


def _gla_tables():
    c = CHUNK
    row = np.arange(c)[:, None]
    u = np.arange(c)[None, :]
    sel = [u <= row, u > row]
    masks = [row == u]
    for bsz in GLA_LEVELS:
        r = (row // (2 * bsz)) * (2 * bsz) + bsz - 1
        sel.append(np.where(row <= r, (u > row) & (u <= r), (u > r) & (u <= row)))
        same_pair = (row // (2 * bsz)) == (u // (2 * bsz))
        masks.append(same_pair & ((row % (2 * bsz)) >= bsz) & ((u % (2 * bsz)) < bsz))
    sel = np.concatenate(sel, axis=0).astype(np.float32)
    return (jnp.asarray(np.concatenate([sel] * 3, axis=1), BF16),
            jnp.asarray(np.stack(masks).astype(np.float32)))


def _gla_stream(q_ref, k_ref, v_ref, g_ref, sm_ref, w2_ref, gb_ref, ng_ref, sel_ref, mask_ref,
                o_ref, st_ref):
    c_len = CHUNK
    norm_g = ng_ref[...]
    z = _dot(sm_ref[...].astype(BF16), w2_ref[...].astype(BF16)) + gb_ref[...]
    la_hi, la_mid, la_lo = _split3(_log_sigmoid(z) * (LOG2_E / GLA_TAU))
    row_k = lax.broadcasted_iota(jnp.int32, (c_len, GLA_DK), 0)
    state = [st_ref[...]]
    tmp = [dict() for _ in range(SEQ_BLOCK // c_len)]

    def rows_of(c):
        return slice(c * c_len, (c + 1) * c_len)

    def decay_table(c):
        rows = rows_of(c)
        la3 = jnp.concatenate([la_hi[rows, :], la_mid[rows, :], la_lo[rows, :]], axis=0)
        tmp[c]["e"] = jnp.exp2(_dot(sel_ref[...], la3))

    def scores(c):
        rows = rows_of(c)
        e = tmp[c].pop("e")
        q = q_ref[rows, :].astype(F32) * (GLA_DK ** -0.5)
        k = k_ref[rows, :].astype(F32)
        k_dec = (k * e[c_len:2 * c_len, :]).astype(BF16)
        attn = _dot_nt(q.astype(BF16), k.astype(BF16)) * mask_ref[0]
        for lv, bsz in enumerate(GLA_LEVELS):
            in_right = (row_k & bsz) != 0
            x = (jnp.where(in_right, q, k) * e[(2 + lv) * c_len:(3 + lv) * c_len, :]).astype(BF16)
            attn = attn + _dot_nt(x, x) * mask_ref[1 + lv]
        tmp[c].update(q_dec=(q * e[0:c_len, :]).astype(BF16),
                      decay=e[c_len - 1:c_len, :],
                      update=_dot_tn(v_ref[rows, :], k_dec),
                      intra=_dot(attn.astype(BF16), v_ref[rows, :]))

    def recurrence(c):
        tmp[c]["st_in"] = state[0].astype(BF16)
        state[0] = state[0] * tmp[c].pop("decay") + tmp[c].pop("update")

    def output(c):
        rows = rows_of(c)
        out = tmp[c].pop("intra") + _dot_nt(tmp[c].pop("q_dec"), tmp[c].pop("st_in"))
        ms = jnp.mean(out * out, axis=-1, keepdims=True)
        y = out * lax.rsqrt(ms + EPS) * norm_g
        gg = g_ref[rows, :].astype(F32)
        o_ref[rows, :] = (y * (gg * _sigmoid(gg))).astype(o_ref.dtype)

    def finish():
        st_ref[...] = state[0]

    return [decay_table, scores, recurrence, output], finish


def _gla(proj, small, w2, gate_b, norm_g, *, batch, seq):
    decay_sel, pair_mask = _gla_tables()
    t = proj.shape[0]
    lb = SEQ_BLOCK
    nsb = seq // lb
    h_ = GLA_HEADS
    qb = GLA_DK
    vb = GLA_DV

    def rows(b, h, i):
        return b * nsb + i

    in_specs = [
        pl.BlockSpec((lb, qb), lambda b, h, i: (rows(b, h, i), h)),
        pl.BlockSpec((lb, qb), lambda b, h, i: (rows(b, h, i), h_ + h)),
        pl.BlockSpec((lb, vb), lambda b, h, i: (rows(b, h, i), (2 * h_ * qb) // vb + h)),
        pl.BlockSpec((lb, vb), lambda b, h, i: (rows(b, h, i), (2 * h_ * qb) // vb + h_ + h)),
        pl.BlockSpec((lb, SMALL_COLS), lambda b, h, i: (rows(b, h, i), 0)),
        pl.BlockSpec((None, SMALL_COLS, qb), lambda b, h, i: (h, 0, 0)),
        pl.BlockSpec((None, 1, qb), lambda b, h, i: (h, 0, 0)),
        pl.BlockSpec((None, 1, vb), lambda b, h, i: (h, 0, 0)),
        pl.BlockSpec(decay_sel.shape, lambda b, h, i: (0, 0)),
        pl.BlockSpec(pair_mask.shape, lambda b, h, i: (0, 0, 0)),
    ]
    return dict(
        in_specs=in_specs,
        args=[proj, proj, proj, proj, small, w2, gate_b, norm_g, decay_sel, pair_mask],
        out_shape=jax.ShapeDtypeStruct((t, h_ * vb), BF16),
        out_spec=pl.BlockSpec((lb, vb), lambda b, h, i: (rows(b, h, i), h)),
        scratch=[pltpu.VMEM((vb, qb), F32)])


def _mlstm_tables():
    lb, c = SEQ_BLOCK, CHUNK
    t = np.arange(lb)[:, None]
    u = np.arange(lb)[None, :]
    shift = np.concatenate([(u == t - d) for d in range(1, CONV_WIDTH)], axis=0).astype(np.float32)
    tc = np.arange(c)[:, None]
    uc = np.arange(c)[None, :]
    cum = np.concatenate([uc <= tc, uc > tc], axis=0).astype(np.float32)
    col = np.concatenate([tc > uc, np.ones((c, c), bool)], axis=1).astype(np.float32)
    return (jnp.asarray(shift, BF16), jnp.asarray(np.concatenate([cum] * 3, axis=1), BF16),
            jnp.asarray(col))


def _mlstm_stream(gbias_ref, q_ref, k_ref, v_ref, og_ref, sm_ref, igt_ref, cwq_ref, cwk_ref,
                  cbq_ref, cbk_ref, ng_ref, shift_ref, cum_ref, col_ref, o_ref,
                  ct_ref, n_ref, m_ref, hist_ref):
    c_len = CHUNK
    lb = SEQ_BLOCK
    taps = CONV_WIDTH
    h = pl.program_id(1)

    x16 = jnp.concatenate([q_ref[...], k_ref[...]], axis=1)
    x = x16.astype(F32)
    shifted = _dot(shift_ref[...], x16)
    head = jnp.concatenate([hist_ref[...], x[0:SUBLANES, :]], axis=0)
    w = jnp.concatenate([cwq_ref[...], cwk_ref[...]], axis=1)
    y = jnp.concatenate([cbq_ref[...], cbk_ref[...]], axis=1) + w[taps - 1:taps, :] * x
    for d in range(1, taps):
        sh = jnp.concatenate([head[SUBLANES - d:2 * SUBLANES - d, :],
                              shifted[(d - 1) * lb + SUBLANES:d * lb, :]], axis=0)
        y = y + w[taps - 1 - d:taps - d, :] * sh
    hist_ref[...] = x[lb - SUBLANES:lb, :]
    act = y * _sigmoid(y)
    q_all = act[:, 0:MLSTM_DK]
    k_all = act[:, MLSTM_DK:2 * MLSTM_DK] * (MLSTM_DK ** -0.5)

    ib = gbias_ref[h]
    fb = gbias_ref[MLSTM_HEADS + h]
    sm = sm_ref[...]
    lane_s = lax.broadcasted_iota(jnp.int32, (lb, SMALL_COLS), 1)
    i_col_all = jnp.sum(jnp.where(lane_s == IG_COL + h, sm, 0.0), axis=1, keepdims=True) + ib
    f_col_all = jnp.sum(jnp.where(lane_s == FG_COL + h, sm, 0.0), axis=1, keepdims=True) + fb
    lf_col_all = _log_sigmoid(f_col_all)

    norm_g = ng_ref[...]
    row_c = lax.broadcasted_iota(jnp.int32, (c_len, c_len), 0)
    col_c = lax.broadcasted_iota(jnp.int32, (c_len, c_len), 1)
    causal = row_c >= col_c

    state = [ct_ref[...], n_ref[...], m_ref[...]]
    tmp = [dict() for _ in range(lb // c_len)]

    def rows_of(c):
        return slice(c * c_len, (c + 1) * c_len)

    def gate_sums(c):
        rows = rows_of(c)
        g_hi, g_mid, g_lo = _split3(lf_col_all[rows, :] * col_ref[...])
        sums = _dot(cum_ref[...], jnp.concatenate([g_hi, g_mid, g_lo], axis=0))
        i_row = igt_ref[c, pl.ds(h, 1), :] + ib
        log_d = jnp.where(causal, sums[0:c_len, 0:c_len] + i_row, -jnp.inf)
        bb_col = sums[0:c_len, c_len:c_len + 1]
        log_w = sums[c_len:2 * c_len, c_len:c_len + 1] + i_col_all[rows, :]
        tmp[c].update(log_d=log_d, bb_col=bb_col, log_w=log_w,
                      d_max=jnp.max(log_d, axis=1, keepdims=True),
                      w_max=jnp.max(log_w, axis=0, keepdims=True),
                      qk=_dot_nt(q_all[rows, :].astype(BF16), k_all[rows, :].astype(BF16)))

    def stabiliser(c):
        d = tmp[c]
        m_prev = state[2]
        bb_col = d.pop("bb_col")
        b_last = bb_col[c_len - 1:c_len, :]
        inter = bb_col + m_prev
        m = jnp.maximum(inter, d.pop("d_max"))
        m_new = jnp.maximum(b_last + m_prev, d.pop("w_max"))
        d.update(m=m, g_inter=jnp.exp(inter - m), w_col=jnp.exp(d.pop("log_w") - m_new),
                 decay=jnp.exp(b_last + m_prev - m_new))
        state[2] = m_new

    def scores(c):
        rows = rows_of(c)
        d = tmp[c]
        s = d.pop("qk") * jnp.exp(d.pop("log_d") - d["m"])
        kw = k_all[rows, :] * d.pop("w_col")
        d.update(s_sum=jnp.sum(s, axis=1, keepdims=True),
                 intra=_dot(s.astype(BF16), v_ref[rows, :]),
                 update=_dot_tn(v_ref[rows, :], kw.astype(BF16)),
                 k_sum=jnp.sum(kw, axis=0, keepdims=True))

    def recurrence(c):
        d = tmp[c]
        decay = d.pop("decay")
        d.update(ct_in=state[0].astype(BF16), n_in=state[1])
        state[0] = decay * state[0] + d.pop("update")
        state[1] = decay * state[1] + d.pop("k_sum")

    def output(c):
        rows = rows_of(c)
        d = tmp[c]
        q = q_all[rows, :]
        g_inter = d.pop("g_inter")
        num = d.pop("intra") + g_inter * _dot_nt(q.astype(BF16), d.pop("ct_in"))
        den = d.pop("s_sum") + g_inter * jnp.sum(q * d.pop("n_in"), axis=1, keepdims=True)
        hid = num / jnp.maximum(jnp.abs(den), jnp.exp(-d.pop("m")))
        ms = jnp.mean(hid * hid, axis=-1, keepdims=True)
        y_out = hid * lax.rsqrt(ms + EPS) * norm_g
        og = og_ref[rows, :].astype(F32)
        o_ref[rows, :] = (_sigmoid(og) * y_out).astype(o_ref.dtype)

    def finish():
        ct_ref[...], n_ref[...], m_ref[...] = state

    return [gate_sums, stabiliser, scores, recurrence, output], finish


def _mlstm(proj, small, ig_t, gate_bias, conv_w, conv_b, norm_g, *, batch, seq):
    shift, cum_sel, col_sel = _mlstm_tables()
    t = proj.shape[0]
    lb = SEQ_BLOCK
    nsb = seq // lb
    h_ = MLSTM_HEADS
    qb = MLSTM_DK
    vb = MLSTM_DV
    q0 = 0
    v0 = (2 * h_ * qb) // vb

    def rows(b, h, i):
        return b * nsb + i

    in_specs = [
        pl.BlockSpec(memory_space=pltpu.SMEM),
        pl.BlockSpec((lb, qb), lambda b, h, i: (rows(b, h, i), q0 + h)),
        pl.BlockSpec((lb, qb), lambda b, h, i: (rows(b, h, i), q0 + h_ + h)),
        pl.BlockSpec((lb, vb), lambda b, h, i: (rows(b, h, i), v0 + h)),
        pl.BlockSpec((lb, vb), lambda b, h, i: (rows(b, h, i), v0 + h_ + h)),
        pl.BlockSpec((lb, SMALL_COLS), lambda b, h, i: (rows(b, h, i), 0)),
        pl.BlockSpec((lb // CHUNK, 2 * h_, CHUNK), lambda b, h, i: (rows(b, h, i), 0, 0)),
        pl.BlockSpec((CONV_WIDTH, qb), lambda b, h, i: (0, h)),
        pl.BlockSpec((CONV_WIDTH, qb), lambda b, h, i: (0, h_ + h)),
        pl.BlockSpec((1, qb), lambda b, h, i: (0, h)),
        pl.BlockSpec((1, qb), lambda b, h, i: (0, h_ + h)),
        pl.BlockSpec((None, 1, vb), lambda b, h, i: (h, 0, 0)),
        pl.BlockSpec(shift.shape, lambda b, h, i: (0, 0)),
        pl.BlockSpec(cum_sel.shape, lambda b, h, i: (0, 0)),
        pl.BlockSpec(col_sel.shape, lambda b, h, i: (0, 0)),
    ]
    scratch = [
        pltpu.VMEM((vb, qb), F32),
        pltpu.VMEM((1, qb), F32),
        pltpu.VMEM((1, 1), F32),
        pltpu.VMEM((SUBLANES, 2 * qb), F32),
    ]
    return dict(
        in_specs=in_specs,
        args=[gate_bias, proj, proj, proj, proj, small, ig_t, conv_w, conv_w, conv_b, conv_b, norm_g,
              shift, cum_sel, col_sel],
        out_shape=jax.ShapeDtypeStruct((t, h_ * vb), BF16),
        out_spec=pl.BlockSpec((lb, vb), lambda b, h, i: (rows(b, h, i), h)),
        scratch=scratch)


def _mixer_scan_kernel(*refs, n_gla_in, n_ml_in, n_gla_scratch):
    gla_in = refs[:n_gla_in]
    ml_in = refs[n_gla_in:n_gla_in + n_ml_in]
    o_gla, o_ml = refs[n_gla_in + n_ml_in:n_gla_in + n_ml_in + 2]
    scratch = refs[n_gla_in + n_ml_in + 2:]

    @pl.when(pl.program_id(2) == 0)
    def _():
        for ref in scratch:
            ref[...] = jnp.zeros_like(ref)

    gla_stages, gla_finish = _gla_stream(*gla_in, o_gla, *scratch[:n_gla_scratch])
    ml_stages, ml_finish = _mlstm_stream(*ml_in, o_ml, *scratch[n_gla_scratch:])
    lead = len(ml_stages) - len(gla_stages)
    for k, ml_stage in enumerate(ml_stages):
        for c in range(SEQ_BLOCK // CHUNK):
            ml_stage(c)
            if k >= lead:
                gla_stages[k - lead](c)
    gla_finish()
    ml_finish()


def _mixer_scan(gla, ml, *, batch, seq):
    assert GLA_HEADS == MLSTM_HEADS
    outs = pl.pallas_call(
        functools.partial(_mixer_scan_kernel, n_gla_in=len(gla["args"]), n_ml_in=len(ml["args"]),
                          n_gla_scratch=len(gla["scratch"])),
        out_shape=[gla["out_shape"], ml["out_shape"]],
        grid=(batch, GLA_HEADS, seq // SEQ_BLOCK),
        in_specs=gla["in_specs"] + ml["in_specs"],
        out_specs=[gla["out_spec"], ml["out_spec"]],
        scratch_shapes=gla["scratch"] + ml["scratch"],
        compiler_params=pltpu.CompilerParams(
            dimension_semantics=("parallel", "parallel", "arbitrary"),
            vmem_limit_bytes=_vmem_limit(24 << 20)),
        name="mixer_scan",
    )(*gla["args"], *ml["args"])
    return outs[0], outs[1]


def _xattn_kernel(q_ref, k_ref, v_ref, o_ref):
    groups = [slice(g * XATTN_ROWS, (g + 1) * XATTN_ROWS) for g in range(q_ref.shape[0] // XATTN_ROWS)]
    scores = [_dot_nt(q_ref[rows, :], k_ref[...]) * (XATTN_HEAD_DIM ** -0.5) for rows in groups]
    probs = []
    for s in scores:
        p = jnp.exp(s - jnp.max(s, axis=-1, keepdims=True))
        probs.append((p / jnp.sum(p, axis=-1, keepdims=True)).astype(BF16))
    for rows, p in zip(groups, probs):
        o_ref[rows, :] = _dot(p, v_ref[...]).astype(o_ref.dtype)


def _xattn(q, k, v, *, batch, seq, tq):
    t, d = q.shape
    mem = k.shape[0] // batch
    hd = XATTN_HEAD_DIM
    nq = seq // tq
    return pl.pallas_call(
        _xattn_kernel,
        out_shape=jax.ShapeDtypeStruct((t, d), BF16),
        grid=(batch, nq, XATTN_HEADS),
        in_specs=[pl.BlockSpec((tq, hd), lambda b, i, h: (b * nq + i, h)),
                  pl.BlockSpec((mem, hd), lambda b, i, h: (b, h)),
                  pl.BlockSpec((mem, hd), lambda b, i, h: (b, h))],
        out_specs=pl.BlockSpec((tq, hd), lambda b, i, h: (b * nq + i, h)),
        compiler_params=pltpu.CompilerParams(
            dimension_semantics=("parallel", "parallel", "parallel"),
            vmem_limit_bytes=_vmem_limit(24 << 20)),
        name="cross_attention",
    )(q, k, v)


def kernel(x, mem, norm_mix_g, w_in, gla_gate_w2, gla_gate_b, gla_norm_g, mlstm_conv_w, mlstm_conv_b, mlstm_igate_b, mlstm_fgate_b, mlstm_norm_g, w_out, norm_cross_g, norm_mem_g, wq_c, wk_c, wv_c, wo_c, norm_ffn_g, w_gate, w_up, w_down, norm_final_g):
    batch, seq, d = x.shape
    mem_tokens = mem.shape[1]
    t = batch * seq
    depth = w_in.shape[0]
    gla_cols = 2 * GLA_HEADS * GLA_DK + 2 * GLA_HEADS * GLA_DV
    ml_cols = 2 * MLSTM_HEADS * MLSTM_DK + 2 * MLSTM_HEADS * MLSTM_DV
    ml_start = gla_cols + GLA_GATE_RANK

    h = x.reshape(t, d)
    mem2 = mem.reshape(batch * mem_tokens, d)
    for l in range(depth):
        w_in16 = w_in[l].astype(BF16)
        w_ml = w_in16[:, ml_start:ml_start + ml_cols]
        w_small = jnp.concatenate(
            [w_in[l][:, gla_cols:ml_start], w_in[l][:, ml_start + ml_cols:]], axis=1)
        w_small = jnp.pad(w_small, ((0, 0), (0, SMALL_COLS - w_small.shape[1]))).astype(BF16)
        w2 = gla_gate_w2[l].reshape(GLA_GATE_RANK, GLA_HEADS, GLA_DK).transpose(1, 0, 2)
        w2 = jnp.pad(w2, ((0, 0), (0, SMALL_COLS - GLA_GATE_RANK), (0, 0)))
        gate_b = gla_gate_b[l].reshape(GLA_HEADS, 1, GLA_DK)
        gla_g = gla_norm_g[l].reshape(GLA_HEADS, 1, GLA_DV)
        ml_g = mlstm_norm_g[l].reshape(MLSTM_HEADS, 1, MLSTM_DV)
        gate_bias = jnp.concatenate([mlstm_igate_b[l], mlstm_fgate_b[l]]).astype(F32)
        conv_b = mlstm_conv_b[l].reshape(1, -1)

        n1 = _rmsnorm(h, norm_mix_g[l], BF16)
        proj_gla = _matmul([n1], w_in16, BF16, n_cols=gla_cols, tm=1024, tn=1024, name="in_proj_gla")
        proj_ml = _matmul([n1], w_ml, BF16, tm=1024, tn=1024, name="in_proj_mlstm")
        small = _matmul([n1], w_small, F32, tm=1024, tn=SMALL_COLS, name="in_proj_small")
        ig_t = small[:, IG_COL:IG_COL + 2 * MLSTM_HEADS].reshape(
            t // CHUNK, CHUNK, 2 * MLSTM_HEADS).transpose(0, 2, 1)
        o_gla, o_ml = _mixer_scan(
            _gla(proj_gla, small, w2, gate_b, gla_g, batch=batch, seq=seq),
            _mlstm(proj_ml, small, ig_t, gate_bias, mlstm_conv_w[l], conv_b, ml_g, batch=batch, seq=seq),
            batch=batch, seq=seq)
        h, hb, ssq = _matmul([o_gla, o_ml], w_out[l].astype(BF16), F32, resid=h,
                             next_gain=norm_cross_g[l], tm=1024, tn=512, name="out_proj")

        q = _norm_proj(hb, ssq, [wq_c[l]], tm=1024, tn=512, name="xattn_q")
        mem_n = _rmsnorm(mem2, norm_mem_g[l], BF16)
        k_mem = _matmul([mem_n], wk_c[l].astype(BF16), BF16, tm=512, tn=1024, name="xattn_k")
        v_mem = _matmul([mem_n], wv_c[l].astype(BF16), BF16, tm=512, tn=1024, name="xattn_v")
        attn = _xattn(q, k_mem, v_mem, batch=batch, seq=seq, tq=1024)
        h, hb, ssq = _matmul([attn], wo_c[l].astype(BF16), F32, resid=h,
                             next_gain=norm_ffn_g[l], tm=1024, tn=512, name="xattn_o")

        hid = _norm_proj(hb, ssq, [w_gate[l], w_up[l]], tm=1024, tn=FFN_TN, name="swiglu_up")
        h = _matmul([hid], w_down[l].astype(BF16), F32, resid=h, tm=512, tn=512, name="ffn_down")

    y = _rmsnorm(h, norm_final_g, x.dtype)
    return y.reshape(batch, seq, d)
```

```python
import functools
import math

import jax
import jax.numpy as jnp
import numpy as np
from jax import lax
from jax.experimental import pallas as pl
from jax.experimental.pallas import tpu as pltpu

F32 = jnp.float32
BF16 = jnp.bfloat16

D_MODEL = 4096
GROUP_WIDTH = D_MODEL // 2
GLA_HEADS = 4
GLA_DV = GROUP_WIDTH // GLA_HEADS
GLA_DK = GLA_DV // 2
GLA_GATE_RANK = 16
GLA_TAU = 16.0
MLSTM_HEADS = 4
MLSTM_DV = GROUP_WIDTH // MLSTM_HEADS
MLSTM_DK = MLSTM_DV // 2
CONV_WIDTH = 4
XATTN_HEADS = 4
XATTN_HEAD_DIM = D_MODEL // XATTN_HEADS
FFN_HIDDEN = 256 * math.ceil(8 * D_MODEL / (3 * 256))
EPS = 1e-6

V7X_VMEM_LIMIT_BYTES = 56 * 1024 * 1024
LANES = 128
SUBLANES = 8

CHUNK = 64
GLA_LEVELS = (32, 16, 8, 4, 2, 1)
LOG2_E = math.log2(math.e)
SEQ_BLOCK = 512
CONV_BLOCK = 256
NORM_ROWS = 256
FFN_TN = 256
XATTN_ROWS = 256
SMALL_COLS = LANES
IG_COL = GLA_GATE_RANK
FG_COL = GLA_GATE_RANK + MLSTM_HEADS


def _vmem_limit(nbytes):
    return int(min(V7X_VMEM_LIMIT_BYTES, max(32 * 1024 * 1024, nbytes * 5 // 4 + (4 << 20))))


def _dot(a, b):
    return jnp.dot(a, b, preferred_element_type=F32)


def _dot_nt(a, b):
    return lax.dot_general(a, b, (((1,), (1,)), ((), ())), preferred_element_type=F32)


def _dot_tn(a, b):
    return lax.dot_general(a, b, (((0,), (0,)), ((), ())), preferred_element_type=F32)


def _split3(x):
    hi = x.astype(BF16)
    r = x - hi.astype(F32)
    mid = r.astype(BF16)
    lo = (r - mid.astype(F32)).astype(BF16)
    return hi, mid, lo


def _log_sigmoid(z):
    return jnp.minimum(z, 0.0) - jnp.log(1.0 + jnp.exp(-jnp.abs(z)))


def _sigmoid(z):
    return 1.0 / (1.0 + jnp.exp(-z))


def _rmsnorm_kernel(x_ref, g_ref, o_ref):
    x = x_ref[...].astype(F32)
    ms = jnp.mean(x * x, axis=-1, keepdims=True)
    o_ref[...] = (x * lax.rsqrt(ms + EPS) * g_ref[...]).astype(o_ref.dtype)


def _rmsnorm(x, g, out_dtype):
    rows, d = x.shape
    tr = min(NORM_ROWS, rows)
    nbytes = 2 * tr * d * (x.dtype.itemsize + jnp.dtype(out_dtype).itemsize) + 3 * tr * d * 4
    return pl.pallas_call(
        _rmsnorm_kernel,
        out_shape=jax.ShapeDtypeStruct((rows, d), out_dtype),
        grid=(rows // tr,),
        in_specs=[pl.BlockSpec((tr, d), lambda i: (i, 0)),
                  pl.BlockSpec((1, d), lambda i: (0, 0))],
        out_specs=pl.BlockSpec((tr, d), lambda i: (i, 0)),
        compiler_params=pltpu.CompilerParams(
            dimension_semantics=("parallel",), vmem_limit_bytes=_vmem_limit(nbytes)),
        name="rmsnorm",
    )(x, g.reshape(1, d).astype(F32))


def _mm_kernel(*refs, n_a, has_resid, has_norm):
    a_refs = refs[:n_a]
    w_ref = refs[n_a]
    pos = n_a + 1
    r_ref = refs[pos] if has_resid else None
    pos += int(has_resid)
    g_ref = refs[pos] if has_norm else None
    pos += int(has_norm)
    o_ref = refs[pos]

    kp = a_refs[0].shape[1]
    acc = _dot(a_refs[0][...], w_ref[0:kp, :])
    for p in range(1, n_a):
        acc = acc + _dot(a_refs[p][...], w_ref[p * kp:(p + 1) * kp, :])
    if has_resid:
        acc = r_ref[...] + acc
    o_ref[...] = acc.astype(o_ref.dtype)

    if has_norm:
        hb_ref, ssq_ref = refs[pos + 1], refs[pos + 2]
        hb_ref[...] = (acc * g_ref[...]).astype(hb_ref.dtype)
        part = jnp.broadcast_to(jnp.sum(acc * acc, axis=1, keepdims=True), ssq_ref.shape)
        j = pl.program_id(1)

        @pl.when(j == 0)
        def _():
            ssq_ref[...] = part

        @pl.when(j > 0)
        def _():
            ssq_ref[...] += part


def _matmul(a_list, w, out_dtype, *, resid=None, next_gain=None, n_cols=None, tm, tn, name):
    m, kp = a_list[0].shape
    kdim = w.shape[0]
    n = w.shape[1] if n_cols is None else n_cols
    n_a = len(a_list)
    assert m % tm == 0 and n % tn == 0 and kdim == n_a * kp
    in_specs = ([pl.BlockSpec((tm, kp), lambda i, j: (i, 0)) for _ in a_list]
                + [pl.BlockSpec((kdim, tn), lambda i, j: (0, j))])
    args = list(a_list) + [w]
    out_shape = [jax.ShapeDtypeStruct((m, n), out_dtype)]
    out_specs = [pl.BlockSpec((tm, tn), lambda i, j: (i, j))]
    nbytes = 2 * (tm * kdim + kdim * tn) * 2 + 2 * tm * tn * jnp.dtype(out_dtype).itemsize + 2 * tm * tn * 4
    if resid is not None:
        in_specs.append(pl.BlockSpec((tm, tn), lambda i, j: (i, j)))
        args.append(resid)
        nbytes += 2 * tm * tn * 4
    if next_gain is not None:
        in_specs.append(pl.BlockSpec((1, tn), lambda i, j: (0, j)))
        args.append(next_gain.reshape(1, n).astype(F32))
        out_shape += [jax.ShapeDtypeStruct((m, n), BF16), jax.ShapeDtypeStruct((m, LANES), F32)]
        out_specs += [pl.BlockSpec((tm, tn), lambda i, j: (i, j)),
                      pl.BlockSpec((tm, LANES), lambda i, j: (i, 0))]
        nbytes += 2 * tm * tn * 2 + 2 * tm * LANES * 4 + tm * tn * 4
    outs = pl.pallas_call(
        functools.partial(_mm_kernel, n_a=n_a, has_resid=resid is not None,
                          has_norm=next_gain is not None),
        out_shape=out_shape,
        grid=(m // tm, n // tn),
        in_specs=in_specs,
        out_specs=out_specs,
        compiler_params=pltpu.CompilerParams(
            dimension_semantics=("parallel", "arbitrary"),
            vmem_limit_bytes=_vmem_limit(nbytes)),
        name=name,
    )(*args)
    return outs if next_gain is not None else outs[0]


def _norm_proj_kernel(a_ref, ssq_ref, *refs, n_w):
    w_refs = refs[:n_w]
    o_ref = refs[n_w]
    wb_refs = refs[n_w + 1:]

    @pl.when(pl.program_id(1) == 0)
    def _():
        for w_ref, wb_ref in zip(w_refs, wb_refs):
            wb_ref[...] = w_ref[...].astype(BF16)

    a = a_ref[...]
    r = lax.rsqrt(ssq_ref[:, 0:1] * (1.0 / a.shape[1]) + EPS)
    first = r * _dot(a, wb_refs[0][...])
    if n_w == 1:
        o_ref[...] = first.astype(o_ref.dtype)
    else:
        up = r * _dot(a, wb_refs[1][...])
        o_ref[...] = (first * _sigmoid(first) * up).astype(o_ref.dtype)


def _norm_proj(a, ssq, w_list, *, tm, tn, name):
    m, kdim = a.shape
    n = w_list[0].shape[1]
    n_w = len(w_list)
    assert m % tm == 0 and n % tn == 0
    nbytes = (2 * tm * kdim * 2 + n_w * kdim * tn * (2 * 4 + 2) + 2 * tm * tn * 2
              + 2 * tm * LANES * 4 + 3 * tm * tn * 4)
    return pl.pallas_call(
        functools.partial(_norm_proj_kernel, n_w=n_w),
        out_shape=jax.ShapeDtypeStruct((m, n), BF16),
        grid=(n // tn, m // tm),
        in_specs=([pl.BlockSpec((tm, kdim), lambda j, i: (i, 0)),
                   pl.BlockSpec((tm, LANES), lambda j, i: (i, 0))]
                  + [pl.BlockSpec((kdim, tn), lambda j, i: (0, j)) for _ in w_list]),
        out_specs=pl.BlockSpec((tm, tn), lambda j, i: (i, j)),
        scratch_shapes=[pltpu.VMEM((kdim, tn), BF16) for _ in w_list],
        compiler_params=pltpu.CompilerParams(
            dimension_semantics=("parallel", "arbitrary"),
            vmem_limit_bytes=_vmem_limit(nbytes)),
        name=name,
    )(a, ssq, *w_list)


def _gla_tables():
    c = CHUNK
    row = np.arange(c)[:, None]
    u = np.arange(c)[None, :]
    sel = [u <= row, u > row]
    masks = [row == u]
    for bsz in GLA_LEVELS:
        r = (row // (2 * bsz)) * (2 * bsz) + bsz - 1
        sel.append(np.where(row <= r, (u > row) & (u <= r), (u > r) & (u <= row)))
        same_pair = (row // (2 * bsz)) == (u // (2 * bsz))
        masks.append(same_pair & ((row % (2 * bsz)) >= bsz) & ((u % (2 * bsz)) < bsz))
    sel = np.concatenate(sel, axis=0).astype(np.float32)
    return (jnp.asarray(np.concatenate([sel] * 3, axis=1), BF16),
            jnp.asarray(np.stack(masks).astype(np.float32)))


def _gla_stream(q_ref, k_ref, v_ref, g_ref, sm_ref, w2_ref, gb_ref, ng_ref, sel_ref, mask_ref,
                o_ref, st_ref):
    c_len = CHUNK
    norm_g = ng_ref[...]
    z = _dot(sm_ref[...].astype(BF16), w2_ref[...].astype(BF16)) + gb_ref[...]
    la_hi, la_mid, la_lo = _split3(_log_sigmoid(z) * (LOG2_E / GLA_TAU))
    row_k = lax.broadcasted_iota(jnp.int32, (c_len, GLA_DK), 0)
    state = [st_ref[...]]
    tmp = [dict() for _ in range(SEQ_BLOCK // c_len)]

    def rows_of(c):
        return slice(c * c_len, (c + 1) * c_len)

    def decay_table(c):
        rows = rows_of(c)
        la3 = jnp.concatenate([la_hi[rows, :], la_mid[rows, :], la_lo[rows, :]], axis=0)
        tmp[c]["e"] = jnp.exp2(_dot(sel_ref[...], la3))

    def scores(c):
        rows = rows_of(c)
        e = tmp[c].pop("e")
        q = q_ref[rows, :].astype(F32) * (GLA_DK ** -0.5)
        k = k_ref[rows, :].astype(F32)
        k_dec = (k * e[c_len:2 * c_len, :]).astype(BF16)
        attn = _dot_nt(q.astype(BF16), k.astype(BF16)) * mask_ref[0]
        for lv, bsz in enumerate(GLA_LEVELS):
            in_right = (row_k & bsz) != 0
            x = (jnp.where(in_right, q, k) * e[(2 + lv) * c_len:(3 + lv) * c_len, :]).astype(BF16)
            attn = attn + _dot_nt(x, x) * mask_ref[1 + lv]
        tmp[c].update(q_dec=(q * e[0:c_len, :]).astype(BF16), k_dec=k_dec,
                      decay=e[c_len - 1:c_len, :],
                      intra=_dot(attn.astype(BF16), v_ref[rows, :]))

    def recurrence(c):
        tmp[c]["st_in"] = state[0].astype(BF16)
        state[0] = state[0] * tmp[c].pop("decay") + _dot_tn(v_ref[rows_of(c), :], tmp[c].pop("k_dec"))

    def output(c):
        rows = rows_of(c)
        out = tmp[c].pop("intra") + _dot_nt(tmp[c].pop("q_dec"), tmp[c].pop("st_in"))
        ms = jnp.mean(out * out, axis=-1, keepdims=True)
        y = out * lax.rsqrt(ms + EPS) * norm_g
        gg = g_ref[rows, :].astype(F32)
        o_ref[rows, :] = (y * (gg * _sigmoid(gg))).astype(o_ref.dtype)

    def finish():
        st_ref[...] = state[0]

    return [decay_table, scores, recurrence, output], finish


def _gla(proj, small, w2, gate_b, norm_g, *, batch, seq):
    decay_sel, pair_mask = _gla_tables()
    t = proj.shape[0]
    lb = SEQ_BLOCK
    nsb = seq // lb
    h_ = GLA_HEADS
    qb = GLA_DK
    vb = GLA_DV

    def rows(b, h, i):
        return b * nsb + i

    in_specs = [
        pl.BlockSpec((lb, qb), lambda b, h, i: (rows(b, h, i), h)),
        pl.BlockSpec((lb, qb), lambda b, h, i: (rows(b, h, i), h_ + h)),
        pl.BlockSpec((lb, vb), lambda b, h, i: (rows(b, h, i), (2 * h_ * qb) // vb + h)),
        pl.BlockSpec((lb, vb), lambda b, h, i: (rows(b, h, i), (2 * h_ * qb) // vb + h_ + h)),
        pl.BlockSpec((lb, SMALL_COLS), lambda b, h, i: (rows(b, h, i), 0)),
        pl.BlockSpec((None, SMALL_COLS, qb), lambda b, h, i: (h, 0, 0)),
        pl.BlockSpec((None, 1, qb), lambda b, h, i: (h, 0, 0)),
        pl.BlockSpec((None, 1, vb), lambda b, h, i: (h, 0, 0)),
        pl.BlockSpec(decay_sel.shape, lambda b, h, i: (0, 0)),
        pl.BlockSpec(pair_mask.shape, lambda b, h, i: (0, 0, 0)),
    ]
    return dict(
        in_specs=in_specs,
        args=[proj, proj, proj, proj, small, w2, gate_b, norm_g, decay_sel, pair_mask],
        out_shape=jax.ShapeDtypeStruct((t, h_ * vb), BF16),
        out_spec=pl.BlockSpec((lb, vb), lambda b, h, i: (rows(b, h, i), h)),
        scratch=[pltpu.VMEM((vb, qb), F32)])


def _mlstm_tables():
    lb, c = CONV_BLOCK, CHUNK
    t = np.arange(lb)[:, None]
    u = np.arange(lb)[None, :]
    shift = np.concatenate([(u == t - d) for d in range(1, CONV_WIDTH)], axis=0).astype(np.float32)
    tc = np.arange(c)[:, None]
    uc = np.arange(c)[None, :]
    cum = np.concatenate([uc <= tc, uc > tc], axis=0).astype(np.float32)
    col = np.concatenate([tc > uc, np.ones((c, c), bool)], axis=1).astype(np.float32)
    return (jnp.asarray(shift, BF16), jnp.asarray(np.concatenate([cum] * 3, axis=1), BF16),
            jnp.asarray(col))


def _mlstm_stream(gbias_ref, q_ref, k_ref, v_ref, og_ref, sm_ref, igt_ref, cwq_ref, cwk_ref,
                  cbq_ref, cbk_ref, ng_ref, shift_ref, cum_ref, col_ref, o_ref,
                  ct_ref, n_ref, m_ref, hist_ref):
    c_len = CHUNK
    lb = SEQ_BLOCK
    taps = CONV_WIDTH
    h = pl.program_id(1)

    cb = CONV_BLOCK
    w = jnp.concatenate([cwq_ref[...], cwk_ref[...]], axis=1)
    bias = jnp.concatenate([cbq_ref[...], cbk_ref[...]], axis=1)
    tail = hist_ref[...]
    acts = []
    for s in range(lb // cb):
        x16 = jnp.concatenate([q_ref[s * cb:(s + 1) * cb, :], k_ref[s * cb:(s + 1) * cb, :]], axis=1)
        x = x16.astype(F32)
        shifted = _dot(shift_ref[...], x16)
        head = jnp.concatenate([tail, x[0:SUBLANES, :]], axis=0)
        y = bias + w[taps - 1:taps, :] * x
        for d in range(1, taps):
            sh = jnp.concatenate([head[SUBLANES - d:2 * SUBLANES - d, :],
                                  shifted[(d - 1) * cb + SUBLANES:d * cb, :]], axis=0)
            y = y + w[taps - 1 - d:taps - d, :] * sh
        tail = x[cb - SUBLANES:cb, :]
        acts.append(y * _sigmoid(y))
    hist_ref[...] = tail
    act = jnp.concatenate(acts, axis=0)
    q_all = act[:, 0:MLSTM_DK]
    k_all = act[:, MLSTM_DK:2 * MLSTM_DK] * (MLSTM_DK ** -0.5)

    ib = gbias_ref[h]
    fb = gbias_ref[MLSTM_HEADS + h]
    sm = sm_ref[...]
    lane_s = lax.broadcasted_iota(jnp.int32, (lb, SMALL_COLS), 1)
    i_col_all = jnp.sum(jnp.where(lane_s == IG_COL + h, sm, 0.0), axis=1, keepdims=True) + ib
    f_col_all = jnp.sum(jnp.where(lane_s == FG_COL + h, sm, 0.0), axis=1, keepdims=True) + fb
    lf_col_all = _log_sigmoid(f_col_all)

    norm_g = ng_ref[...]
    row_c = lax.broadcasted_iota(jnp.int32, (c_len, c_len), 0)
    col_c = lax.broadcasted_iota(jnp.int32, (c_len, c_len), 1)
    causal = row_c >= col_c

    state = [ct_ref[...], n_ref[...], m_ref[...]]
    tmp = [dict() for _ in range(lb // c_len)]

    def rows_of(c):
        return slice(c * c_len, (c + 1) * c_len)

    def gate_sums(c):
        rows = rows_of(c)
        g_hi, g_mid, g_lo = _split3(lf_col_all[rows, :] * col_ref[...])
        sums = _dot(cum_ref[...], jnp.concatenate([g_hi, g_mid, g_lo], axis=0))
        i_row = igt_ref[c, pl.ds(h, 1), :] + ib
        log_d = jnp.where(causal, sums[0:c_len, 0:c_len] + i_row, -jnp.inf)
        bb_col = sums[0:c_len, c_len:c_len + 1]
        log_w = sums[c_len:2 * c_len, c_len:c_len + 1] + i_col_all[rows, :]
        tmp[c].update(log_d=log_d, bb_col=bb_col, log_w=log_w,
                      d_max=jnp.max(log_d, axis=1, keepdims=True),
                      w_max=jnp.max(log_w, axis=0, keepdims=True),
                      qk=_dot_nt(q_all[rows, :].astype(BF16), k_all[rows, :].astype(BF16)))

    def stabiliser(c):
        d = tmp[c]
        m_prev = state[2]
        bb_col = d.pop("bb_col")
        b_last = bb_col[c_len - 1:c_len, :]
        inter = bb_col + m_prev
        m = jnp.maximum(inter, d.pop("d_max"))
        m_new = jnp.maximum(b_last + m_prev, d.pop("w_max"))
        d.update(m=m, g_inter=jnp.exp(inter - m), w_col=jnp.exp(d.pop("log_w") - m_new),
                 decay=jnp.exp(b_last + m_prev - m_new))
        state[2] = m_new

    def scores(c):
        rows = rows_of(c)
        d = tmp[c]
        s = d.pop("qk") * jnp.exp(d.pop("log_d") - d["m"])
        kw = k_all[rows, :] * d.pop("w_col")
        d.update(s_sum=jnp.sum(s, axis=1, keepdims=True),
                 intra=_dot(s.astype(BF16), v_ref[rows, :]),
                 kw=kw.astype(BF16),
                 k_sum=jnp.sum(kw, axis=0, keepdims=True))

    def recurrence(c):
        d = tmp[c]
        decay = d.pop("decay")
        d.update(ct_in=state[0].astype(BF16), n_in=state[1])
        state[0] = decay * state[0] + _dot_tn(v_ref[rows_of(c), :], d.pop("kw"))
        state[1] = decay * state[1] + d.pop("k_sum")

    def output(c):
        rows = rows_of(c)
        d = tmp[c]
        q = q_all[rows, :]
        g_inter = d.pop("g_inter")
        num = d.pop("intra") + g_inter * _dot_nt(q.astype(BF16), d.pop("ct_in"))
        den = d.pop("s_sum") + g_inter * jnp.sum(q * d.pop("n_in"), axis=1, keepdims=True)
        hid = num / jnp.maximum(jnp.abs(den), jnp.exp(-d.pop("m")))
        ms = jnp.mean(hid * hid, axis=-1, keepdims=True)
        y_out = hid * lax.rsqrt(ms + EPS) * norm_g
        og = og_ref[rows, :].astype(F32)
        o_ref[rows, :] = (_sigmoid(og) * y_out).astype(o_ref.dtype)

    def finish():
        ct_ref[...], n_ref[...], m_ref[...] = state

    return [gate_sums, stabiliser, scores, recurrence, output], finish


def _mlstm(proj, small, ig_t, gate_bias, conv_w, conv_b, norm_g, *, batch, seq):
    shift, cum_sel, col_sel = _mlstm_tables()
    t = proj.shape[0]
    lb = SEQ_BLOCK
    nsb = seq // lb
    h_ = MLSTM_HEADS
    qb = MLSTM_DK
    vb = MLSTM_DV
    q0 = 0
    v0 = (2 * h_ * qb) // vb

    def rows(b, h, i):
        return b * nsb + i

    in_specs = [
        pl.BlockSpec(memory_space=pltpu.SMEM),
        pl.BlockSpec((lb, qb), lambda b, h, i: (rows(b, h, i), q0 + h)),
        pl.BlockSpec((lb, qb), lambda b, h, i: (rows(b, h, i), q0 + h_ + h)),
        pl.BlockSpec((lb, vb), lambda b, h, i: (rows(b, h, i), v0 + h)),
        pl.BlockSpec((lb, vb), lambda b, h, i: (rows(b, h, i), v0 + h_ + h)),
        pl.BlockSpec((lb, SMALL_COLS), lambda b, h, i: (rows(b, h, i), 0)),
        pl.BlockSpec((lb // CHUNK, 2 * h_, CHUNK), lambda b, h, i: (rows(b, h, i), 0, 0)),
        pl.BlockSpec((CONV_WIDTH, qb), lambda b, h, i: (0, h)),
        pl.BlockSpec((CONV_WIDTH, qb), lambda b, h, i: (0, h_ + h)),
        pl.BlockSpec((1, qb), lambda b, h, i: (0, h)),
        pl.BlockSpec((1, qb), lambda b, h, i: (0, h_ + h)),
        pl.BlockSpec((None, 1, vb), lambda b, h, i: (h, 0, 0)),
        pl.BlockSpec(shift.shape, lambda b, h, i: (0, 0)),
        pl.BlockSpec(cum_sel.shape, lambda b, h, i: (0, 0)),
        pl.BlockSpec(col_sel.shape, lambda b, h, i: (0, 0)),
    ]
    scratch = [
        pltpu.VMEM((vb, qb), F32),
        pltpu.VMEM((1, qb), F32),
        pltpu.VMEM((1, 1), F32),
        pltpu.VMEM((SUBLANES, 2 * qb), F32),
    ]
    return dict(
        in_specs=in_specs,
        args=[gate_bias, proj, proj, proj, proj, small, ig_t, conv_w, conv_w, conv_b, conv_b, norm_g,
              shift, cum_sel, col_sel],
        out_shape=jax.ShapeDtypeStruct((t, h_ * vb), BF16),
        out_spec=pl.BlockSpec((lb, vb), lambda b, h, i: (rows(b, h, i), h)),
        scratch=scratch)


def _mixer_scan_kernel(*refs, n_gla_in, n_ml_in, n_gla_scratch):
    gla_in = refs[:n_gla_in]
    ml_in = refs[n_gla_in:n_gla_in + n_ml_in]
    o_gla, o_ml = refs[n_gla_in + n_ml_in:n_gla_in + n_ml_in + 2]
    scratch = refs[n_gla_in + n_ml_in + 2:]

    @pl.when(pl.program_id(2) == 0)
    def _():
        for ref in scratch:
            ref[...] = jnp.zeros_like(ref)

    gla_stages, gla_finish = _gla_stream(*gla_in, o_gla, *scratch[:n_gla_scratch])
    ml_stages, ml_finish = _mlstm_stream(*ml_in, o_ml, *scratch[n_gla_scratch:])
    n_chunks = SEQ_BLOCK // CHUNK
    lead = len(ml_stages) - len(gla_stages)
    for t in range(n_chunks + len(ml_stages) - 1):
        for k in reversed(range(len(ml_stages))):
            c = t - k
            if 0 <= c < n_chunks:
                ml_stages[k](c)
                if k >= lead:
                    gla_stages[k - lead](c)
    gla_finish()
    ml_finish()


def _mixer_scan(gla, ml, *, batch, seq):
    assert GLA_HEADS == MLSTM_HEADS
    outs = pl.pallas_call(
        functools.partial(_mixer_scan_kernel, n_gla_in=len(gla["args"]), n_ml_in=len(ml["args"]),
                          n_gla_scratch=len(gla["scratch"])),
        out_shape=[gla["out_shape"], ml["out_shape"]],
        grid=(batch, GLA_HEADS, seq // SEQ_BLOCK),
        in_specs=gla["in_specs"] + ml["in_specs"],
        out_specs=[gla["out_spec"], ml["out_spec"]],
        scratch_shapes=gla["scratch"] + ml["scratch"],
        compiler_params=pltpu.CompilerParams(
            dimension_semantics=("parallel", "parallel", "arbitrary"),
            vmem_limit_bytes=_vmem_limit(24 << 20)),
        name="mixer_scan",
    )(*gla["args"], *ml["args"])
    return outs[0], outs[1]


def _xattn_kernel(q_ref, k_ref, v_ref, o_ref):
    groups = [slice(g * XATTN_ROWS, (g + 1) * XATTN_ROWS) for g in range(q_ref.shape[0] // XATTN_ROWS)]
    scores = [_dot_nt(q_ref[rows, :], k_ref[...]) * (XATTN_HEAD_DIM ** -0.5) for rows in groups]
    probs = []
    for s in scores:
        p = jnp.exp(s - jnp.max(s, axis=-1, keepdims=True))
        probs.append((p / jnp.sum(p, axis=-1, keepdims=True)).astype(BF16))
    for rows, p in zip(groups, probs):
        o_ref[rows, :] = _dot(p, v_ref[...]).astype(o_ref.dtype)


def _xattn(q, k, v, *, batch, seq, tq):
    t, d = q.shape
    mem = k.shape[0] // batch
    hd = XATTN_HEAD_DIM
    nq = seq // tq
    return pl.pallas_call(
        _xattn_kernel,
        out_shape=jax.ShapeDtypeStruct((t, d), BF16),
        grid=(batch, nq, XATTN_HEADS),
        in_specs=[pl.BlockSpec((tq, hd), lambda b, i, h: (b * nq + i, h)),
                  pl.BlockSpec((mem, hd), lambda b, i, h: (b, h)),
                  pl.BlockSpec((mem, hd), lambda b, i, h: (b, h))],
        out_specs=pl.BlockSpec((tq, hd), lambda b, i, h: (b * nq + i, h)),
        compiler_params=pltpu.CompilerParams(
            dimension_semantics=("parallel", "parallel", "parallel"),
            vmem_limit_bytes=_vmem_limit(24 << 20)),
        name="cross_attention",
    )(q, k, v)


def kernel(x, mem, norm_mix_g, w_in, gla_gate_w2, gla_gate_b, gla_norm_g, mlstm_conv_w, mlstm_conv_b, mlstm_igate_b, mlstm_fgate_b, mlstm_norm_g, w_out, norm_cross_g, norm_mem_g, wq_c, wk_c, wv_c, wo_c, norm_ffn_g, w_gate, w_up, w_down, norm_final_g):
    batch, seq, d = x.shape
    mem_tokens = mem.shape[1]
    t = batch * seq
    depth = w_in.shape[0]
    gla_cols = 2 * GLA_HEADS * GLA_DK + 2 * GLA_HEADS * GLA_DV
    ml_cols = 2 * MLSTM_HEADS * MLSTM_DK + 2 * MLSTM_HEADS * MLSTM_DV
    ml_start = gla_cols + GLA_GATE_RANK

    h = x.reshape(t, d)
    mem2 = mem.reshape(batch * mem_tokens, d)
    for l in range(depth):
        w_in16 = w_in[l].astype(BF16)
        w_ml = w_in16[:, ml_start:ml_start + ml_cols]
        w_small = jnp.concatenate(
            [w_in[l][:, gla_cols:ml_start], w_in[l][:, ml_start + ml_cols:]], axis=1)
        w_small = jnp.pad(w_small, ((0, 0), (0, SMALL_COLS - w_small.shape[1]))).astype(BF16)
        w2 = gla_gate_w2[l].reshape(GLA_GATE_RANK, GLA_HEADS, GLA_DK).transpose(1, 0, 2)
        w2 = jnp.pad(w2, ((0, 0), (0, SMALL_COLS - GLA_GATE_RANK), (0, 0)))
        gate_b = gla_gate_b[l].reshape(GLA_HEADS, 1, GLA_DK)
        gla_g = gla_norm_g[l].reshape(GLA_HEADS, 1, GLA_DV)
        ml_g = mlstm_norm_g[l].reshape(MLSTM_HEADS, 1, MLSTM_DV)
        gate_bias = jnp.concatenate([mlstm_igate_b[l], mlstm_fgate_b[l]]).astype(F32)
        conv_b = mlstm_conv_b[l].reshape(1, -1)

        n1 = _rmsnorm(h, norm_mix_g[l], BF16)
        proj_gla = _matmul([n1], w_in16, BF16, n_cols=gla_cols, tm=1024, tn=1024, name="in_proj_gla")
        proj_ml = _matmul([n1], w_ml, BF16, tm=1024, tn=1024, name="in_proj_mlstm")
        small = _matmul([n1], w_small, F32, tm=1024, tn=SMALL_COLS, name="in_proj_small")
        ig_t = small[:, IG_COL:IG_COL + 2 * MLSTM_HEADS].reshape(
            t // CHUNK, CHUNK, 2 * MLSTM_HEADS).transpose(0, 2, 1)
        o_gla, o_ml = _mixer_scan(
            _gla(proj_gla, small, w2, gate_b, gla_g, batch=batch, seq=seq),
            _mlstm(proj_ml, small, ig_t, gate_bias, mlstm_conv_w[l], conv_b, ml_g, batch=batch, seq=seq),
            batch=batch, seq=seq)
        h, hb, ssq = _matmul([o_gla, o_ml], w_out[l].astype(BF16), F32, resid=h,
                             next_gain=norm_cross_g[l], tm=1024, tn=512, name="out_proj")

        q = _norm_proj(hb, ssq, [wq_c[l]], tm=1024, tn=512, name="xattn_q")
        mem_n = _rmsnorm(mem2, norm_mem_g[l], BF16)
        k_mem = _matmul([mem_n], wk_c[l].astype(BF16), BF16, tm=512, tn=1024, name="xattn_k")
        v_mem = _matmul([mem_n], wv_c[l].astype(BF16), BF16, tm=512, tn=1024, name="xattn_v")
        attn = _xattn(q, k_mem, v_mem, batch=batch, seq=seq, tq=1024)
        h, hb, ssq = _matmul([attn], wo_c[l].astype(BF16), F32, resid=h,
                             next_gain=norm_ffn_g[l], tm=1024, tn=512, name="xattn_o")

        hid = _norm_proj(hb, ssq, [w_gate[l], w_up[l]], tm=1024, tn=FFN_TN, name="swiglu_up")
        h = _matmul([hid], w_down[l].astype(BF16), F32, resid=h, tm=512, tn=512, name="ffn_down")

    y = _rmsnorm(h, norm_final_g, x.dtype)
    return y.reshape(batch, seq, d)
```

```python
import functools
import math

import jax
import jax.numpy as jnp
import numpy as np
from jax import lax
from jax.experimental import pallas as pl
from jax.experimental.pallas import tpu as pltpu

F32 = jnp.float32
BF16 = jnp.bfloat16

D_MODEL = 4096
GROUP_WIDTH = D_MODEL // 2
GLA_HEADS = 4
GLA_DV = GROUP_WIDTH // GLA_HEADS
GLA_DK = GLA_DV // 2
GLA_GATE_RANK = 16
GLA_TAU = 16.0
MLSTM_HEADS = 4
MLSTM_DV = GROUP_WIDTH // MLSTM_HEADS
MLSTM_DK = MLSTM_DV // 2
CONV_WIDTH = 4
XATTN_HEADS = 4
XATTN_HEAD_DIM = D_MODEL // XATTN_HEADS
FFN_HIDDEN = 256 * math.ceil(8 * D_MODEL / (3 * 256))
EPS = 1e-6

V7X_VMEM_LIMIT_BYTES = 56 * 1024 * 1024
LANES = 128
SUBLANES = 8

CHUNK = 64
GLA_LEVELS = (32, 16, 8, 4, 2, 1)
LOG2_E = math.log2(math.e)
SEQ_BLOCK = 512
CONV_BLOCK = 256
NORM_ROWS = 256
FFN_TN = 256
XATTN_ROWS = 256
SMALL_COLS = LANES
IG_COL = GLA_GATE_RANK
FG_COL = GLA_GATE_RANK + MLSTM_HEADS


def _vmem_limit(nbytes):
    return int(min(V7X_VMEM_LIMIT_BYTES, max(32 * 1024 * 1024, nbytes * 5 // 4 + (4 << 20))))


def _dot(a, b):
    return jnp.dot(a, b, preferred_element_type=F32)


def _dot_nt(a, b):
    return lax.dot_general(a, b, (((1,), (1,)), ((), ())), preferred_element_type=F32)


def _dot_tn(a, b):
    return lax.dot_general(a, b, (((0,), (0,)), ((), ())), preferred_element_type=F32)


def _split3(x):
    hi = x.astype(BF16)
    r = x - hi.astype(F32)
    mid = r.astype(BF16)
    lo = (r - mid.astype(F32)).astype(BF16)
    return hi, mid, lo


def _log_sigmoid(z):
    return jnp.minimum(z, 0.0) - jnp.log(1.0 + jnp.exp(-jnp.abs(z)))


def _sigmoid(z):
    return 1.0 / (1.0 + jnp.exp(-z))


def _rmsnorm_kernel(x_ref, g_ref, o_ref):
    x = x_ref[...].astype(F32)
    ms = jnp.mean(x * x, axis=-1, keepdims=True)
    o_ref[...] = (x * lax.rsqrt(ms + EPS) * g_ref[...]).astype(o_ref.dtype)


def _rmsnorm(x, g, out_dtype):
    rows, d = x.shape
    tr = min(NORM_ROWS, rows)
    nbytes = 2 * tr * d * (x.dtype.itemsize + jnp.dtype(out_dtype).itemsize) + 3 * tr * d * 4
    return pl.pallas_call(
        _rmsnorm_kernel,
        out_shape=jax.ShapeDtypeStruct((rows, d), out_dtype),
        grid=(rows // tr,),
        in_specs=[pl.BlockSpec((tr, d), lambda i: (i, 0)),
                  pl.BlockSpec((1, d), lambda i: (0, 0))],
        out_specs=pl.BlockSpec((tr, d), lambda i: (i, 0)),
        compiler_params=pltpu.CompilerParams(
            dimension_semantics=("parallel",), vmem_limit_bytes=_vmem_limit(nbytes)),
        name="rmsnorm",
    )(x, g.reshape(1, d).astype(F32))


def _mm_kernel(*refs, n_a, has_resid, has_norm):
    a_refs = refs[:n_a]
    w_ref = refs[n_a]
    pos = n_a + 1
    r_ref = refs[pos] if has_resid else None
    pos += int(has_resid)
    g_ref = refs[pos] if has_norm else None
    pos += int(has_norm)
    o_ref = refs[pos]

    kp = a_refs[0].shape[1]
    acc = _dot(a_refs[0][...], w_ref[0:kp, :])
    for p in range(1, n_a):
        acc = acc + _dot(a_refs[p][...], w_ref[p * kp:(p + 1) * kp, :])
    if has_resid:
        acc = r_ref[...] + acc
    o_ref[...] = acc.astype(o_ref.dtype)

    if has_norm:
        hb_ref, ssq_ref = refs[pos + 1], refs[pos + 2]
        hb_ref[...] = (acc * g_ref[...]).astype(hb_ref.dtype)
        part = jnp.broadcast_to(jnp.sum(acc * acc, axis=1, keepdims=True), ssq_ref.shape)
        j = pl.program_id(1)

        @pl.when(j == 0)
        def _():
            ssq_ref[...] = part

        @pl.when(j > 0)
        def _():
            ssq_ref[...] += part


def _matmul(a_list, w, out_dtype, *, resid=None, next_gain=None, n_cols=None, w_outer=False, tm, tn, name):
    m, kp = a_list[0].shape
    kdim = w.shape[0]
    n = w.shape[1] if n_cols is None else n_cols
    n_a = len(a_list)
    assert m % tm == 0 and n % tn == 0 and kdim == n_a * kp
    assert not (w_outer and next_gain is not None)

    def spec(shape, f):
        return pl.BlockSpec(shape, (lambda j, i: f(i, j)) if w_outer else f)

    in_specs = ([spec((tm, kp), lambda i, j: (i, 0)) for _ in a_list]
                + [spec((kdim, tn), lambda i, j: (0, j))])
    args = list(a_list) + [w]
    out_shape = [jax.ShapeDtypeStruct((m, n), out_dtype)]
    out_specs = [spec((tm, tn), lambda i, j: (i, j))]
    nbytes = 2 * (tm * kdim + kdim * tn) * 2 + 2 * tm * tn * jnp.dtype(out_dtype).itemsize + 2 * tm * tn * 4
    if resid is not None:
        in_specs.append(spec((tm, tn), lambda i, j: (i, j)))
        args.append(resid)
        nbytes += 2 * tm * tn * 4
    if next_gain is not None:
        in_specs.append(spec((1, tn), lambda i, j: (0, j)))
        args.append(next_gain.reshape(1, n).astype(F32))
        out_shape += [jax.ShapeDtypeStruct((m, n), BF16), jax.ShapeDtypeStruct((m, LANES), F32)]
        out_specs += [spec((tm, tn), lambda i, j: (i, j)),
                      spec((tm, LANES), lambda i, j: (i, 0))]
        nbytes += 2 * tm * tn * 2 + 2 * tm * LANES * 4 + tm * tn * 4
    outs = pl.pallas_call(
        functools.partial(_mm_kernel, n_a=n_a, has_resid=resid is not None,
                          has_norm=next_gain is not None),
        out_shape=out_shape,
        grid=(n // tn, m // tm) if w_outer else (m // tm, n // tn),
        in_specs=in_specs,
        out_specs=out_specs,
        compiler_params=pltpu.CompilerParams(
            dimension_semantics=("parallel", "arbitrary"),
            vmem_limit_bytes=_vmem_limit(nbytes)),
        name=name,
    )(*args)
    return outs if next_gain is not None else outs[0]


def _norm_proj_kernel(a_ref, ssq_ref, *refs, n_w):
    w_refs = refs[:n_w]
    o_ref = refs[n_w]
    wb_refs = refs[n_w + 1:]

    @pl.when(pl.program_id(1) == 0)
    def _():
        for w_ref, wb_ref in zip(w_refs, wb_refs):
            wb_ref[...] = w_ref[...].astype(BF16)

    a = a_ref[...]
    r = lax.rsqrt(ssq_ref[:, 0:1] * (1.0 / a.shape[1]) + EPS)
    first = r * _dot(a, wb_refs[0][...])
    if n_w == 1:
        o_ref[...] = first.astype(o_ref.dtype)
    else:
        up = r * _dot(a, wb_refs[1][...])
        o_ref[...] = (first * _sigmoid(first) * up).astype(o_ref.dtype)


def _norm_proj(a, ssq, w_list, *, tm, tn, name):
    m, kdim = a.shape
    n = w_list[0].shape[1]
    n_w = len(w_list)
    assert m % tm == 0 and n % tn == 0
    nbytes = (2 * tm * kdim * 2 + n_w * kdim * tn * (2 * 4 + 2) + 2 * tm * tn * 2
              + 2 * tm * LANES * 4 + 3 * tm * tn * 4)
    return pl.pallas_call(
        functools.partial(_norm_proj_kernel, n_w=n_w),
        out_shape=jax.ShapeDtypeStruct((m, n), BF16),
        grid=(n // tn, m // tm),
        in_specs=([pl.BlockSpec((tm, kdim), lambda j, i: (i, 0)),
                   pl.BlockSpec((tm, LANES), lambda j, i: (i, 0))]
                  + [pl.BlockSpec((kdim, tn), lambda j, i: (0, j)) for _ in w_list]),
        out_specs=pl.BlockSpec((tm, tn), lambda j, i: (i, j)),
        scratch_shapes=[pltpu.VMEM((kdim, tn), BF16) for _ in w_list],
        compiler_params=pltpu.CompilerParams(
            dimension_semantics=("parallel", "arbitrary"),
            vmem_limit_bytes=_vmem_limit(nbytes)),
        name=name,
    )(a, ssq, *w_list)


def _gla_tables():
    c = CHUNK
    row = np.arange(c)[:, None]
    u = np.arange(c)[None, :]
    sel = [u <= row, u > row]
    masks = [row == u]
    for bsz in GLA_LEVELS:
        r = (row // (2 * bsz)) * (2 * bsz) + bsz - 1
        sel.append(np.where(row <= r, (u > row) & (u <= r), (u > r) & (u <= row)))
        same_pair = (row // (2 * bsz)) == (u // (2 * bsz))
        masks.append(same_pair & ((row % (2 * bsz)) >= bsz) & ((u % (2 * bsz)) < bsz))
    sel = np.concatenate(sel, axis=0).astype(np.float32)
    return (jnp.asarray(np.concatenate([sel] * 3, axis=1), BF16),
            jnp.asarray(np.stack(masks).astype(np.float32)))


def _gla_stream(q_ref, k_ref, v_ref, g_ref, sm_ref, w2_ref, gb_ref, ng_ref, sel_ref, mask_ref,
                o_ref, st_ref):
    c_len = CHUNK
    norm_g = ng_ref[...]
    z = _dot(sm_ref[...].astype(BF16), w2_ref[...].astype(BF16)) + gb_ref[...]
    la_hi, la_mid, la_lo = _split3(_log_sigmoid(z) * (LOG2_E / GLA_TAU))
    row_k = lax.broadcasted_iota(jnp.int32, (c_len, GLA_DK), 0)
    state = [st_ref[...]]
    tmp = [dict() for _ in range(SEQ_BLOCK // c_len)]

    def rows_of(c):
        return slice(c * c_len, (c + 1) * c_len)

    def decay_table(c):
        rows = rows_of(c)
        la3 = jnp.concatenate([la_hi[rows, :], la_mid[rows, :], la_lo[rows, :]], axis=0)
        tmp[c]["e"] = jnp.exp2(_dot(sel_ref[...], la3))

    def scores(c):
        rows = rows_of(c)
        e = tmp[c].pop("e")
        q = q_ref[rows, :].astype(F32) * (GLA_DK ** -0.5)
        k = k_ref[rows, :].astype(F32)
        k_dec = (k * e[c_len:2 * c_len, :]).astype(BF16)
        attn = _dot_nt(q.astype(BF16), k.astype(BF16)) * mask_ref[0]
        for lv, bsz in enumerate(GLA_LEVELS):
            in_right = (row_k & bsz) != 0
            x = (jnp.where(in_right, q, k) * e[(2 + lv) * c_len:(3 + lv) * c_len, :]).astype(BF16)
            attn = attn + _dot_nt(x, x) * mask_ref[1 + lv]
        tmp[c].update(q_dec=(q * e[0:c_len, :]).astype(BF16), k_dec=k_dec,
                      decay=e[c_len - 1:c_len, :],
                      intra=_dot(attn.astype(BF16), v_ref[rows, :]))

    def recurrence(c):
        tmp[c]["st_in"] = state[0].astype(BF16)
        state[0] = state[0] * tmp[c].pop("decay") + _dot_tn(v_ref[rows_of(c), :], tmp[c].pop("k_dec"))

    def output(c):
        rows = rows_of(c)
        out = tmp[c].pop("intra") + _dot_nt(tmp[c].pop("q_dec"), tmp[c].pop("st_in"))
        ms = jnp.mean(out * out, axis=-1, keepdims=True)
        y = out * lax.rsqrt(ms + EPS) * norm_g
        gg = g_ref[rows, :].astype(F32)
        o_ref[rows, :] = (y * (gg * _sigmoid(gg))).astype(o_ref.dtype)

    def finish():
        st_ref[...] = state[0]

    return [decay_table, scores, recurrence, output], finish


def _gla(proj, small, w2, gate_b, norm_g, *, batch, seq):
    decay_sel, pair_mask = _gla_tables()
    t = proj.shape[0]
    lb = SEQ_BLOCK
    nsb = seq // lb
    h_ = GLA_HEADS
    qb = GLA_DK
    vb = GLA_DV

    def rows(b, h, i):
        return b * nsb + i

    in_specs = [
        pl.BlockSpec((lb, qb), lambda b, h, i: (rows(b, h, i), h)),
        pl.BlockSpec((lb, qb), lambda b, h, i: (rows(b, h, i), h_ + h)),
        pl.BlockSpec((lb, vb), lambda b, h, i: (rows(b, h, i), (2 * h_ * qb) // vb + h)),
        pl.BlockSpec((lb, vb), lambda b, h, i: (rows(b, h, i), (2 * h_ * qb) // vb + h_ + h)),
        pl.BlockSpec((lb, SMALL_COLS), lambda b, h, i: (rows(b, h, i), 0)),
        pl.BlockSpec((None, SMALL_COLS, qb), lambda b, h, i: (h, 0, 0)),
        pl.BlockSpec((None, 1, qb), lambda b, h, i: (h, 0, 0)),
        pl.BlockSpec((None, 1, vb), lambda b, h, i: (h, 0, 0)),
        pl.BlockSpec(decay_sel.shape, lambda b, h, i: (0, 0)),
        pl.BlockSpec(pair_mask.shape, lambda b, h, i: (0, 0, 0)),
    ]
    return dict(
        in_specs=in_specs,
        args=[proj, proj, proj, proj, small, w2, gate_b, norm_g, decay_sel, pair_mask],
        out_shape=jax.ShapeDtypeStruct((t, h_ * vb), BF16),
        out_spec=pl.BlockSpec((lb, vb), lambda b, h, i: (rows(b, h, i), h)),
        scratch=[pltpu.VMEM((vb, qb), F32)])


def _mlstm_tables():
    lb, c = CONV_BLOCK, CHUNK
    t = np.arange(lb)[:, None]
    u = np.arange(lb)[None, :]
    shift = np.concatenate([(u == t - d) for d in range(1, CONV_WIDTH)], axis=0).astype(np.float32)
    tc = np.arange(c)[:, None]
    uc = np.arange(c)[None, :]
    cum = np.concatenate([uc <= tc, uc > tc], axis=0).astype(np.float32)
    col = np.concatenate([tc > uc, np.ones((c, c), bool)], axis=1).astype(np.float32)
    return (jnp.asarray(shift, BF16), jnp.asarray(np.concatenate([cum] * 3, axis=1), BF16),
            jnp.asarray(col))


def _mlstm_stream(gbias_ref, q_ref, k_ref, v_ref, og_ref, sm_ref, igt_ref, cwq_ref, cwk_ref,
                  cbq_ref, cbk_ref, ng_ref, shift_ref, cum_ref, col_ref, o_ref,
                  ct_ref, n_ref, m_ref, hist_ref):
    c_len = CHUNK
    lb = SEQ_BLOCK
    taps = CONV_WIDTH
    h = pl.program_id(1)

    cb = CONV_BLOCK
    w = jnp.concatenate([cwq_ref[...], cwk_ref[...]], axis=1)
    bias = jnp.concatenate([cbq_ref[...], cbk_ref[...]], axis=1)
    tail = hist_ref[...]
    acts = []
    for s in range(lb // cb):
        x16 = jnp.concatenate([q_ref[s * cb:(s + 1) * cb, :], k_ref[s * cb:(s + 1) * cb, :]], axis=1)
        x = x16.astype(F32)
        shifted = _dot(shift_ref[...], x16)
        head = jnp.concatenate([tail, x[0:SUBLANES, :]], axis=0)
        y = bias + w[taps - 1:taps, :] * x
        for d in range(1, taps):
            sh = jnp.concatenate([head[SUBLANES - d:2 * SUBLANES - d, :],
                                  shifted[(d - 1) * cb + SUBLANES:d * cb, :]], axis=0)
            y = y + w[taps - 1 - d:taps - d, :] * sh
        tail = x[cb - SUBLANES:cb, :]
        acts.append(y * _sigmoid(y))
    hist_ref[...] = tail
    act = jnp.concatenate(acts, axis=0)
    q_all = act[:, 0:MLSTM_DK]
    k_all = act[:, MLSTM_DK:2 * MLSTM_DK] * (MLSTM_DK ** -0.5)

    ib = gbias_ref[h]
    fb = gbias_ref[MLSTM_HEADS + h]
    sm = sm_ref[...]
    lane_s = lax.broadcasted_iota(jnp.int32, (lb, SMALL_COLS), 1)
    i_col_all = jnp.sum(jnp.where(lane_s == IG_COL + h, sm, 0.0), axis=1, keepdims=True) + ib
    f_col_all = jnp.sum(jnp.where(lane_s == FG_COL + h, sm, 0.0), axis=1, keepdims=True) + fb
    lf_col_all = _log_sigmoid(f_col_all)

    norm_g = ng_ref[...]
    row_c = lax.broadcasted_iota(jnp.int32, (c_len, c_len), 0)
    col_c = lax.broadcasted_iota(jnp.int32, (c_len, c_len), 1)
    causal = row_c >= col_c

    state = [ct_ref[...], n_ref[...], m_ref[...]]
    tmp = [dict() for _ in range(lb // c_len)]

    def rows_of(c):
        return slice(c * c_len, (c + 1) * c_len)

    def gate_sums(c):
        rows = rows_of(c)
        g_hi, g_mid, g_lo = _split3(lf_col_all[rows, :] * col_ref[...])
        sums = _dot(cum_ref[...], jnp.concatenate([g_hi, g_mid, g_lo], axis=0))
        i_row = igt_ref[c, pl.ds(h, 1), :] + ib
        log_d = jnp.where(causal, sums[0:c_len, 0:c_len] + i_row, -jnp.inf)
        bb_col = sums[0:c_len, c_len:c_len + 1]
        log_w = sums[c_len:2 * c_len, c_len:c_len + 1] + i_col_all[rows, :]
        tmp[c].update(log_d=log_d, bb_col=bb_col, log_w=log_w,
                      d_max=jnp.max(log_d, axis=1, keepdims=True),
                      w_max=jnp.max(log_w, axis=0, keepdims=True),
                      qk=_dot_nt(q_all[rows, :].astype(BF16), k_all[rows, :].astype(BF16)))

    def stabiliser(c):
        d = tmp[c]
        m_prev = state[2]
        bb_col = d.pop("bb_col")
        b_last = bb_col[c_len - 1:c_len, :]
        inter = bb_col + m_prev
        m = jnp.maximum(inter, d.pop("d_max"))
        m_new = jnp.maximum(b_last + m_prev, d.pop("w_max"))
        d.update(m=m, g_inter=jnp.exp(inter - m), w_col=jnp.exp(d.pop("log_w") - m_new),
                 decay=jnp.exp(b_last + m_prev - m_new))
        state[2] = m_new

    def scores(c):
        rows = rows_of(c)
        d = tmp[c]
        s = d.pop("qk") * jnp.exp(d.pop("log_d") - d["m"])
        kw = k_all[rows, :] * d.pop("w_col")
        d.update(s_sum=jnp.sum(s, axis=1, keepdims=True),
                 intra=_dot(s.astype(BF16), v_ref[rows, :]),
                 kw=kw.astype(BF16),
                 k_sum=jnp.sum(kw, axis=0, keepdims=True))

    def recurrence(c):
        d = tmp[c]
        decay = d.pop("decay")
        d.update(ct_in=state[0].astype(BF16), n_in=state[1])
        state[0] = decay * state[0] + _dot_tn(v_ref[rows_of(c), :], d.pop("kw"))
        state[1] = decay * state[1] + d.pop("k_sum")

    def output(c):
        rows = rows_of(c)
        d = tmp[c]
        q = q_all[rows, :]
        g_inter = d.pop("g_inter")
        num = d.pop("intra") + g_inter * _dot_nt(q.astype(BF16), d.pop("ct_in"))
        den = d.pop("s_sum") + g_inter * jnp.sum(q * d.pop("n_in"), axis=1, keepdims=True)
        hid = num / jnp.maximum(jnp.abs(den), jnp.exp(-d.pop("m")))
        ms = jnp.mean(hid * hid, axis=-1, keepdims=True)
        y_out = hid * lax.rsqrt(ms + EPS) * norm_g
        og = og_ref[rows, :].astype(F32)
        o_ref[rows, :] = (_sigmoid(og) * y_out).astype(o_ref.dtype)

    def finish():
        ct_ref[...], n_ref[...], m_ref[...] = state

    return [gate_sums, stabiliser, scores, recurrence, output], finish


def _mlstm(proj, small, ig_t, gate_bias, conv_w, conv_b, norm_g, *, batch, seq):
    shift, cum_sel, col_sel = _mlstm_tables()
    t = proj.shape[0]
    lb = SEQ_BLOCK
    nsb = seq // lb
    h_ = MLSTM_HEADS
    qb = MLSTM_DK
    vb = MLSTM_DV
    q0 = 0
    v0 = (2 * h_ * qb) // vb

    def rows(b, h, i):
        return b * nsb + i

    in_specs = [
        pl.BlockSpec(memory_space=pltpu.SMEM),
        pl.BlockSpec((lb, qb), lambda b, h, i: (rows(b, h, i), q0 + h)),
        pl.BlockSpec((lb, qb), lambda b, h, i: (rows(b, h, i), q0 + h_ + h)),
        pl.BlockSpec((lb, vb), lambda b, h, i: (rows(b, h, i), v0 + h)),
        pl.BlockSpec((lb, vb), lambda b, h, i: (rows(b, h, i), v0 + h_ + h)),
        pl.BlockSpec((lb, SMALL_COLS), lambda b, h, i: (rows(b, h, i), 0)),
        pl.BlockSpec((lb // CHUNK, 2 * h_, CHUNK), lambda b, h, i: (rows(b, h, i), 0, 0)),
        pl.BlockSpec((CONV_WIDTH, qb), lambda b, h, i: (0, h)),
        pl.BlockSpec((CONV_WIDTH, qb), lambda b, h, i: (0, h_ + h)),
        pl.BlockSpec((1, qb), lambda b, h, i: (0, h)),
        pl.BlockSpec((1, qb), lambda b, h, i: (0, h_ + h)),
        pl.BlockSpec((None, 1, vb), lambda b, h, i: (h, 0, 0)),
        pl.BlockSpec(shift.shape, lambda b, h, i: (0, 0)),
        pl.BlockSpec(cum_sel.shape, lambda b, h, i: (0, 0)),
        pl.BlockSpec(col_sel.shape, lambda b, h, i: (0, 0)),
    ]
    scratch = [
        pltpu.VMEM((vb, qb), F32),
        pltpu.VMEM((1, qb), F32),
        pltpu.VMEM((1, 1), F32),
        pltpu.VMEM((SUBLANES, 2 * qb), F32),
    ]
    return dict(
        in_specs=in_specs,
        args=[gate_bias, proj, proj, proj, proj, small, ig_t, conv_w, conv_w, conv_b, conv_b, norm_g,
              shift, cum_sel, col_sel],
        out_shape=jax.ShapeDtypeStruct((t, h_ * vb), BF16),
        out_spec=pl.BlockSpec((lb, vb), lambda b, h, i: (rows(b, h, i), h)),
        scratch=scratch)


def _mixer_scan_kernel(*refs, n_gla_in, n_ml_in, n_gla_scratch):
    gla_in = refs[:n_gla_in]
    ml_in = refs[n_gla_in:n_gla_in + n_ml_in]
    o_gla, o_ml = refs[n_gla_in + n_ml_in:n_gla_in + n_ml_in + 2]
    scratch = refs[n_gla_in + n_ml_in + 2:]

    @pl.when(pl.program_id(2) == 0)
    def _():
        for ref in scratch:
            ref[...] = jnp.zeros_like(ref)

    gla_stages, gla_finish = _gla_stream(*gla_in, o_gla, *scratch[:n_gla_scratch])
    ml_stages, ml_finish = _mlstm_stream(*ml_in, o_ml, *scratch[n_gla_scratch:])
    n_chunks = SEQ_BLOCK // CHUNK
    lead = len(ml_stages) - len(gla_stages)
    for t in range(n_chunks + len(ml_stages) - 1):
        for k in reversed(range(len(ml_stages))):
            c = t - k
            if 0 <= c < n_chunks:
                ml_stages[k](c)
                if k >= lead:
                    gla_stages[k - lead](c)
    gla_finish()
    ml_finish()


def _mixer_scan(gla, ml, *, batch, seq):
    assert GLA_HEADS == MLSTM_HEADS
    outs = pl.pallas_call(
        functools.partial(_mixer_scan_kernel, n_gla_in=len(gla["args"]), n_ml_in=len(ml["args"]),
                          n_gla_scratch=len(gla["scratch"])),
        out_shape=[gla["out_shape"], ml["out_shape"]],
        grid=(batch, GLA_HEADS, seq // SEQ_BLOCK),
        in_specs=gla["in_specs"] + ml["in_specs"],
        out_specs=[gla["out_spec"], ml["out_spec"]],
        scratch_shapes=gla["scratch"] + ml["scratch"],
        compiler_params=pltpu.CompilerParams(
            dimension_semantics=("parallel", "parallel", "arbitrary"),
            vmem_limit_bytes=_vmem_limit(24 << 20)),
        name="mixer_scan",
    )(*gla["args"], *ml["args"])
    return outs[0], outs[1]


def _xattn_kernel(a_ref, ssq_ref, wq_ref, k_ref, v_ref, o_ref):
    a = a_ref[...]
    r = lax.rsqrt(ssq_ref[:, 0:1] * (1.0 / a.shape[1]) + EPS)
    q = (r * _dot(a, wq_ref[...])).astype(BF16)
    groups = [slice(g * XATTN_ROWS, (g + 1) * XATTN_ROWS) for g in range(a.shape[0] // XATTN_ROWS)]
    scores = [_dot_nt(q[rows, :], k_ref[...]) * (XATTN_HEAD_DIM ** -0.5) for rows in groups]
    probs = []
    for s in scores:
        p = jnp.exp(s - jnp.max(s, axis=-1, keepdims=True))
        probs.append((p / jnp.sum(p, axis=-1, keepdims=True)).astype(BF16))
    for rows, p in zip(groups, probs):
        o_ref[rows, :] = _dot(p, v_ref[...]).astype(o_ref.dtype)


def _xattn(a, ssq, wq, k, v, *, batch, seq, tq):
    t, d = a.shape
    mem = k.shape[0] // batch
    hd = XATTN_HEAD_DIM
    nq = seq // tq
    nbytes = 2 * (tq * d + d * hd + 2 * mem * hd + tq * hd) * 2 + 2 * tq * LANES * 4 + 3 * tq * hd * 4
    return pl.pallas_call(
        _xattn_kernel,
        out_shape=jax.ShapeDtypeStruct((t, d), BF16),
        grid=(batch * nq, XATTN_HEADS),
        in_specs=[pl.BlockSpec((tq, d), lambda i, h: (i, 0)),
                  pl.BlockSpec((tq, LANES), lambda i, h: (i, 0)),
                  pl.BlockSpec((d, hd), lambda i, h: (0, h)),
                  pl.BlockSpec((mem, hd), lambda i, h: (i // nq, h)),
                  pl.BlockSpec((mem, hd), lambda i, h: (i // nq, h))],
        out_specs=pl.BlockSpec((tq, hd), lambda i, h: (i, h)),
        compiler_params=pltpu.CompilerParams(
            dimension_semantics=("parallel", "parallel"),
            vmem_limit_bytes=_vmem_limit(nbytes)),
        name="cross_attention",
    )(a, ssq, wq, k, v)


def kernel(x, mem, norm_mix_g, w_in, gla_gate_w2, gla_gate_b, gla_norm_g, mlstm_conv_w, mlstm_conv_b, mlstm_igate_b, mlstm_fgate_b, mlstm_norm_g, w_out, norm_cross_g, norm_mem_g, wq_c, wk_c, wv_c, wo_c, norm_ffn_g, w_gate, w_up, w_down, norm_final_g):
    batch, seq, d = x.shape
    mem_tokens = mem.shape[1]
    t = batch * seq
    depth = w_in.shape[0]
    gla_cols = 2 * GLA_HEADS * GLA_DK + 2 * GLA_HEADS * GLA_DV
    ml_cols = 2 * MLSTM_HEADS * MLSTM_DK + 2 * MLSTM_HEADS * MLSTM_DV
    ml_start = gla_cols + GLA_GATE_RANK

    h = x.reshape(t, d)
    mem2 = mem.reshape(batch * mem_tokens, d)
    for l in range(depth):
        w_in16 = w_in[l].astype(BF16)
        w_ml = w_in16[:, ml_start:ml_start + ml_cols]
        w_small = jnp.concatenate(
            [w_in[l][:, gla_cols:ml_start], w_in[l][:, ml_start + ml_cols:]], axis=1)
        w_small = jnp.pad(w_small, ((0, 0), (0, SMALL_COLS - w_small.shape[1]))).astype(BF16)
        w2 = gla_gate_w2[l].reshape(GLA_GATE_RANK, GLA_HEADS, GLA_DK).transpose(1, 0, 2)
        w2 = jnp.pad(w2, ((0, 0), (0, SMALL_COLS - GLA_GATE_RANK), (0, 0)))
        gate_b = gla_gate_b[l].reshape(GLA_HEADS, 1, GLA_DK)
        gla_g = gla_norm_g[l].reshape(GLA_HEADS, 1, GLA_DV)
        ml_g = mlstm_norm_g[l].reshape(MLSTM_HEADS, 1, MLSTM_DV)
        gate_bias = jnp.concatenate([mlstm_igate_b[l], mlstm_fgate_b[l]]).astype(F32)
        conv_b = mlstm_conv_b[l].reshape(1, -1)

        n1 = _rmsnorm(h, norm_mix_g[l], BF16)
        proj_gla = _matmul([n1], w_in16, BF16, n_cols=gla_cols, tm=1024, tn=1024, name="in_proj_gla")
        proj_ml = _matmul([n1], w_ml, BF16, tm=1024, tn=1024, name="in_proj_mlstm")
        small = _matmul([n1], w_small, F32, tm=1024, tn=SMALL_COLS, name="in_proj_small")
        ig_t = small[:, IG_COL:IG_COL + 2 * MLSTM_HEADS].reshape(
            t // CHUNK, CHUNK, 2 * MLSTM_HEADS).transpose(0, 2, 1)
        o_gla, o_ml = _mixer_scan(
            _gla(proj_gla, small, w2, gate_b, gla_g, batch=batch, seq=seq),
            _mlstm(proj_ml, small, ig_t, gate_bias, mlstm_conv_w[l], conv_b, ml_g, batch=batch, seq=seq),
            batch=batch, seq=seq)
        h, hb, ssq = _matmul([o_gla, o_ml], w_out[l].astype(BF16), F32, resid=h,
                             next_gain=norm_cross_g[l], tm=1024, tn=512, name="out_proj")

        mem_n = _rmsnorm(mem2, norm_mem_g[l], BF16)
        k_mem = _matmul([mem_n], wk_c[l].astype(BF16), BF16, tm=512, tn=1024, name="xattn_k")
        v_mem = _matmul([mem_n], wv_c[l].astype(BF16), BF16, tm=512, tn=1024, name="xattn_v")
        attn = _xattn(hb, ssq, wq_c[l].astype(BF16), k_mem, v_mem, batch=batch, seq=seq, tq=1024)
        h, hb, ssq = _matmul([attn], wo_c[l].astype(BF16), F32, resid=h,
                             next_gain=norm_ffn_g[l], tm=1024, tn=512, name="xattn_o")

        hid = _norm_proj(hb, ssq, [w_gate[l], w_up[l]], tm=1024, tn=FFN_TN, name="swiglu_up")
        h = _matmul([hid], w_down[l].astype(BF16), F32, resid=h, w_outer=True, tm=512, tn=512,
                    name="ffn_down")

    y = _rmsnorm(h, norm_final_g, x.dtype)
    return y.reshape(batch, seq, d)
```

```python
import functools
import math

import jax
import jax.numpy as jnp
import numpy as np
from jax import lax
from jax.experimental import pallas as pl
from jax.experimental.pallas import tpu as pltpu

F32 = jnp.float32
BF16 = jnp.bfloat16

D_MODEL = 4096
GROUP_WIDTH = D_MODEL // 2
GLA_HEADS = 4
GLA_DV = GROUP_WIDTH // GLA_HEADS
GLA_DK = GLA_DV // 2
GLA_GATE_RANK = 16
GLA_TAU = 16.0
MLSTM_HEADS = 4
MLSTM_DV = GROUP_WIDTH // MLSTM_HEADS
MLSTM_DK = MLSTM_DV // 2
CONV_WIDTH = 4
XATTN_HEADS = 4
XATTN_HEAD_DIM = D_MODEL // XATTN_HEADS
FFN_HIDDEN = 256 * math.ceil(8 * D_MODEL / (3 * 256))
EPS = 1e-6

V7X_VMEM_LIMIT_BYTES = 56 * 1024 * 1024
LANES = 128
SUBLANES = 8

CHUNK = 64
GLA_LEVELS = (32, 16, 8, 4, 2, 1)
LOG2_E = math.log2(math.e)
SEQ_BLOCK = 512
CONV_BLOCK = 256
NORM_ROWS = 256
FFN_TN = 256
XATTN_ROWS = 256
SWIGLU_PASS_ROWS = 1024
SMALL_COLS = LANES
IG_COL = GLA_GATE_RANK
FG_COL = GLA_GATE_RANK + MLSTM_HEADS


def _vmem_limit(nbytes):
    return int(min(V7X_VMEM_LIMIT_BYTES, max(32 * 1024 * 1024, nbytes * 5 // 4 + (4 << 20))))


def _dot(a, b):
    return jnp.dot(a, b, preferred_element_type=F32)


def _dot_nt(a, b):
    return lax.dot_general(a, b, (((1,), (1,)), ((), ())), preferred_element_type=F32)


def _dot_tn(a, b):
    return lax.dot_general(a, b, (((0,), (0,)), ((), ())), preferred_element_type=F32)


def _split3(x):
    hi = x.astype(BF16)
    r = x - hi.astype(F32)
    mid = r.astype(BF16)
    lo = (r - mid.astype(F32)).astype(BF16)
    return hi, mid, lo


def _log_sigmoid(z):
    return jnp.minimum(z, 0.0) - jnp.log(1.0 + jnp.exp(-jnp.abs(z)))


def _sigmoid(z):
    return 1.0 / (1.0 + jnp.exp(-z))


def _rmsnorm_kernel(x_ref, g_ref, o_ref):
    x = x_ref[...].astype(F32)
    ms = jnp.mean(x * x, axis=-1, keepdims=True)
    o_ref[...] = (x * lax.rsqrt(ms + EPS) * g_ref[...]).astype(o_ref.dtype)


def _rmsnorm(x, g, out_dtype):
    rows, d = x.shape
    tr = min(NORM_ROWS, rows)
    nbytes = 2 * tr * d * (x.dtype.itemsize + jnp.dtype(out_dtype).itemsize) + 3 * tr * d * 4
    return pl.pallas_call(
        _rmsnorm_kernel,
        out_shape=jax.ShapeDtypeStruct((rows, d), out_dtype),
        grid=(rows // tr,),
        in_specs=[pl.BlockSpec((tr, d), lambda i: (i, 0)),
                  pl.BlockSpec((1, d), lambda i: (0, 0))],
        out_specs=pl.BlockSpec((tr, d), lambda i: (i, 0)),
        compiler_params=pltpu.CompilerParams(
            dimension_semantics=("parallel",), vmem_limit_bytes=_vmem_limit(nbytes)),
        name="rmsnorm",
    )(x, g.reshape(1, d).astype(F32))


def _mm_kernel(*refs, n_a, has_resid, has_norm):
    a_refs = refs[:n_a]
    w_ref = refs[n_a]
    pos = n_a + 1
    r_ref = refs[pos] if has_resid else None
    pos += int(has_resid)
    g_ref = refs[pos] if has_norm else None
    pos += int(has_norm)
    o_ref = refs[pos]

    kp = a_refs[0].shape[1]
    acc = _dot(a_refs[0][...], w_ref[0:kp, :])
    for p in range(1, n_a):
        acc = acc + _dot(a_refs[p][...], w_ref[p * kp:(p + 1) * kp, :])
    if has_resid:
        acc = r_ref[...] + acc
    o_ref[...] = acc.astype(o_ref.dtype)

    if has_norm:
        hb_ref, ssq_ref = refs[pos + 1], refs[pos + 2]
        hb_ref[...] = (acc * g_ref[...]).astype(hb_ref.dtype)
        part = jnp.broadcast_to(jnp.sum(acc * acc, axis=1, keepdims=True), ssq_ref.shape)
        j = pl.program_id(1)

        @pl.when(j == 0)
        def _():
            ssq_ref[...] = part

        @pl.when(j > 0)
        def _():
            ssq_ref[...] += part


def _matmul(a_list, w, out_dtype, *, resid=None, next_gain=None, n_cols=None, w_outer=False, tm, tn, name):
    m, kp = a_list[0].shape
    kdim = w.shape[0]
    n = w.shape[1] if n_cols is None else n_cols
    n_a = len(a_list)
    assert m % tm == 0 and n % tn == 0 and kdim == n_a * kp
    assert not (w_outer and next_gain is not None)

    def spec(shape, f):
        return pl.BlockSpec(shape, (lambda j, i: f(i, j)) if w_outer else f)

    in_specs = ([spec((tm, kp), lambda i, j: (i, 0)) for _ in a_list]
                + [spec((kdim, tn), lambda i, j: (0, j))])
    args = list(a_list) + [w]
    out_shape = [jax.ShapeDtypeStruct((m, n), out_dtype)]
    out_specs = [spec((tm, tn), lambda i, j: (i, j))]
    nbytes = 2 * (tm * kdim + kdim * tn) * 2 + 2 * tm * tn * jnp.dtype(out_dtype).itemsize + 2 * tm * tn * 4
    if resid is not None:
        in_specs.append(spec((tm, tn), lambda i, j: (i, j)))
        args.append(resid)
        nbytes += 2 * tm * tn * 4
    if next_gain is not None:
        in_specs.append(spec((1, tn), lambda i, j: (0, j)))
        args.append(next_gain.reshape(1, n).astype(F32))
        out_shape += [jax.ShapeDtypeStruct((m, n), BF16), jax.ShapeDtypeStruct((m, LANES), F32)]
        out_specs += [spec((tm, tn), lambda i, j: (i, j)),
                      spec((tm, LANES), lambda i, j: (i, 0))]
        nbytes += 2 * tm * tn * 2 + 2 * tm * LANES * 4 + tm * tn * 4
    outs = pl.pallas_call(
        functools.partial(_mm_kernel, n_a=n_a, has_resid=resid is not None,
                          has_norm=next_gain is not None),
        out_shape=out_shape,
        grid=(n // tn, m // tm) if w_outer else (m // tm, n // tn),
        in_specs=in_specs,
        out_specs=out_specs,
        compiler_params=pltpu.CompilerParams(
            dimension_semantics=("parallel", "arbitrary"),
            vmem_limit_bytes=_vmem_limit(nbytes)),
        name=name,
    )(*args)
    return outs if next_gain is not None else outs[0]


def _swiglu_up_kernel(a_ref, ssq_ref, wg_ref, wu_ref, o_ref):
    wg = wg_ref[...].astype(BF16)
    wu = wu_ref[...].astype(BF16)
    kdim = a_ref.shape[1]
    for p in range(a_ref.shape[0] // SWIGLU_PASS_ROWS):
        rows = slice(p * SWIGLU_PASS_ROWS, (p + 1) * SWIGLU_PASS_ROWS)
        a = a_ref[rows, :]
        r = lax.rsqrt(ssq_ref[rows, 0:1] * (1.0 / kdim) + EPS)
        gate = r * _dot(a, wg)
        up = r * _dot(a, wu)
        o_ref[rows, :] = (gate * _sigmoid(gate) * up).astype(o_ref.dtype)


def _swiglu_up(a, ssq, wg, wu, *, tm, tn):
    m, kdim = a.shape
    n = wg.shape[1]
    assert m % tm == 0 and n % tn == 0 and tm % SWIGLU_PASS_ROWS == 0
    nbytes = (tm * kdim * 2 + 2 * kdim * tn * (2 * 4 + 2) + 2 * tm * tn * 2 + 2 * tm * LANES * 4
              + 3 * SWIGLU_PASS_ROWS * tn * 4)
    return pl.pallas_call(
        _swiglu_up_kernel,
        out_shape=jax.ShapeDtypeStruct((m, n), BF16),
        grid=(m // tm, n // tn),
        in_specs=[pl.BlockSpec((tm, kdim), lambda i, j: (i, 0), pipeline_mode=pl.Buffered(1)),
                  pl.BlockSpec((tm, LANES), lambda i, j: (i, 0)),
                  pl.BlockSpec((kdim, tn), lambda i, j: (0, j)),
                  pl.BlockSpec((kdim, tn), lambda i, j: (0, j))],
        out_specs=pl.BlockSpec((tm, tn), lambda i, j: (i, j)),
        compiler_params=pltpu.CompilerParams(
            dimension_semantics=("parallel", "arbitrary"),
            vmem_limit_bytes=_vmem_limit(nbytes)),
        name="swiglu_up",
    )(a, ssq, wg, wu)


def _gla_tables():
    c = CHUNK
    row = np.arange(c)[:, None]
    u = np.arange(c)[None, :]
    sel = [u <= row, u > row]
    masks = [row == u]
    for bsz in GLA_LEVELS:
        r = (row // (2 * bsz)) * (2 * bsz) + bsz - 1
        sel.append(np.where(row <= r, (u > row) & (u <= r), (u > r) & (u <= row)))
        same_pair = (row // (2 * bsz)) == (u // (2 * bsz))
        masks.append(same_pair & ((row % (2 * bsz)) >= bsz) & ((u % (2 * bsz)) < bsz))
    sel = np.concatenate(sel, axis=0).astype(np.float32)
    return (jnp.asarray(np.concatenate([sel] * 3, axis=1), BF16),
            jnp.asarray(np.stack(masks).astype(np.float32)))


def _gla_stream(q_ref, k_ref, v_ref, g_ref, sm_ref, w2_ref, gb_ref, ng_ref, sel_ref, mask_ref,
                o_ref, st_ref):
    c_len = CHUNK
    norm_g = ng_ref[...]
    z = _dot(sm_ref[...].astype(BF16), w2_ref[...].astype(BF16)) + gb_ref[...]
    la_hi, la_mid, la_lo = _split3(_log_sigmoid(z) * (LOG2_E / GLA_TAU))
    row_k = lax.broadcasted_iota(jnp.int32, (c_len, GLA_DK), 0)
    state = [st_ref[...]]
    tmp = [dict() for _ in range(SEQ_BLOCK // c_len)]

    def rows_of(c):
        return slice(c * c_len, (c + 1) * c_len)

    def decay_table(c):
        rows = rows_of(c)
        la3 = jnp.concatenate([la_hi[rows, :], la_mid[rows, :], la_lo[rows, :]], axis=0)
        tmp[c]["e"] = jnp.exp2(_dot(sel_ref[...], la3))

    def scores(c):
        rows = rows_of(c)
        e = tmp[c].pop("e")
        q = q_ref[rows, :].astype(F32) * (GLA_DK ** -0.5)
        k = k_ref[rows, :].astype(F32)
        k_dec = (k * e[c_len:2 * c_len, :]).astype(BF16)
        attn = _dot_nt(q.astype(BF16), k.astype(BF16)) * mask_ref[0]
        for lv, bsz in enumerate(GLA_LEVELS):
            in_right = (row_k & bsz) != 0
            x = (jnp.where(in_right, q, k) * e[(2 + lv) * c_len:(3 + lv) * c_len, :]).astype(BF16)
            attn = attn + _dot_nt(x, x) * mask_ref[1 + lv]
        tmp[c].update(q_dec=(q * e[0:c_len, :]).astype(BF16), k_dec=k_dec,
                      decay=e[c_len - 1:c_len, :],
                      intra=_dot(attn.astype(BF16), v_ref[rows, :]))

    def recurrence(c):
        tmp[c]["st_in"] = state[0].astype(BF16)
        state[0] = state[0] * tmp[c].pop("decay") + _dot_tn(v_ref[rows_of(c), :], tmp[c].pop("k_dec"))

    def output(c):
        rows = rows_of(c)
        out = tmp[c].pop("intra") + _dot_nt(tmp[c].pop("q_dec"), tmp[c].pop("st_in"))
        ms = jnp.mean(out * out, axis=-1, keepdims=True)
        y = out * lax.rsqrt(ms + EPS) * norm_g
        gg = g_ref[rows, :].astype(F32)
        o_ref[rows, :] = (y * (gg * _sigmoid(gg))).astype(o_ref.dtype)

    def finish():
        st_ref[...] = state[0]

    return [decay_table, scores, recurrence, output], finish


def _gla(proj, small, w2, gate_b, norm_g, *, batch, seq):
    decay_sel, pair_mask = _gla_tables()
    t = proj.shape[0]
    lb = SEQ_BLOCK
    nsb = seq // lb
    h_ = GLA_HEADS
    qb = GLA_DK
    vb = GLA_DV

    def rows(b, h, i):
        return b * nsb + i

    in_specs = [
        pl.BlockSpec((lb, qb), lambda b, h, i: (rows(b, h, i), h)),
        pl.BlockSpec((lb, qb), lambda b, h, i: (rows(b, h, i), h_ + h)),
        pl.BlockSpec((lb, vb), lambda b, h, i: (rows(b, h, i), (2 * h_ * qb) // vb + h)),
        pl.BlockSpec((lb, vb), lambda b, h, i: (rows(b, h, i), (2 * h_ * qb) // vb + h_ + h)),
        pl.BlockSpec((lb, SMALL_COLS), lambda b, h, i: (rows(b, h, i), 0)),
        pl.BlockSpec((None, SMALL_COLS, qb), lambda b, h, i: (h, 0, 0)),
        pl.BlockSpec((None, 1, qb), lambda b, h, i: (h, 0, 0)),
        pl.BlockSpec((None, 1, vb), lambda b, h, i: (h, 0, 0)),
        pl.BlockSpec(decay_sel.shape, lambda b, h, i: (0, 0)),
        pl.BlockSpec(pair_mask.shape, lambda b, h, i: (0, 0, 0)),
    ]
    return dict(
        in_specs=in_specs,
        args=[proj, proj, proj, proj, small, w2, gate_b, norm_g, decay_sel, pair_mask],
        out_shape=jax.ShapeDtypeStruct((t, h_ * vb), BF16),
        out_spec=pl.BlockSpec((lb, vb), lambda b, h, i: (rows(b, h, i), h)),
        scratch=[pltpu.VMEM((vb, qb), F32)])


def _mlstm_tables():
    lb, c = CONV_BLOCK, CHUNK
    t = np.arange(lb)[:, None]
    u = np.arange(lb)[None, :]
    shift = np.concatenate([(u == t - d) for d in range(1, CONV_WIDTH)], axis=0).astype(np.float32)
    tc = np.arange(c)[:, None]
    uc = np.arange(c)[None, :]
    cum = np.concatenate([uc <= tc, uc > tc], axis=0).astype(np.float32)
    col = np.concatenate([tc > uc, np.ones((c, c), bool)], axis=1).astype(np.float32)
    return (jnp.asarray(shift, BF16), jnp.asarray(np.concatenate([cum] * 3, axis=1), BF16),
            jnp.asarray(col))


def _mlstm_stream(gbias_ref, q_ref, k_ref, v_ref, og_ref, sm_ref, igt_ref, cwq_ref, cwk_ref,
                  cbq_ref, cbk_ref, ng_ref, shift_ref, cum_ref, col_ref, o_ref,
                  ct_ref, n_ref, m_ref, hist_ref):
    c_len = CHUNK
    lb = SEQ_BLOCK
    taps = CONV_WIDTH
    h = pl.program_id(1)

    cb = CONV_BLOCK
    w = jnp.concatenate([cwq_ref[...], cwk_ref[...]], axis=1)
    bias = jnp.concatenate([cbq_ref[...], cbk_ref[...]], axis=1)
    tail = hist_ref[...]
    acts = []
    for s in range(lb // cb):
        x16 = jnp.concatenate([q_ref[s * cb:(s + 1) * cb, :], k_ref[s * cb:(s + 1) * cb, :]], axis=1)
        x = x16.astype(F32)
        shifted = _dot(shift_ref[...], x16)
        head = jnp.concatenate([tail, x[0:SUBLANES, :]], axis=0)
        y = bias + w[taps - 1:taps, :] * x
        for d in range(1, taps):
            sh = jnp.concatenate([head[SUBLANES - d:2 * SUBLANES - d, :],
                                  shifted[(d - 1) * cb + SUBLANES:d * cb, :]], axis=0)
            y = y + w[taps - 1 - d:taps - d, :] * sh
        tail = x[cb - SUBLANES:cb, :]
        acts.append(y * _sigmoid(y))
    hist_ref[...] = tail
    act = jnp.concatenate(acts, axis=0)
    q_all = act[:, 0:MLSTM_DK]
    k_all = act[:, MLSTM_DK:2 * MLSTM_DK] * (MLSTM_DK ** -0.5)

    ib = gbias_ref[h]
    fb = gbias_ref[MLSTM_HEADS + h]
    sm = sm_ref[...]
    lane_s = lax.broadcasted_iota(jnp.int32, (lb, SMALL_COLS), 1)
    i_col_all = jnp.sum(jnp.where(lane_s == IG_COL + h, sm, 0.0), axis=1, keepdims=True) + ib
    f_col_all = jnp.sum(jnp.where(lane_s == FG_COL + h, sm, 0.0), axis=1, keepdims=True) + fb
    lf_col_all = _log_sigmoid(f_col_all)

    norm_g = ng_ref[...]
    row_c = lax.broadcasted_iota(jnp.int32, (c_len, c_len), 0)
    col_c = lax.broadcasted_iota(jnp.int32, (c_len, c_len), 1)
    causal = row_c >= col_c

    state = [ct_ref[...], n_ref[...], m_ref[...]]
    tmp = [dict() for _ in range(lb // c_len)]

    def rows_of(c):
        return slice(c * c_len, (c + 1) * c_len)

    def gate_sums(c):
        rows = rows_of(c)
        g_hi, g_mid, g_lo = _split3(lf_col_all[rows, :] * col_ref[...])
        sums = _dot(cum_ref[...], jnp.concatenate([g_hi, g_mid, g_lo], axis=0))
        i_row = igt_ref[c, pl.ds(h, 1), :] + ib
        log_d = jnp.where(causal, sums[0:c_len, 0:c_len] + i_row, -jnp.inf)
        bb_col = sums[0:c_len, c_len:c_len + 1]
        log_w = sums[c_len:2 * c_len, c_len:c_len + 1] + i_col_all[rows, :]
        tmp[c].update(log_d=log_d, bb_col=bb_col, log_w=log_w,
                      d_max=jnp.max(log_d, axis=1, keepdims=True),
                      w_max=jnp.max(log_w, axis=0, keepdims=True),
                      qk=_dot_nt(q_all[rows, :].astype(BF16), k_all[rows, :].astype(BF16)))

    def stabiliser(c):
        d = tmp[c]
        m_prev = state[2]
        bb_col = d.pop("bb_col")
        b_last = bb_col[c_len - 1:c_len, :]
        inter = bb_col + m_prev
        m = jnp.maximum(inter, d.pop("d_max"))
        m_new = jnp.maximum(b_last + m_prev, d.pop("w_max"))
        d.update(m=m, g_inter=jnp.exp(inter - m), w_col=jnp.exp(d.pop("log_w") - m_new),
                 decay=jnp.exp(b_last + m_prev - m_new))
        state[2] = m_new

    def scores(c):
        rows = rows_of(c)
        d = tmp[c]
        s = d.pop("qk") * jnp.exp(d.pop("log_d") - d["m"])
        kw = k_all[rows, :] * d.pop("w_col")
        d.update(s_sum=jnp.sum(s, axis=1, keepdims=True),
                 intra=_dot(s.astype(BF16), v_ref[rows, :]),
                 kw=kw.astype(BF16),
                 k_sum=jnp.sum(kw, axis=0, keepdims=True))

    def recurrence(c):
        d = tmp[c]
        decay = d.pop("decay")
        d.update(ct_in=state[0].astype(BF16), n_in=state[1])
        state[0] = decay * state[0] + _dot_tn(v_ref[rows_of(c), :], d.pop("kw"))
        state[1] = decay * state[1] + d.pop("k_sum")

    def output(c):
        rows = rows_of(c)
        d = tmp[c]
        q = q_all[rows, :]
        g_inter = d.pop("g_inter")
        num = d.pop("intra") + g_inter * _dot_nt(q.astype(BF16), d.pop("ct_in"))
        den = d.pop("s_sum") + g_inter * jnp.sum(q * d.pop("n_in"), axis=1, keepdims=True)
        hid = num / jnp.maximum(jnp.abs(den), jnp.exp(-d.pop("m")))
        ms = jnp.mean(hid * hid, axis=-1, keepdims=True)
        y_out = hid * lax.rsqrt(ms + EPS) * norm_g
        og = og_ref[rows, :].astype(F32)
        o_ref[rows, :] = (_sigmoid(og) * y_out).astype(o_ref.dtype)

    def finish():
        ct_ref[...], n_ref[...], m_ref[...] = state

    return [gate_sums, stabiliser, scores, recurrence, output], finish


def _mlstm(proj, small, ig_t, gate_bias, conv_w, conv_b, norm_g, *, batch, seq):
    shift, cum_sel, col_sel = _mlstm_tables()
    t = proj.shape[0]
    lb = SEQ_BLOCK
    nsb = seq // lb
    h_ = MLSTM_HEADS
    qb = MLSTM_DK
    vb = MLSTM_DV
    q0 = 0
    v0 = (2 * h_ * qb) // vb

    def rows(b, h, i):
        return b * nsb + i

    in_specs = [
        pl.BlockSpec(memory_space=pltpu.SMEM),
        pl.BlockSpec((lb, qb), lambda b, h, i: (rows(b, h, i), q0 + h)),
        pl.BlockSpec((lb, qb), lambda b, h, i: (rows(b, h, i), q0 + h_ + h)),
        pl.BlockSpec((lb, vb), lambda b, h, i: (rows(b, h, i), v0 + h)),
        pl.BlockSpec((lb, vb), lambda b, h, i: (rows(b, h, i), v0 + h_ + h)),
        pl.BlockSpec((lb, SMALL_COLS), lambda b, h, i: (rows(b, h, i), 0)),
        pl.BlockSpec((lb // CHUNK, 2 * h_, CHUNK), lambda b, h, i: (rows(b, h, i), 0, 0)),
        pl.BlockSpec((CONV_WIDTH, qb), lambda b, h, i: (0, h)),
        pl.BlockSpec((CONV_WIDTH, qb), lambda b, h, i: (0, h_ + h)),
        pl.BlockSpec((1, qb), lambda b, h, i: (0, h)),
        pl.BlockSpec((1, qb), lambda b, h, i: (0, h_ + h)),
        pl.BlockSpec((None, 1, vb), lambda b, h, i: (h, 0, 0)),
        pl.BlockSpec(shift.shape, lambda b, h, i: (0, 0)),
        pl.BlockSpec(cum_sel.shape, lambda b, h, i: (0, 0)),
        pl.BlockSpec(col_sel.shape, lambda b, h, i: (0, 0)),
    ]
    scratch = [
        pltpu.VMEM((vb, qb), F32),
        pltpu.VMEM((1, qb), F32),
        pltpu.VMEM((1, 1), F32),
        pltpu.VMEM((SUBLANES, 2 * qb), F32),
    ]
    return dict(
        in_specs=in_specs,
        args=[gate_bias, proj, proj, proj, proj, small, ig_t, conv_w, conv_w, conv_b, conv_b, norm_g,
              shift, cum_sel, col_sel],
        out_shape=jax.ShapeDtypeStruct((t, h_ * vb), BF16),
        out_spec=pl.BlockSpec((lb, vb), lambda b, h, i: (rows(b, h, i), h)),
        scratch=scratch)


def _mixer_scan_kernel(*refs, n_gla_in, n_ml_in, n_gla_scratch):
    gla_in = refs[:n_gla_in]
    ml_in = refs[n_gla_in:n_gla_in + n_ml_in]
    o_gla, o_ml = refs[n_gla_in + n_ml_in:n_gla_in + n_ml_in + 2]
    scratch = refs[n_gla_in + n_ml_in + 2:]

    @pl.when(pl.program_id(2) == 0)
    def _():
        for ref in scratch:
            ref[...] = jnp.zeros_like(ref)

    gla_stages, gla_finish = _gla_stream(*gla_in, o_gla, *scratch[:n_gla_scratch])
    ml_stages, ml_finish = _mlstm_stream(*ml_in, o_ml, *scratch[n_gla_scratch:])
    n_chunks = SEQ_BLOCK // CHUNK
    lead = len(ml_stages) - len(gla_stages)
    for t in range(n_chunks + len(ml_stages) - 1):
        for k in reversed(range(len(ml_stages))):
            c = t - k
            if 0 <= c < n_chunks:
                ml_stages[k](c)
                if k >= lead:
                    gla_stages[k - lead](c)
    gla_finish()
    ml_finish()


def _mixer_scan(gla, ml, *, batch, seq):
    assert GLA_HEADS == MLSTM_HEADS
    outs = pl.pallas_call(
        functools.partial(_mixer_scan_kernel, n_gla_in=len(gla["args"]), n_ml_in=len(ml["args"]),
                          n_gla_scratch=len(gla["scratch"])),
        out_shape=[gla["out_shape"], ml["out_shape"]],
        grid=(batch, GLA_HEADS, seq // SEQ_BLOCK),
        in_specs=gla["in_specs"] + ml["in_specs"],
        out_specs=[gla["out_spec"], ml["out_spec"]],
        scratch_shapes=gla["scratch"] + ml["scratch"],
        compiler_params=pltpu.CompilerParams(
            dimension_semantics=("parallel", "parallel", "arbitrary"),
            vmem_limit_bytes=_vmem_limit(24 << 20)),
        name="mixer_scan",
    )(*gla["args"], *ml["args"])
    return outs[0], outs[1]


def _xattn_kernel(a_ref, ssq_ref, wq_ref, k_ref, v_ref, o_ref):
    a = a_ref[...]
    r = lax.rsqrt(ssq_ref[:, 0:1] * (1.0 / a.shape[1]) + EPS)
    q = (r * _dot(a, wq_ref[...])).astype(BF16)
    groups = [slice(g * XATTN_ROWS, (g + 1) * XATTN_ROWS) for g in range(a.shape[0] // XATTN_ROWS)]
    scores = [_dot_nt(q[rows, :], k_ref[...]) * (XATTN_HEAD_DIM ** -0.5) for rows in groups]
    probs = []
    for s in scores:
        p = jnp.exp(s - jnp.max(s, axis=-1, keepdims=True))
        probs.append((p / jnp.sum(p, axis=-1, keepdims=True)).astype(BF16))
    for rows, p in zip(groups, probs):
        o_ref[rows, :] = _dot(p, v_ref[...]).astype(o_ref.dtype)


def _xattn(a, ssq, wq, k, v, *, batch, seq, tq):
    t, d = a.shape
    mem = k.shape[0] // batch
    hd = XATTN_HEAD_DIM
    nq = seq // tq
    nbytes = 2 * (tq * d + d * hd + 2 * mem * hd + tq * hd) * 2 + 2 * tq * LANES * 4 + 3 * tq * hd * 4
    return pl.pallas_call(
        _xattn_kernel,
        out_shape=jax.ShapeDtypeStruct((t, d), BF16),
        grid=(batch * nq, XATTN_HEADS),
        in_specs=[pl.BlockSpec((tq, d), lambda i, h: (i, 0)),
                  pl.BlockSpec((tq, LANES), lambda i, h: (i, 0)),
                  pl.BlockSpec((d, hd), lambda i, h: (0, h)),
                  pl.BlockSpec((mem, hd), lambda i, h: (i // nq, h)),
                  pl.BlockSpec((mem, hd), lambda i, h: (i // nq, h))],
        out_specs=pl.BlockSpec((tq, hd), lambda i, h: (i, h)),
        compiler_params=pltpu.CompilerParams(
            dimension_semantics=("parallel", "parallel"),
            vmem_limit_bytes=_vmem_limit(nbytes)),
        name="cross_attention",
    )(a, ssq, wq, k, v)


def kernel(x, mem, norm_mix_g, w_in, gla_gate_w2, gla_gate_b, gla_norm_g, mlstm_conv_w, mlstm_conv_b, mlstm_igate_b, mlstm_fgate_b, mlstm_norm_g, w_out, norm_cross_g, norm_mem_g, wq_c, wk_c, wv_c, wo_c, norm_ffn_g, w_gate, w_up, w_down, norm_final_g):
    batch, seq, d = x.shape
    mem_tokens = mem.shape[1]
    t = batch * seq
    depth = w_in.shape[0]
    gla_cols = 2 * GLA_HEADS * GLA_DK + 2 * GLA_HEADS * GLA_DV
    ml_cols = 2 * MLSTM_HEADS * MLSTM_DK + 2 * MLSTM_HEADS * MLSTM_DV
    ml_start = gla_cols + GLA_GATE_RANK

    h = x.reshape(t, d)
    mem2 = mem.reshape(batch * mem_tokens, d)
    for l in range(depth):
        w_in16 = w_in[l].astype(BF16)
        w_ml = w_in16[:, ml_start:ml_start + ml_cols]
        w_small = jnp.concatenate(
            [w_in[l][:, gla_cols:ml_start], w_in[l][:, ml_start + ml_cols:]], axis=1)
        w_small = jnp.pad(w_small, ((0, 0), (0, SMALL_COLS - w_small.shape[1]))).astype(BF16)
        w2 = gla_gate_w2[l].reshape(GLA_GATE_RANK, GLA_HEADS, GLA_DK).transpose(1, 0, 2)
        w2 = jnp.pad(w2, ((0, 0), (0, SMALL_COLS - GLA_GATE_RANK), (0, 0)))
        gate_b = gla_gate_b[l].reshape(GLA_HEADS, 1, GLA_DK)
        gla_g = gla_norm_g[l].reshape(GLA_HEADS, 1, GLA_DV)
        ml_g = mlstm_norm_g[l].reshape(MLSTM_HEADS, 1, MLSTM_DV)
        gate_bias = jnp.concatenate([mlstm_igate_b[l], mlstm_fgate_b[l]]).astype(F32)
        conv_b = mlstm_conv_b[l].reshape(1, -1)

        n1 = _rmsnorm(h, norm_mix_g[l], BF16)
        proj_gla = _matmul([n1], w_in16, BF16, n_cols=gla_cols, tm=1024, tn=1024, name="in_proj_gla")
        proj_ml = _matmul([n1], w_ml, BF16, tm=1024, tn=1024, name="in_proj_mlstm")
        small = _matmul([n1], w_small, F32, tm=1024, tn=SMALL_COLS, name="in_proj_small")
        ig_t = small[:, IG_COL:IG_COL + 2 * MLSTM_HEADS].reshape(
            t // CHUNK, CHUNK, 2 * MLSTM_HEADS).transpose(0, 2, 1)
        o_gla, o_ml = _mixer_scan(
            _gla(proj_gla, small, w2, gate_b, gla_g, batch=batch, seq=seq),
            _mlstm(proj_ml, small, ig_t, gate_bias, mlstm_conv_w[l], conv_b, ml_g, batch=batch, seq=seq),
            batch=batch, seq=seq)
        h, hb, ssq = _matmul([o_gla, o_ml], w_out[l].astype(BF16), F32, resid=h,
                             next_gain=norm_cross_g[l], tm=1024, tn=512, name="out_proj")

        mem_n = _rmsnorm(mem2, norm_mem_g[l], BF16)
        k_mem = _matmul([mem_n], wk_c[l].astype(BF16), BF16, tm=512, tn=1024, name="xattn_k")
        v_mem = _matmul([mem_n], wv_c[l].astype(BF16), BF16, tm=512, tn=1024, name="xattn_v")
        attn = _xattn(hb, ssq, wq_c[l].astype(BF16), k_mem, v_mem, batch=batch, seq=seq, tq=1024)
        h, hb, ssq = _matmul([attn], wo_c[l].astype(BF16), F32, resid=h,
                             next_gain=norm_ffn_g[l], tm=1024, tn=512, name="xattn_o")

        hid = _swiglu_up(hb, ssq, w_gate[l], w_up[l], tm=2048, tn=FFN_TN)
        h = _matmul([hid], w_down[l].astype(BF16), F32, resid=h, w_outer=True, tm=512, tn=512,
                    name="ffn_down")

    y = _rmsnorm(h, norm_final_g, x.dtype)
    return y.reshape(batch, seq, d)
```

```python
import functools
import math

import jax
import jax.numpy as jnp
import numpy as np
from jax import lax
from jax.experimental import pallas as pl
from jax.experimental.pallas import tpu as pltpu

F32 = jnp.float32
BF16 = jnp.bfloat16

D_MODEL = 4096
GROUP_WIDTH = D_MODEL // 2
GLA_HEADS = 4
GLA_DV = GROUP_WIDTH // GLA_HEADS
GLA_DK = GLA_DV // 2
GLA_GATE_RANK = 16
GLA_TAU = 16.0
MLSTM_HEADS = 4
MLSTM_DV = GROUP_WIDTH // MLSTM_HEADS
MLSTM_DK = MLSTM_DV // 2
CONV_WIDTH = 4
XATTN_HEADS = 4
XATTN_HEAD_DIM = D_MODEL // XATTN_HEADS
FFN_HIDDEN = 256 * math.ceil(8 * D_MODEL / (3 * 256))
EPS = 1e-6

V7X_VMEM_LIMIT_BYTES = 56 * 1024 * 1024
LANES = 128
SUBLANES = 8

CHUNK = 64
GLA_LEVELS = (32, 16, 8, 4, 2, 1)
LOG2_E = math.log2(math.e)
SEQ_BLOCK = 1024
CONV_BLOCK = 256
NORM_ROWS = 256
FFN_TN = 256
XATTN_ROWS = 256
SWIGLU_PASS_ROWS = 1024
SMALL_COLS = LANES
IG_COL = GLA_GATE_RANK
FG_COL = GLA_GATE_RANK + MLSTM_HEADS


def _vmem_limit(nbytes):
    return int(min(V7X_VMEM_LIMIT_BYTES, max(32 * 1024 * 1024, nbytes * 5 // 4 + (4 << 20))))


def _dot(a, b):
    return jnp.dot(a, b, preferred_element_type=F32)


def _dot_nt(a, b):
    return lax.dot_general(a, b, (((1,), (1,)), ((), ())), preferred_element_type=F32)


def _dot_tn(a, b):
    return lax.dot_general(a, b, (((0,), (0,)), ((), ())), preferred_element_type=F32)


def _split3(x):
    hi = x.astype(BF16)
    r = x - hi.astype(F32)
    mid = r.astype(BF16)
    lo = (r - mid.astype(F32)).astype(BF16)
    return hi, mid, lo


def _log_sigmoid(z):
    return jnp.minimum(z, 0.0) - jnp.log(1.0 + jnp.exp(-jnp.abs(z)))


def _sigmoid(z):
    return 1.0 / (1.0 + jnp.exp(-z))


def _rmsnorm_kernel(x_ref, g_ref, o_ref):
    x = x_ref[...].astype(F32)
    ms = jnp.mean(x * x, axis=-1, keepdims=True)
    o_ref[...] = (x * lax.rsqrt(ms + EPS) * g_ref[...]).astype(o_ref.dtype)


def _rmsnorm(x, g, out_dtype):
    rows, d = x.shape
    tr = min(NORM_ROWS, rows)
    nbytes = 2 * tr * d * (x.dtype.itemsize + jnp.dtype(out_dtype).itemsize) + 3 * tr * d * 4
    return pl.pallas_call(
        _rmsnorm_kernel,
        out_shape=jax.ShapeDtypeStruct((rows, d), out_dtype),
        grid=(rows // tr,),
        in_specs=[pl.BlockSpec((tr, d), lambda i: (i, 0)),
                  pl.BlockSpec((1, d), lambda i: (0, 0))],
        out_specs=pl.BlockSpec((tr, d), lambda i: (i, 0)),
        compiler_params=pltpu.CompilerParams(
            dimension_semantics=("parallel",), vmem_limit_bytes=_vmem_limit(nbytes)),
        name="rmsnorm",
    )(x, g.reshape(1, d).astype(F32))


def _mm_kernel(*refs, n_a, has_resid, has_norm):
    a_refs = refs[:n_a]
    w_ref = refs[n_a]
    pos = n_a + 1
    r_ref = refs[pos] if has_resid else None
    pos += int(has_resid)
    g_ref = refs[pos] if has_norm else None
    pos += int(has_norm)
    o_ref = refs[pos]

    kp = a_refs[0].shape[1]
    acc = _dot(a_refs[0][...], w_ref[0:kp, :])
    for p in range(1, n_a):
        acc = acc + _dot(a_refs[p][...], w_ref[p * kp:(p + 1) * kp, :])
    if has_resid:
        acc = r_ref[...] + acc
    o_ref[...] = acc.astype(o_ref.dtype)

    if has_norm:
        hb_ref, ssq_ref = refs[pos + 1], refs[pos + 2]
        hb_ref[...] = (acc * g_ref[...]).astype(hb_ref.dtype)
        part = jnp.broadcast_to(jnp.sum(acc * acc, axis=1, keepdims=True), ssq_ref.shape)
        j = pl.program_id(1)

        @pl.when(j == 0)
        def _():
            ssq_ref[...] = part

        @pl.when(j > 0)
        def _():
            ssq_ref[...] += part


def _matmul(a_list, w, out_dtype, *, resid=None, next_gain=None, n_cols=None, w_outer=False, tm, tn, name):
    m, kp = a_list[0].shape
    kdim = w.shape[0]
    n = w.shape[1] if n_cols is None else n_cols
    n_a = len(a_list)
    assert m % tm == 0 and n % tn == 0 and kdim == n_a * kp
    assert not (w_outer and next_gain is not None)

    def spec(shape, f):
        return pl.BlockSpec(shape, (lambda j, i: f(i, j)) if w_outer else f)

    in_specs = ([spec((tm, kp), lambda i, j: (i, 0)) for _ in a_list]
                + [spec((kdim, tn), lambda i, j: (0, j))])
    args = list(a_list) + [w]
    out_shape = [jax.ShapeDtypeStruct((m, n), out_dtype)]
    out_specs = [spec((tm, tn), lambda i, j: (i, j))]
    nbytes = 2 * (tm * kdim + kdim * tn) * 2 + 2 * tm * tn * jnp.dtype(out_dtype).itemsize + 2 * tm * tn * 4
    if resid is not None:
        in_specs.append(spec((tm, tn), lambda i, j: (i, j)))
        args.append(resid)
        nbytes += 2 * tm * tn * 4
    if next_gain is not None:
        in_specs.append(spec((1, tn), lambda i, j: (0, j)))
        args.append(next_gain.reshape(1, n).astype(F32))
        out_shape += [jax.ShapeDtypeStruct((m, n), BF16), jax.ShapeDtypeStruct((m, LANES), F32)]
        out_specs += [spec((tm, tn), lambda i, j: (i, j)),
                      spec((tm, LANES), lambda i, j: (i, 0))]
        nbytes += 2 * tm * tn * 2 + 2 * tm * LANES * 4 + tm * tn * 4
    outs = pl.pallas_call(
        functools.partial(_mm_kernel, n_a=n_a, has_resid=resid is not None,
                          has_norm=next_gain is not None),
        out_shape=out_shape,
        grid=(n // tn, m // tm) if w_outer else (m // tm, n // tn),
        in_specs=in_specs,
        out_specs=out_specs,
        compiler_params=pltpu.CompilerParams(
            dimension_semantics=("parallel", "arbitrary"),
            vmem_limit_bytes=_vmem_limit(nbytes)),
        name=name,
    )(*args)
    return outs if next_gain is not None else outs[0]


def _short_proj_kernel(a_ref, wk_ref, wv_ref, k_ref, v_ref):
    a = a_ref[...]
    k_ref[...] = _dot(a, wk_ref[...].astype(BF16)).astype(k_ref.dtype)
    v_ref[...] = _dot(a, wv_ref[...].astype(BF16)).astype(v_ref.dtype)


def _short_proj(a, wk, wv, *, tn):
    m, kdim = a.shape
    n = wk.shape[1]
    assert n % tn == 0
    nbytes = 2 * m * kdim * 2 + 2 * kdim * tn * (2 * 4 + 2) + 4 * m * tn * 2 + 2 * m * tn * 4
    return pl.pallas_call(
        _short_proj_kernel,
        out_shape=[jax.ShapeDtypeStruct((m, n), BF16)] * 2,
        grid=(n // tn,),
        in_specs=[pl.BlockSpec((m, kdim), lambda j: (0, 0)),
                  pl.BlockSpec((kdim, tn), lambda j: (0, j)),
                  pl.BlockSpec((kdim, tn), lambda j: (0, j))],
        out_specs=[pl.BlockSpec((m, tn), lambda j: (0, j))] * 2,
        compiler_params=pltpu.CompilerParams(
            dimension_semantics=("parallel",), vmem_limit_bytes=_vmem_limit(nbytes)),
        name="xattn_kv",
    )(a, wk, wv)


def _swiglu_up_kernel(a_ref, ssq_ref, wg_ref, wu_ref, o_ref):
    wg = wg_ref[...].astype(BF16)
    wu = wu_ref[...].astype(BF16)
    kdim = a_ref.shape[1]
    for p in range(a_ref.shape[0] // SWIGLU_PASS_ROWS):
        rows = slice(p * SWIGLU_PASS_ROWS, (p + 1) * SWIGLU_PASS_ROWS)
        a = a_ref[rows, :]
        r = lax.rsqrt(ssq_ref[rows, 0:1] * (1.0 / kdim) + EPS)
        gate = r * _dot(a, wg)
        up = r * _dot(a, wu)
        o_ref[rows, :] = (gate * _sigmoid(gate) * up).astype(o_ref.dtype)


def _swiglu_up(a, ssq, wg, wu, *, tm, tn):
    m, kdim = a.shape
    n = wg.shape[1]
    assert m % tm == 0 and n % tn == 0 and tm % SWIGLU_PASS_ROWS == 0
    nbytes = (tm * kdim * 2 + 2 * kdim * tn * (2 * 4 + 2) + 2 * tm * tn * 2 + 2 * tm * LANES * 4
              + 3 * SWIGLU_PASS_ROWS * tn * 4)
    return pl.pallas_call(
        _swiglu_up_kernel,
        out_shape=jax.ShapeDtypeStruct((m, n), BF16),
        grid=(m // tm, n // tn),
        in_specs=[pl.BlockSpec((tm, kdim), lambda i, j: (i, 0), pipeline_mode=pl.Buffered(1)),
                  pl.BlockSpec((tm, LANES), lambda i, j: (i, 0)),
                  pl.BlockSpec((kdim, tn), lambda i, j: (0, j)),
                  pl.BlockSpec((kdim, tn), lambda i, j: (0, j))],
        out_specs=pl.BlockSpec((tm, tn), lambda i, j: (i, j)),
        compiler_params=pltpu.CompilerParams(
            dimension_semantics=("parallel", "arbitrary"),
            vmem_limit_bytes=_vmem_limit(nbytes)),
        name="swiglu_up",
    )(a, ssq, wg, wu)


def _gla_tables():
    c = CHUNK
    row = np.arange(c)[:, None]
    u = np.arange(c)[None, :]
    sel = [u <= row, u > row]
    masks = [row == u]
    for bsz in GLA_LEVELS:
        r = (row // (2 * bsz)) * (2 * bsz) + bsz - 1
        sel.append(np.where(row <= r, (u > row) & (u <= r), (u > r) & (u <= row)))
        same_pair = (row // (2 * bsz)) == (u // (2 * bsz))
        masks.append(same_pair & ((row % (2 * bsz)) >= bsz) & ((u % (2 * bsz)) < bsz))
    sel = np.concatenate(sel, axis=0).astype(np.float32)
    return (jnp.asarray(np.concatenate([sel] * 3, axis=1), BF16),
            jnp.asarray(np.stack(masks).astype(np.float32)))


def _gla_stream(q_ref, k_ref, v_ref, g_ref, sm_ref, w2_ref, gb_ref, ng_ref, sel_ref, mask_ref,
                o_ref, st_ref):
    c_len = CHUNK
    norm_g = ng_ref[...]
    z = _dot(sm_ref[...].astype(BF16), w2_ref[...].astype(BF16)) + gb_ref[...]
    la_hi, la_mid, la_lo = _split3(_log_sigmoid(z) * (LOG2_E / GLA_TAU))
    row_k = lax.broadcasted_iota(jnp.int32, (c_len, GLA_DK), 0)
    state = [st_ref[...]]
    tmp = [dict() for _ in range(SEQ_BLOCK // c_len)]

    def rows_of(c):
        return slice(c * c_len, (c + 1) * c_len)

    def decay_table(c):
        rows = rows_of(c)
        la3 = jnp.concatenate([la_hi[rows, :], la_mid[rows, :], la_lo[rows, :]], axis=0)
        tmp[c]["e"] = jnp.exp2(_dot(sel_ref[...], la3))

    def scores(c):
        rows = rows_of(c)
        e = tmp[c].pop("e")
        q = q_ref[rows, :].astype(F32) * (GLA_DK ** -0.5)
        k = k_ref[rows, :].astype(F32)
        k_dec = (k * e[c_len:2 * c_len, :]).astype(BF16)
        attn = _dot_nt(q.astype(BF16), k.astype(BF16)) * mask_ref[0]
        for lv, bsz in enumerate(GLA_LEVELS):
            in_right = (row_k & bsz) != 0
            x = (jnp.where(in_right, q, k) * e[(2 + lv) * c_len:(3 + lv) * c_len, :]).astype(BF16)
            attn = attn + _dot_nt(x, x) * mask_ref[1 + lv]
        tmp[c].update(q_dec=(q * e[0:c_len, :]).astype(BF16), k_dec=k_dec,
                      decay=e[c_len - 1:c_len, :],
                      intra=_dot(attn.astype(BF16), v_ref[rows, :]))

    def recurrence(c):
        tmp[c]["st_in"] = state[0].astype(BF16)
        state[0] = state[0] * tmp[c].pop("decay") + _dot_tn(v_ref[rows_of(c), :], tmp[c].pop("k_dec"))

    def output(c):
        rows = rows_of(c)
        out = tmp[c].pop("intra") + _dot_nt(tmp[c].pop("q_dec"), tmp[c].pop("st_in"))
        ms = jnp.mean(out * out, axis=-1, keepdims=True)
        y = out * lax.rsqrt(ms + EPS) * norm_g
        gg = g_ref[rows, :].astype(F32)
        o_ref[rows, :] = (y * (gg * _sigmoid(gg))).astype(o_ref.dtype)

    def finish():
        st_ref[...] = state[0]

    return [decay_table, scores, recurrence, output], finish


def _gla(proj, small, w2, gate_b, norm_g, *, batch, seq):
    decay_sel, pair_mask = _gla_tables()
    t = proj.shape[0]
    lb = SEQ_BLOCK
    nsb = seq // lb
    h_ = GLA_HEADS
    qb = GLA_DK
    vb = GLA_DV

    def rows(b, h, i):
        return b * nsb + i

    in_specs = [
        pl.BlockSpec((lb, qb), lambda b, h, i: (rows(b, h, i), h)),
        pl.BlockSpec((lb, qb), lambda b, h, i: (rows(b, h, i), h_ + h)),
        pl.BlockSpec((lb, vb), lambda b, h, i: (rows(b, h, i), (2 * h_ * qb) // vb + h)),
        pl.BlockSpec((lb, vb), lambda b, h, i: (rows(b, h, i), (2 * h_ * qb) // vb + h_ + h)),
        pl.BlockSpec((lb, SMALL_COLS), lambda b, h, i: (rows(b, h, i), 0)),
        pl.BlockSpec((None, SMALL_COLS, qb), lambda b, h, i: (h, 0, 0)),
        pl.BlockSpec((None, 1, qb), lambda b, h, i: (h, 0, 0)),
        pl.BlockSpec((None, 1, vb), lambda b, h, i: (h, 0, 0)),
        pl.BlockSpec(decay_sel.shape, lambda b, h, i: (0, 0)),
        pl.BlockSpec(pair_mask.shape, lambda b, h, i: (0, 0, 0)),
    ]
    return dict(
        in_specs=in_specs,
        args=[proj, proj, proj, proj, small, w2, gate_b, norm_g, decay_sel, pair_mask],
        out_shape=jax.ShapeDtypeStruct((t, h_ * vb), BF16),
        out_spec=pl.BlockSpec((lb, vb), lambda b, h, i: (rows(b, h, i), h)),
        scratch=[pltpu.VMEM((vb, qb), F32)])


def _mlstm_tables():
    lb, c = CONV_BLOCK, CHUNK
    t = np.arange(lb)[:, None]
    u = np.arange(lb)[None, :]
    shift = np.concatenate([(u == t - d) for d in range(1, CONV_WIDTH)], axis=0).astype(np.float32)
    tc = np.arange(c)[:, None]
    uc = np.arange(c)[None, :]
    cum = np.concatenate([uc <= tc, uc > tc], axis=0).astype(np.float32)
    col = np.concatenate([tc > uc, np.ones((c, c), bool)], axis=1).astype(np.float32)
    return (jnp.asarray(shift, BF16), jnp.asarray(np.concatenate([cum] * 3, axis=1), BF16),
            jnp.asarray(col))


def _mlstm_stream(gbias_ref, q_ref, k_ref, v_ref, og_ref, sm_ref, igt_ref, cwq_ref, cwk_ref,
                  cbq_ref, cbk_ref, ng_ref, shift_ref, cum_ref, col_ref, o_ref,
                  ct_ref, n_ref, m_ref, hist_ref):
    c_len = CHUNK
    lb = SEQ_BLOCK
    taps = CONV_WIDTH
    h = pl.program_id(1)

    cb = CONV_BLOCK
    w = jnp.concatenate([cwq_ref[...], cwk_ref[...]], axis=1)
    bias = jnp.concatenate([cbq_ref[...], cbk_ref[...]], axis=1)
    tail = hist_ref[...]
    acts = []
    for s in range(lb // cb):
        x16 = jnp.concatenate([q_ref[s * cb:(s + 1) * cb, :], k_ref[s * cb:(s + 1) * cb, :]], axis=1)
        x = x16.astype(F32)
        shifted = _dot(shift_ref[...], x16)
        head = jnp.concatenate([tail, x[0:SUBLANES, :]], axis=0)
        y = bias + w[taps - 1:taps, :] * x
        for d in range(1, taps):
            sh = jnp.concatenate([head[SUBLANES - d:2 * SUBLANES - d, :],
                                  shifted[(d - 1) * cb + SUBLANES:d * cb, :]], axis=0)
            y = y + w[taps - 1 - d:taps - d, :] * sh
        tail = x[cb - SUBLANES:cb, :]
        acts.append(y * _sigmoid(y))
    hist_ref[...] = tail
    act = jnp.concatenate(acts, axis=0)
    q_all = act[:, 0:MLSTM_DK]
    k_all = act[:, MLSTM_DK:2 * MLSTM_DK] * (MLSTM_DK ** -0.5)

    ib = gbias_ref[h]
    fb = gbias_ref[MLSTM_HEADS + h]
    sm = sm_ref[...]
    lane_s = lax.broadcasted_iota(jnp.int32, (lb, SMALL_COLS), 1)
    i_col_all = jnp.sum(jnp.where(lane_s == IG_COL + h, sm, 0.0), axis=1, keepdims=True) + ib
    f_col_all = jnp.sum(jnp.where(lane_s == FG_COL + h, sm, 0.0), axis=1, keepdims=True) + fb
    lf_col_all = _log_sigmoid(f_col_all)

    norm_g = ng_ref[...]
    row_c = lax.broadcasted_iota(jnp.int32, (c_len, c_len), 0)
    col_c = lax.broadcasted_iota(jnp.int32, (c_len, c_len), 1)
    causal = row_c >= col_c

    state = [ct_ref[...], n_ref[...], m_ref[...]]
    tmp = [dict() for _ in range(lb // c_len)]

    def rows_of(c):
        return slice(c * c_len, (c + 1) * c_len)

    def gate_sums(c):
        rows = rows_of(c)
        g_hi, g_mid, g_lo = _split3(lf_col_all[rows, :] * col_ref[...])
        sums = _dot(cum_ref[...], jnp.concatenate([g_hi, g_mid, g_lo], axis=0))
        i_row = igt_ref[c, pl.ds(h, 1), :] + ib
        log_d = jnp.where(causal, sums[0:c_len, 0:c_len] + i_row, -jnp.inf)
        bb_col = sums[0:c_len, c_len:c_len + 1]
        log_w = sums[c_len:2 * c_len, c_len:c_len + 1] + i_col_all[rows, :]
        tmp[c].update(log_d=log_d, bb_col=bb_col, log_w=log_w,
                      d_max=jnp.max(log_d, axis=1, keepdims=True),
                      w_max=jnp.max(log_w, axis=0, keepdims=True),
                      qk=_dot_nt(q_all[rows, :].astype(BF16), k_all[rows, :].astype(BF16)))

    def stabiliser(c):
        d = tmp[c]
        m_prev = state[2]
        bb_col = d.pop("bb_col")
        b_last = bb_col[c_len - 1:c_len, :]
        inter = bb_col + m_prev
        m = jnp.maximum(inter, d.pop("d_max"))
        m_new = jnp.maximum(b_last + m_prev, d.pop("w_max"))
        d.update(m=m, g_inter=jnp.exp(inter - m), w_col=jnp.exp(d.pop("log_w") - m_new),
                 decay=jnp.exp(b_last + m_prev - m_new))
        state[2] = m_new

    def scores(c):
        rows = rows_of(c)
        d = tmp[c]
        s = d.pop("qk") * jnp.exp(d.pop("log_d") - d["m"])
        kw = k_all[rows, :] * d.pop("w_col")
        d.update(s_sum=jnp.sum(s, axis=1, keepdims=True),
                 intra=_dot(s.astype(BF16), v_ref[rows, :]),
                 kw=kw.astype(BF16),
                 k_sum=jnp.sum(kw, axis=0, keepdims=True))

    def recurrence(c):
        d = tmp[c]
        decay = d.pop("decay")
        d.update(ct_in=state[0].astype(BF16), n_in=state[1])
        state[0] = decay * state[0] + _dot_tn(v_ref[rows_of(c), :], d.pop("kw"))
        state[1] = decay * state[1] + d.pop("k_sum")

    def output(c):
        rows = rows_of(c)
        d = tmp[c]
        q = q_all[rows, :]
        g_inter = d.pop("g_inter")
        num = d.pop("intra") + g_inter * _dot_nt(q.astype(BF16), d.pop("ct_in"))
        den = d.pop("s_sum") + g_inter * jnp.sum(q * d.pop("n_in"), axis=1, keepdims=True)
        hid = num / jnp.maximum(jnp.abs(den), jnp.exp(-d.pop("m")))
        ms = jnp.mean(hid * hid, axis=-1, keepdims=True)
        y_out = hid * lax.rsqrt(ms + EPS) * norm_g
        og = og_ref[rows, :].astype(F32)
        o_ref[rows, :] = (_sigmoid(og) * y_out).astype(o_ref.dtype)

    def finish():
        ct_ref[...], n_ref[...], m_ref[...] = state

    return [gate_sums, stabiliser, scores, recurrence, output], finish


def _mlstm(proj, small, ig_t, gate_bias, conv_w, conv_b, norm_g, *, batch, seq):
    shift, cum_sel, col_sel = _mlstm_tables()
    t = proj.shape[0]
    lb = SEQ_BLOCK
    nsb = seq // lb
    h_ = MLSTM_HEADS
    qb = MLSTM_DK
    vb = MLSTM_DV
    q0 = 0
    v0 = (2 * h_ * qb) // vb

    def rows(b, h, i):
        return b * nsb + i

    in_specs = [
        pl.BlockSpec(memory_space=pltpu.SMEM),
        pl.BlockSpec((lb, qb), lambda b, h, i: (rows(b, h, i), q0 + h)),
        pl.BlockSpec((lb, qb), lambda b, h, i: (rows(b, h, i), q0 + h_ + h)),
        pl.BlockSpec((lb, vb), lambda b, h, i: (rows(b, h, i), v0 + h)),
        pl.BlockSpec((lb, vb), lambda b, h, i: (rows(b, h, i), v0 + h_ + h)),
        pl.BlockSpec((lb, SMALL_COLS), lambda b, h, i: (rows(b, h, i), 0)),
        pl.BlockSpec((lb // CHUNK, 2 * h_, CHUNK), lambda b, h, i: (rows(b, h, i), 0, 0)),
        pl.BlockSpec((CONV_WIDTH, qb), lambda b, h, i: (0, h)),
        pl.BlockSpec((CONV_WIDTH, qb), lambda b, h, i: (0, h_ + h)),
        pl.BlockSpec((1, qb), lambda b, h, i: (0, h)),
        pl.BlockSpec((1, qb), lambda b, h, i: (0, h_ + h)),
        pl.BlockSpec((None, 1, vb), lambda b, h, i: (h, 0, 0)),
        pl.BlockSpec(shift.shape, lambda b, h, i: (0, 0)),
        pl.BlockSpec(cum_sel.shape, lambda b, h, i: (0, 0)),
        pl.BlockSpec(col_sel.shape, lambda b, h, i: (0, 0)),
    ]
    scratch = [
        pltpu.VMEM((vb, qb), F32),
        pltpu.VMEM((1, qb), F32),
        pltpu.VMEM((1, 1), F32),
        pltpu.VMEM((SUBLANES, 2 * qb), F32),
    ]
    return dict(
        in_specs=in_specs,
        args=[gate_bias, proj, proj, proj, proj, small, ig_t, conv_w, conv_w, conv_b, conv_b, norm_g,
              shift, cum_sel, col_sel],
        out_shape=jax.ShapeDtypeStruct((t, h_ * vb), BF16),
        out_spec=pl.BlockSpec((lb, vb), lambda b, h, i: (rows(b, h, i), h)),
        scratch=scratch)


def _mixer_scan_kernel(*refs, n_gla_in, n_ml_in, n_gla_scratch):
    gla_in = refs[:n_gla_in]
    ml_in = refs[n_gla_in:n_gla_in + n_ml_in]
    o_gla, o_ml = refs[n_gla_in + n_ml_in:n_gla_in + n_ml_in + 2]
    scratch = refs[n_gla_in + n_ml_in + 2:]

    @pl.when(pl.program_id(2) == 0)
    def _():
        for ref in scratch:
            ref[...] = jnp.zeros_like(ref)

    gla_stages, gla_finish = _gla_stream(*gla_in, o_gla, *scratch[:n_gla_scratch])
    ml_stages, ml_finish = _mlstm_stream(*ml_in, o_ml, *scratch[n_gla_scratch:])
    n_chunks = SEQ_BLOCK // CHUNK
    lead = len(ml_stages) - len(gla_stages)
    for t in range(n_chunks + len(ml_stages) - 1):
        for k in reversed(range(len(ml_stages))):
            c = t - k
            if 0 <= c < n_chunks:
                ml_stages[k](c)
                if k >= lead:
                    gla_stages[k - lead](c)
    gla_finish()
    ml_finish()


def _mixer_scan(gla, ml, *, batch, seq):
    assert GLA_HEADS == MLSTM_HEADS
    outs = pl.pallas_call(
        functools.partial(_mixer_scan_kernel, n_gla_in=len(gla["args"]), n_ml_in=len(ml["args"]),
                          n_gla_scratch=len(gla["scratch"])),
        out_shape=[gla["out_shape"], ml["out_shape"]],
        grid=(batch, GLA_HEADS, seq // SEQ_BLOCK),
        in_specs=gla["in_specs"] + ml["in_specs"],
        out_specs=[gla["out_spec"], ml["out_spec"]],
        scratch_shapes=gla["scratch"] + ml["scratch"],
        compiler_params=pltpu.CompilerParams(
            dimension_semantics=("parallel", "parallel", "arbitrary"),
            vmem_limit_bytes=_vmem_limit(24 << 20)),
        name="mixer_scan",
    )(*gla["args"], *ml["args"])
    return outs[0], outs[1]


def _xattn_kernel(a_ref, ssq_ref, wq_ref, k_ref, v_ref, o_ref):
    a = a_ref[...]
    r = lax.rsqrt(ssq_ref[:, 0:1] * (1.0 / a.shape[1]) + EPS)
    q = (r * _dot(a, wq_ref[...])).astype(BF16)
    groups = [slice(g * XATTN_ROWS, (g + 1) * XATTN_ROWS) for g in range(a.shape[0] // XATTN_ROWS)]
    scores = [_dot_nt(q[rows, :], k_ref[...]) * (XATTN_HEAD_DIM ** -0.5) for rows in groups]
    probs = []
    for s in scores:
        p = jnp.exp(s - jnp.max(s, axis=-1, keepdims=True))
        probs.append((p / jnp.sum(p, axis=-1, keepdims=True)).astype(BF16))
    for rows, p in zip(groups, probs):
        o_ref[rows, :] = _dot(p, v_ref[...]).astype(o_ref.dtype)


def _xattn(a, ssq, wq, k, v, *, batch, seq, tq):
    t, d = a.shape
    mem = k.shape[0] // batch
    hd = XATTN_HEAD_DIM
    nq = seq // tq
    nbytes = 2 * (tq * d + d * hd + 2 * mem * hd + tq * hd) * 2 + 2 * tq * LANES * 4 + 3 * tq * hd * 4
    return pl.pallas_call(
        _xattn_kernel,
        out_shape=jax.ShapeDtypeStruct((t, d), BF16),
        grid=(batch * nq, XATTN_HEADS),
        in_specs=[pl.BlockSpec((tq, d), lambda i, h: (i, 0)),
                  pl.BlockSpec((tq, LANES), lambda i, h: (i, 0)),
                  pl.BlockSpec((d, hd), lambda i, h: (0, h)),
                  pl.BlockSpec((mem, hd), lambda i, h: (i // nq, h)),
                  pl.BlockSpec((mem, hd), lambda i, h: (i // nq, h))],
        out_specs=pl.BlockSpec((tq, hd), lambda i, h: (i, h)),
        compiler_params=pltpu.CompilerParams(
            dimension_semantics=("parallel", "parallel"),
            vmem_limit_bytes=_vmem_limit(nbytes)),
        name="cross_attention",
    )(a, ssq, wq, k, v)


def kernel(x, mem, norm_mix_g, w_in, gla_gate_w2, gla_gate_b, gla_norm_g, mlstm_conv_w, mlstm_conv_b, mlstm_igate_b, mlstm_fgate_b, mlstm_norm_g, w_out, norm_cross_g, norm_mem_g, wq_c, wk_c, wv_c, wo_c, norm_ffn_g, w_gate, w_up, w_down, norm_final_g):
    batch, seq, d = x.shape
    mem_tokens = mem.shape[1]
    t = batch * seq
    depth = w_in.shape[0]
    gla_cols = 2 * GLA_HEADS * GLA_DK + 2 * GLA_HEADS * GLA_DV
    ml_cols = 2 * MLSTM_HEADS * MLSTM_DK + 2 * MLSTM_HEADS * MLSTM_DV
    ml_start = gla_cols + GLA_GATE_RANK

    h = x.reshape(t, d)
    mem2 = mem.reshape(batch * mem_tokens, d)
    for l in range(depth):
        w_in16 = w_in[l].astype(BF16)
        w_ml = w_in16[:, ml_start:ml_start + ml_cols]
        w_small = jnp.concatenate(
            [w_in[l][:, gla_cols:ml_start], w_in[l][:, ml_start + ml_cols:]], axis=1)
        w_small = jnp.pad(w_small, ((0, 0), (0, SMALL_COLS - w_small.shape[1]))).astype(BF16)
        w2 = gla_gate_w2[l].reshape(GLA_GATE_RANK, GLA_HEADS, GLA_DK).transpose(1, 0, 2)
        w2 = jnp.pad(w2, ((0, 0), (0, SMALL_COLS - GLA_GATE_RANK), (0, 0)))
        gate_b = gla_gate_b[l].reshape(GLA_HEADS, 1, GLA_DK)
        gla_g = gla_norm_g[l].reshape(GLA_HEADS, 1, GLA_DV)
        ml_g = mlstm_norm_g[l].reshape(MLSTM_HEADS, 1, MLSTM_DV)
        gate_bias = jnp.concatenate([mlstm_igate_b[l], mlstm_fgate_b[l]]).astype(F32)
        conv_b = mlstm_conv_b[l].reshape(1, -1)

        n1 = _rmsnorm(h, norm_mix_g[l], BF16)
        proj_gla = _matmul([n1], w_in16, BF16, n_cols=gla_cols, tm=1024, tn=1024, name="in_proj_gla")
        proj_ml = _matmul([n1], w_ml, BF16, tm=1024, tn=1024, name="in_proj_mlstm")
        small = _matmul([n1], w_small, F32, tm=1024, tn=SMALL_COLS, name="in_proj_small")
        ig_t = small[:, IG_COL:IG_COL + 2 * MLSTM_HEADS].reshape(
            t // CHUNK, CHUNK, 2 * MLSTM_HEADS).transpose(0, 2, 1)
        o_gla, o_ml = _mixer_scan(
            _gla(proj_gla, small, w2, gate_b, gla_g, batch=batch, seq=seq),
            _mlstm(proj_ml, small, ig_t, gate_bias, mlstm_conv_w[l], conv_b, ml_g, batch=batch, seq=seq),
            batch=batch, seq=seq)
        h, hb, ssq = _matmul([o_gla, o_ml], w_out[l].astype(BF16), F32, resid=h,
                             next_gain=norm_cross_g[l], tm=1024, tn=512, name="out_proj")

        mem_n = _rmsnorm(mem2, norm_mem_g[l], BF16)
        k_mem, v_mem = _short_proj(mem_n, wk_c[l], wv_c[l], tn=512)
        attn = _xattn(hb, ssq, wq_c[l].astype(BF16), k_mem, v_mem, batch=batch, seq=seq, tq=1024)
        h, hb, ssq = _matmul([attn], wo_c[l].astype(BF16), F32, resid=h,
                             next_gain=norm_ffn_g[l], tm=1024, tn=512, name="xattn_o")

        hid = _swiglu_up(hb, ssq, w_gate[l], w_up[l], tm=2048, tn=FFN_TN)
        h = _matmul([hid], w_down[l].astype(BF16), F32, resid=h, w_outer=True, tm=512, tn=512,
                    name="ffn_down")

    y = _rmsnorm(h, norm_final_g, x.dtype)
    return y.reshape(batch, seq, d)
```

```python
import functools
import math

import jax
import jax.numpy as jnp
import numpy as np
from jax import lax
from jax.experimental import pallas as pl
from jax.experimental.pallas import tpu as pltpu

F32 = jnp.float32
BF16 = jnp.bfloat16

D_MODEL = 4096
GROUP_WIDTH = D_MODEL // 2
GLA_HEADS = 4
GLA_DV = GROUP_WIDTH // GLA_HEADS
GLA_DK = GLA_DV // 2
GLA_GATE_RANK = 16
GLA_TAU = 16.0
MLSTM_HEADS = 4
MLSTM_DV = GROUP_WIDTH // MLSTM_HEADS
MLSTM_DK = MLSTM_DV // 2
CONV_WIDTH = 4
XATTN_HEADS = 4
XATTN_HEAD_DIM = D_MODEL // XATTN_HEADS
FFN_HIDDEN = 256 * math.ceil(8 * D_MODEL / (3 * 256))
EPS = 1e-6

V7X_VMEM_LIMIT_BYTES = 56 * 1024 * 1024
LANES = 128
SUBLANES = 8

CHUNK = 64
GLA_LEVELS = (32, 16, 8, 4, 2, 1)
LOG2_E = math.log2(math.e)
SEQ_BLOCK = 1024
NORM_ROWS = 512
FFN_TN = 256
XATTN_ROWS = 256
SWIGLU_PASS_ROWS = 1024
SMALL_COLS = LANES
IG_COL = GLA_GATE_RANK
FG_COL = GLA_GATE_RANK + MLSTM_HEADS


def _vmem_limit(nbytes):
    return int(min(V7X_VMEM_LIMIT_BYTES, max(32 * 1024 * 1024, nbytes * 5 // 4 + (4 << 20))))


def _dot(a, b):
    return jnp.dot(a, b, preferred_element_type=F32)


def _dot_nt(a, b):
    return lax.dot_general(a, b, (((1,), (1,)), ((), ())), preferred_element_type=F32)


def _dot_tn(a, b):
    return lax.dot_general(a, b, (((0,), (0,)), ((), ())), preferred_element_type=F32)


def _split3(x):
    hi = x.astype(BF16)
    r = x - hi.astype(F32)
    mid = r.astype(BF16)
    lo = (r - mid.astype(F32)).astype(BF16)
    return hi, mid, lo


def _log_sigmoid(z):
    return jnp.minimum(z, 0.0) - jnp.log(1.0 + jnp.exp(-jnp.abs(z)))


def _sigmoid(z):
    return 1.0 / (1.0 + jnp.exp(-z))


def _rmsnorm_kernel(x_ref, g_ref, o_ref):
    x = x_ref[...].astype(F32)
    ms = jnp.mean(x * x, axis=-1, keepdims=True)
    o_ref[...] = (x * lax.rsqrt(ms + EPS) * g_ref[...]).astype(o_ref.dtype)


def _rmsnorm(x, g, out_dtype):
    rows, d = x.shape
    tr = min(NORM_ROWS, rows)
    nbytes = 2 * tr * d * (x.dtype.itemsize + jnp.dtype(out_dtype).itemsize) + 3 * tr * d * 4
    return pl.pallas_call(
        _rmsnorm_kernel,
        out_shape=jax.ShapeDtypeStruct((rows, d), out_dtype),
        grid=(rows // tr,),
        in_specs=[pl.BlockSpec((tr, d), lambda i: (i, 0)),
                  pl.BlockSpec((1, d), lambda i: (0, 0))],
        out_specs=pl.BlockSpec((tr, d), lambda i: (i, 0)),
        compiler_params=pltpu.CompilerParams(
            dimension_semantics=("parallel",), vmem_limit_bytes=_vmem_limit(nbytes)),
        name="rmsnorm",
    )(x, g.reshape(1, d).astype(F32))


def _mm_kernel(*refs, n_a, has_resid, has_norm, has_narrow):
    a_refs = refs[:n_a]
    w_ref = refs[n_a]
    pos = n_a + 1
    r_ref = refs[pos] if has_resid else None
    pos += int(has_resid)
    g_ref = refs[pos] if has_norm else None
    pos += int(has_norm)
    wn_ref = refs[pos] if has_narrow else None
    pos += int(has_narrow)
    o_ref = refs[pos]

    if has_narrow:
        @pl.when(pl.program_id(1) == 0)
        def _():
            refs[-1][...] = _dot(a_refs[0][...], wn_ref[...])

    kp = a_refs[0].shape[1]
    acc = _dot(a_refs[0][...], w_ref[0:kp, :])
    for p in range(1, n_a):
        acc = acc + _dot(a_refs[p][...], w_ref[p * kp:(p + 1) * kp, :])
    if has_resid:
        acc = r_ref[...] + acc
    o_ref[...] = acc.astype(o_ref.dtype)

    if has_norm:
        hb_ref, ssq_ref = refs[pos + 1], refs[pos + 2]
        hb_ref[...] = (acc * g_ref[...]).astype(hb_ref.dtype)
        part = jnp.broadcast_to(jnp.sum(acc * acc, axis=1, keepdims=True), ssq_ref.shape)
        j = pl.program_id(1)

        @pl.when(j == 0)
        def _():
            ssq_ref[...] = part

        @pl.when(j > 0)
        def _():
            ssq_ref[...] += part


def _matmul(a_list, w, out_dtype, *, resid=None, next_gain=None, w_narrow=None, n_cols=None, w_outer=False,
            tm, tn, name):
    m, kp = a_list[0].shape
    kdim = w.shape[0]
    n = w.shape[1] if n_cols is None else n_cols
    n_a = len(a_list)
    assert m % tm == 0 and n % tn == 0 and kdim == n_a * kp
    assert not (w_outer and (next_gain is not None or w_narrow is not None))
    assert w_narrow is None or (n_a == 1 and w_narrow.shape == (kdim, LANES))

    def spec(shape, f):
        return pl.BlockSpec(shape, (lambda j, i: f(i, j)) if w_outer else f)

    in_specs = ([spec((tm, kp), lambda i, j: (i, 0)) for _ in a_list]
                + [spec((kdim, tn), lambda i, j: (0, j))])
    args = list(a_list) + [w]
    out_shape = [jax.ShapeDtypeStruct((m, n), out_dtype)]
    out_specs = [spec((tm, tn), lambda i, j: (i, j))]
    nbytes = 2 * (tm * kdim + kdim * tn) * 2 + 2 * tm * tn * jnp.dtype(out_dtype).itemsize + 2 * tm * tn * 4
    if resid is not None:
        in_specs.append(spec((tm, tn), lambda i, j: (i, j)))
        args.append(resid)
        nbytes += 2 * tm * tn * 4
    if next_gain is not None:
        in_specs.append(spec((1, tn), lambda i, j: (0, j)))
        args.append(next_gain.reshape(1, n).astype(F32))
        out_shape += [jax.ShapeDtypeStruct((m, n), BF16), jax.ShapeDtypeStruct((m, LANES), F32)]
        out_specs += [spec((tm, tn), lambda i, j: (i, j)),
                      spec((tm, LANES), lambda i, j: (i, 0))]
        nbytes += 2 * tm * tn * 2 + 2 * tm * LANES * 4 + tm * tn * 4
    if w_narrow is not None:
        in_specs.append(spec((kdim, LANES), lambda i, j: (0, 0)))
        args.append(w_narrow)
        out_shape.append(jax.ShapeDtypeStruct((m, LANES), F32))
        out_specs.append(spec((tm, LANES), lambda i, j: (i, 0)))
        nbytes += 2 * kdim * LANES * 2 + 2 * tm * LANES * 4
    outs = pl.pallas_call(
        functools.partial(_mm_kernel, n_a=n_a, has_resid=resid is not None,
                          has_norm=next_gain is not None, has_narrow=w_narrow is not None),
        out_shape=out_shape,
        grid=(n // tn, m // tm) if w_outer else (m // tm, n // tn),
        in_specs=in_specs,
        out_specs=out_specs,
        compiler_params=pltpu.CompilerParams(
            dimension_semantics=("parallel", "arbitrary"),
            vmem_limit_bytes=_vmem_limit(nbytes)),
        name=name,
    )(*args)
    return outs if len(outs) > 1 else outs[0]


def _short_proj_kernel(a_ref, wk_ref, wv_ref, k_ref, v_ref):
    a = a_ref[...]
    k_ref[...] = _dot(a, wk_ref[...].astype(BF16)).astype(k_ref.dtype)
    v_ref[...] = _dot(a, wv_ref[...].astype(BF16)).astype(v_ref.dtype)


def _short_proj(a, wk, wv, *, tn):
    m, kdim = a.shape
    n = wk.shape[1]
    assert n % tn == 0
    nbytes = 2 * m * kdim * 2 + 2 * kdim * tn * (2 * 4 + 2) + 4 * m * tn * 2 + 2 * m * tn * 4
    return pl.pallas_call(
        _short_proj_kernel,
        out_shape=[jax.ShapeDtypeStruct((m, n), BF16)] * 2,
        grid=(n // tn,),
        in_specs=[pl.BlockSpec((m, kdim), lambda j: (0, 0)),
                  pl.BlockSpec((kdim, tn), lambda j: (0, j)),
                  pl.BlockSpec((kdim, tn), lambda j: (0, j))],
        out_specs=[pl.BlockSpec((m, tn), lambda j: (0, j))] * 2,
        compiler_params=pltpu.CompilerParams(
            dimension_semantics=("parallel",), vmem_limit_bytes=_vmem_limit(nbytes)),
        name="xattn_kv",
    )(a, wk, wv)


def _swiglu_up_kernel(a_ref, ssq_ref, wg_ref, wu_ref, o_ref):
    wg = wg_ref[...].astype(BF16)
    wu = wu_ref[...].astype(BF16)
    kdim = a_ref.shape[1]
    for p in range(a_ref.shape[0] // SWIGLU_PASS_ROWS):
        rows = slice(p * SWIGLU_PASS_ROWS, (p + 1) * SWIGLU_PASS_ROWS)
        a = a_ref[rows, :]
        r = lax.rsqrt(ssq_ref[rows, 0:1] * (1.0 / kdim) + EPS)
        gate = r * _dot(a, wg)
        up = r * _dot(a, wu)
        o_ref[rows, :] = (gate * _sigmoid(gate) * up).astype(o_ref.dtype)


def _swiglu_up(a, ssq, wg, wu, *, tm, tn):
    m, kdim = a.shape
    n = wg.shape[1]
    assert m % tm == 0 and n % tn == 0 and tm % SWIGLU_PASS_ROWS == 0
    nbytes = (tm * kdim * 2 + 2 * kdim * tn * (2 * 4 + 2) + 2 * tm * tn * 2 + 2 * tm * LANES * 4
              + 3 * SWIGLU_PASS_ROWS * tn * 4)
    return pl.pallas_call(
        _swiglu_up_kernel,
        out_shape=jax.ShapeDtypeStruct((m, n), BF16),
        grid=(m // tm, n // tn),
        in_specs=[pl.BlockSpec((tm, kdim), lambda i, j: (i, 0), pipeline_mode=pl.Buffered(1)),
                  pl.BlockSpec((tm, LANES), lambda i, j: (i, 0)),
                  pl.BlockSpec((kdim, tn), lambda i, j: (0, j)),
                  pl.BlockSpec((kdim, tn), lambda i, j: (0, j))],
        out_specs=pl.BlockSpec((tm, tn), lambda i, j: (i, j)),
        compiler_params=pltpu.CompilerParams(
            dimension_semantics=("parallel", "arbitrary"),
            vmem_limit_bytes=_vmem_limit(nbytes)),
        name="swiglu_up",
    )(a, ssq, wg, wu)


def _gla_tables():
    c = CHUNK
    row = np.arange(c)[:, None]
    u = np.arange(c)[None, :]
    sel = [u <= row, u > row]
    masks = [row == u]
    for bsz in GLA_LEVELS:
        r = (row // (2 * bsz)) * (2 * bsz) + bsz - 1
        sel.append(np.where(row <= r, (u > row) & (u <= r), (u > r) & (u <= row)))
        same_pair = (row // (2 * bsz)) == (u // (2 * bsz))
        masks.append(same_pair & ((row % (2 * bsz)) >= bsz) & ((u % (2 * bsz)) < bsz))
    sel = np.concatenate(sel, axis=0).astype(np.float32)
    return (jnp.asarray(np.concatenate([sel] * 3, axis=1), BF16),
            jnp.asarray(np.stack(masks).astype(np.float32)))


def _gla_stream(q_ref, k_ref, v_ref, g_ref, sm_ref, w2_ref, gb_ref, ng_ref, sel_ref, mask_ref,
                o_ref, st_ref):
    c_len = CHUNK
    norm_g = ng_ref[...]
    w2 = w2_ref[...].astype(BF16)
    row_k = lax.broadcasted_iota(jnp.int32, (c_len, GLA_DK), 0)
    state = [st_ref[...]]
    tmp = [dict() for _ in range(SEQ_BLOCK // c_len)]

    def rows_of(c):
        return slice(c * c_len, (c + 1) * c_len)

    def gates(c):
        z = _dot(sm_ref[rows_of(c), :].astype(BF16), w2) + gb_ref[...]
        tmp[c]["la3"] = jnp.concatenate(_split3(_log_sigmoid(z) * (LOG2_E / GLA_TAU)), axis=0)

    def decay_table(c):
        tmp[c]["e"] = jnp.exp2(_dot(sel_ref[...], tmp[c].pop("la3")))

    def scores(c):
        rows = rows_of(c)
        e = tmp[c].pop("e")
        q = q_ref[rows, :].astype(F32) * (GLA_DK ** -0.5)
        k = k_ref[rows, :].astype(F32)
        k_dec = (k * e[c_len:2 * c_len, :]).astype(BF16)
        attn = _dot_nt(q.astype(BF16), k.astype(BF16)) * mask_ref[0]
        for lv, bsz in enumerate(GLA_LEVELS):
            in_right = (row_k & bsz) != 0
            x = (jnp.where(in_right, q, k) * e[(2 + lv) * c_len:(3 + lv) * c_len, :]).astype(BF16)
            attn = attn + _dot_nt(x, x) * mask_ref[1 + lv]
        tmp[c].update(q_dec=(q * e[0:c_len, :]).astype(BF16), k_dec=k_dec,
                      decay=e[c_len - 1:c_len, :],
                      intra=_dot(attn.astype(BF16), v_ref[rows, :]))

    def recurrence(c):
        tmp[c]["st_in"] = state[0].astype(BF16)
        state[0] = state[0] * tmp[c].pop("decay") + _dot_tn(v_ref[rows_of(c), :], tmp[c].pop("k_dec"))

    def output(c):
        rows = rows_of(c)
        out = tmp[c].pop("intra") + _dot_nt(tmp[c].pop("q_dec"), tmp[c].pop("st_in"))
        ms = jnp.mean(out * out, axis=-1, keepdims=True)
        y = out * lax.rsqrt(ms + EPS) * norm_g
        gg = g_ref[rows, :].astype(F32)
        o_ref[rows, :] = (y * (gg * _sigmoid(gg))).astype(o_ref.dtype)

    def finish():
        st_ref[...] = state[0]

    return [gates, decay_table, scores, recurrence, output], finish


def _gla(proj, small, w2, gate_b, norm_g, *, batch, seq):
    decay_sel, pair_mask = _gla_tables()
    t = proj.shape[0]
    lb = SEQ_BLOCK
    nsb = seq // lb
    h_ = GLA_HEADS
    qb = GLA_DK
    vb = GLA_DV

    def rows(b, h, i):
        return b * nsb + i

    in_specs = [
        pl.BlockSpec((lb, qb), lambda b, h, i: (rows(b, h, i), h)),
        pl.BlockSpec((lb, qb), lambda b, h, i: (rows(b, h, i), h_ + h)),
        pl.BlockSpec((lb, vb), lambda b, h, i: (rows(b, h, i), (2 * h_ * qb) // vb + h)),
        pl.BlockSpec((lb, vb), lambda b, h, i: (rows(b, h, i), (2 * h_ * qb) // vb + h_ + h)),
        pl.BlockSpec((lb, SMALL_COLS), lambda b, h, i: (rows(b, h, i), 0)),
        pl.BlockSpec((None, SMALL_COLS, qb), lambda b, h, i: (h, 0, 0)),
        pl.BlockSpec((None, 1, qb), lambda b, h, i: (h, 0, 0)),
        pl.BlockSpec((None, 1, vb), lambda b, h, i: (h, 0, 0)),
        pl.BlockSpec(decay_sel.shape, lambda b, h, i: (0, 0)),
        pl.BlockSpec(pair_mask.shape, lambda b, h, i: (0, 0, 0)),
    ]
    return dict(
        in_specs=in_specs,
        args=[proj, proj, proj, proj, small, w2, gate_b, norm_g, decay_sel, pair_mask],
        out_shape=jax.ShapeDtypeStruct((t, h_ * vb), BF16),
        out_spec=pl.BlockSpec((lb, vb), lambda b, h, i: (rows(b, h, i), h)),
        scratch=[pltpu.VMEM((vb, qb), F32)])


def _mlstm_tables():
    lb, c = CHUNK, CHUNK
    t = np.arange(lb)[:, None]
    u = np.arange(lb)[None, :]
    shift = np.concatenate([(u == t - d) for d in range(1, CONV_WIDTH)], axis=0).astype(np.float32)
    tc = np.arange(c)[:, None]
    uc = np.arange(c)[None, :]
    cum = np.concatenate([uc <= tc, uc > tc], axis=0).astype(np.float32)
    col = np.concatenate([tc > uc, np.ones((c, c), bool)], axis=1).astype(np.float32)
    return (jnp.asarray(shift, BF16), jnp.asarray(np.concatenate([cum] * 3, axis=1), BF16),
            jnp.asarray(col))


def _mlstm_stream(gbias_ref, q_ref, k_ref, v_ref, og_ref, sm_ref, igt_ref, cwq_ref, cwk_ref,
                  cbq_ref, cbk_ref, ng_ref, shift_ref, cum_ref, col_ref, o_ref,
                  ct_ref, n_ref, m_ref, hist_ref):
    c_len = CHUNK
    lb = SEQ_BLOCK
    taps = CONV_WIDTH
    h = pl.program_id(1)

    w = jnp.concatenate([cwq_ref[...], cwk_ref[...]], axis=1)
    bias = jnp.concatenate([cbq_ref[...], cbk_ref[...]], axis=1)
    ib = gbias_ref[h]
    fb = gbias_ref[MLSTM_HEADS + h]
    lane_s = lax.broadcasted_iota(jnp.int32, (c_len, SMALL_COLS), 1)
    norm_g = ng_ref[...]
    row_c = lax.broadcasted_iota(jnp.int32, (c_len, c_len), 0)
    col_c = lax.broadcasted_iota(jnp.int32, (c_len, c_len), 1)
    causal = row_c >= col_c

    state = [ct_ref[...], n_ref[...], m_ref[...]]
    tmp = [dict() for _ in range(lb // c_len)]

    def rows_of(c):
        return slice(c * c_len, (c + 1) * c_len)

    def prepare(c):
        rows = rows_of(c)
        x16 = jnp.concatenate([q_ref[rows, :], k_ref[rows, :]], axis=1)
        x = x16.astype(F32)
        shifted = _dot(shift_ref[...], x16)
        tail = hist_ref[...] if c == 0 else tmp[c - 1]["x_tail"]
        head = jnp.concatenate([tail, x[0:SUBLANES, :]], axis=0)
        y = bias + w[taps - 1:taps, :] * x
        for d in range(1, taps):
            sh = jnp.concatenate([head[SUBLANES - d:2 * SUBLANES - d, :],
                                  shifted[(d - 1) * c_len + SUBLANES:d * c_len, :]], axis=0)
            y = y + w[taps - 1 - d:taps - d, :] * sh
        act = y * _sigmoid(y)
        sm = sm_ref[rows, :]
        i_col = jnp.sum(jnp.where(lane_s == IG_COL + h, sm, 0.0), axis=1, keepdims=True) + ib
        f_col = jnp.sum(jnp.where(lane_s == FG_COL + h, sm, 0.0), axis=1, keepdims=True) + fb
        tmp[c].update(x_tail=x[c_len - SUBLANES:c_len, :], q=act[:, 0:MLSTM_DK],
                      k=act[:, MLSTM_DK:2 * MLSTM_DK] * (MLSTM_DK ** -0.5),
                      i_col=i_col, lf_col=_log_sigmoid(f_col))

    def gate_sums(c):
        d = tmp[c]
        g_hi, g_mid, g_lo = _split3(d.pop("lf_col") * col_ref[...])
        sums = _dot(cum_ref[...], jnp.concatenate([g_hi, g_mid, g_lo], axis=0))
        i_row = igt_ref[c, pl.ds(h, 1), :] + ib
        log_d = jnp.where(causal, sums[0:c_len, 0:c_len] + i_row, -jnp.inf)
        bb_col = sums[0:c_len, c_len:c_len + 1]
        log_w = sums[c_len:2 * c_len, c_len:c_len + 1] + d.pop("i_col")
        d.update(log_d=log_d, bb_col=bb_col, log_w=log_w,
                 d_max=jnp.max(log_d, axis=1, keepdims=True),
                 w_max=jnp.max(log_w, axis=0, keepdims=True),
                 qk=_dot_nt(d["q"].astype(BF16), d["k"].astype(BF16)))

    def stabiliser(c):
        d = tmp[c]
        m_prev = state[2]
        bb_col = d.pop("bb_col")
        b_last = bb_col[c_len - 1:c_len, :]
        inter = bb_col + m_prev
        m = jnp.maximum(inter, d.pop("d_max"))
        m_new = jnp.maximum(b_last + m_prev, d.pop("w_max"))
        d.update(m=m, g_inter=jnp.exp(inter - m), w_col=jnp.exp(d.pop("log_w") - m_new),
                 decay=jnp.exp(b_last + m_prev - m_new))
        state[2] = m_new

    def scores(c):
        rows = rows_of(c)
        d = tmp[c]
        s = d.pop("qk") * jnp.exp(d.pop("log_d") - d["m"])
        kw = d.pop("k") * d.pop("w_col")
        d.update(s_sum=jnp.sum(s, axis=1, keepdims=True),
                 intra=_dot(s.astype(BF16), v_ref[rows, :]),
                 kw=kw.astype(BF16),
                 k_sum=jnp.sum(kw, axis=0, keepdims=True))

    def recurrence(c):
        d = tmp[c]
        decay = d.pop("decay")
        d.update(ct_in=state[0].astype(BF16), n_in=state[1])
        state[0] = decay * state[0] + _dot_tn(v_ref[rows_of(c), :], d.pop("kw"))
        state[1] = decay * state[1] + d.pop("k_sum")

    def output(c):
        rows = rows_of(c)
        d = tmp[c]
        q = d.pop("q")
        g_inter = d.pop("g_inter")
        num = d.pop("intra") + g_inter * _dot_nt(q.astype(BF16), d.pop("ct_in"))
        den = d.pop("s_sum") + g_inter * jnp.sum(q * d.pop("n_in"), axis=1, keepdims=True)
        hid = num / jnp.maximum(jnp.abs(den), jnp.exp(-d.pop("m")))
        ms = jnp.mean(hid * hid, axis=-1, keepdims=True)
        y_out = hid * lax.rsqrt(ms + EPS) * norm_g
        og = og_ref[rows, :].astype(F32)
        o_ref[rows, :] = (_sigmoid(og) * y_out).astype(o_ref.dtype)

    def finish():
        ct_ref[...], n_ref[...], m_ref[...] = state
        hist_ref[...] = tmp[-1]["x_tail"]

    return [prepare, gate_sums, stabiliser, scores, recurrence, output], finish


def _mlstm(proj, small, ig_t, gate_bias, conv_w, conv_b, norm_g, *, batch, seq):
    shift, cum_sel, col_sel = _mlstm_tables()
    t = proj.shape[0]
    lb = SEQ_BLOCK
    nsb = seq // lb
    h_ = MLSTM_HEADS
    qb = MLSTM_DK
    vb = MLSTM_DV
    q0 = 0
    v0 = (2 * h_ * qb) // vb

    def rows(b, h, i):
        return b * nsb + i

    in_specs = [
        pl.BlockSpec(memory_space=pltpu.SMEM),
        pl.BlockSpec((lb, qb), lambda b, h, i: (rows(b, h, i), q0 + h)),
        pl.BlockSpec((lb, qb), lambda b, h, i: (rows(b, h, i), q0 + h_ + h)),
        pl.BlockSpec((lb, vb), lambda b, h, i: (rows(b, h, i), v0 + h)),
        pl.BlockSpec((lb, vb), lambda b, h, i: (rows(b, h, i), v0 + h_ + h)),
        pl.BlockSpec((lb, SMALL_COLS), lambda b, h, i: (rows(b, h, i), 0)),
        pl.BlockSpec((lb // CHUNK, 2 * h_, CHUNK), lambda b, h, i: (rows(b, h, i), 0, 0)),
        pl.BlockSpec((CONV_WIDTH, qb), lambda b, h, i: (0, h)),
        pl.BlockSpec((CONV_WIDTH, qb), lambda b, h, i: (0, h_ + h)),
        pl.BlockSpec((1, qb), lambda b, h, i: (0, h)),
        pl.BlockSpec((1, qb), lambda b, h, i: (0, h_ + h)),
        pl.BlockSpec((None, 1, vb), lambda b, h, i: (h, 0, 0)),
        pl.BlockSpec(shift.shape, lambda b, h, i: (0, 0)),
        pl.BlockSpec(cum_sel.shape, lambda b, h, i: (0, 0)),
        pl.BlockSpec(col_sel.shape, lambda b, h, i: (0, 0)),
    ]
    scratch = [
        pltpu.VMEM((vb, qb), F32),
        pltpu.VMEM((1, qb), F32),
        pltpu.VMEM((1, 1), F32),
        pltpu.VMEM((SUBLANES, 2 * qb), F32),
    ]
    return dict(
        in_specs=in_specs,
        args=[gate_bias, proj, proj, proj, proj, small, ig_t, conv_w, conv_w, conv_b, conv_b, norm_g,
              shift, cum_sel, col_sel],
        out_shape=jax.ShapeDtypeStruct((t, h_ * vb), BF16),
        out_spec=pl.BlockSpec((lb, vb), lambda b, h, i: (rows(b, h, i), h)),
        scratch=scratch)


def _mixer_scan_kernel(*refs, n_gla_in, n_ml_in, n_gla_scratch):
    gla_in = refs[:n_gla_in]
    ml_in = refs[n_gla_in:n_gla_in + n_ml_in]
    o_gla, o_ml = refs[n_gla_in + n_ml_in:n_gla_in + n_ml_in + 2]
    scratch = refs[n_gla_in + n_ml_in + 2:]

    @pl.when(pl.program_id(2) == 0)
    def _():
        for ref in scratch:
            ref[...] = jnp.zeros_like(ref)

    gla_stages, gla_finish = _gla_stream(*gla_in, o_gla, *scratch[:n_gla_scratch])
    ml_stages, ml_finish = _mlstm_stream(*ml_in, o_ml, *scratch[n_gla_scratch:])
    n_chunks = SEQ_BLOCK // CHUNK
    lead = len(ml_stages) - len(gla_stages)
    for t in range(n_chunks + len(ml_stages) - 1):
        for k in reversed(range(len(ml_stages))):
            c = t - k
            if 0 <= c < n_chunks:
                ml_stages[k](c)
                if k >= lead:
                    gla_stages[k - lead](c)
    gla_finish()
    ml_finish()


def _mixer_scan(gla, ml, *, batch, seq):
    assert GLA_HEADS == MLSTM_HEADS
    outs = pl.pallas_call(
        functools.partial(_mixer_scan_kernel, n_gla_in=len(gla["args"]), n_ml_in=len(ml["args"]),
                          n_gla_scratch=len(gla["scratch"])),
        out_shape=[gla["out_shape"], ml["out_shape"]],
        grid=(batch, GLA_HEADS, seq // SEQ_BLOCK),
        in_specs=gla["in_specs"] + ml["in_specs"],
        out_specs=[gla["out_spec"], ml["out_spec"]],
        scratch_shapes=gla["scratch"] + ml["scratch"],
        compiler_params=pltpu.CompilerParams(
            dimension_semantics=("parallel", "parallel", "arbitrary"),
            vmem_limit_bytes=_vmem_limit(24 << 20)),
        name="mixer_scan",
    )(*gla["args"], *ml["args"])
    return outs[0], outs[1]


def _xattn_kernel(a_ref, ssq_ref, wq_ref, k_ref, v_ref, o_ref):
    a = a_ref[...]
    r = lax.rsqrt(ssq_ref[:, 0:1] * (1.0 / a.shape[1]) + EPS)
    q = (r * _dot(a, wq_ref[...])).astype(BF16)
    groups = [slice(g * XATTN_ROWS, (g + 1) * XATTN_ROWS) for g in range(a.shape[0] // XATTN_ROWS)]
    scores = [_dot_nt(q[rows, :], k_ref[...]) * (XATTN_HEAD_DIM ** -0.5) for rows in groups]
    probs = []
    for s in scores:
        p = jnp.exp(s - jnp.max(s, axis=-1, keepdims=True))
        probs.append((p / jnp.sum(p, axis=-1, keepdims=True)).astype(BF16))
    for rows, p in zip(groups, probs):
        o_ref[rows, :] = _dot(p, v_ref[...]).astype(o_ref.dtype)


def _xattn(a, ssq, wq, k, v, *, batch, seq, tq):
    t, d = a.shape
    mem = k.shape[0] // batch
    hd = XATTN_HEAD_DIM
    nq = seq // tq
    nbytes = 2 * (tq * d + d * hd + 2 * mem * hd + tq * hd) * 2 + 2 * tq * LANES * 4 + 3 * tq * hd * 4
    return pl.pallas_call(
        _xattn_kernel,
        out_shape=jax.ShapeDtypeStruct((t, d), BF16),
        grid=(batch * nq, XATTN_HEADS),
        in_specs=[pl.BlockSpec((tq, d), lambda i, h: (i, 0)),
                  pl.BlockSpec((tq, LANES), lambda i, h: (i, 0)),
                  pl.BlockSpec((d, hd), lambda i, h: (0, h)),
                  pl.BlockSpec((mem, hd), lambda i, h: (i // nq, h)),
                  pl.BlockSpec((mem, hd), lambda i, h: (i // nq, h))],
        out_specs=pl.BlockSpec((tq, hd), lambda i, h: (i, h)),
        compiler_params=pltpu.CompilerParams(
            dimension_semantics=("parallel", "parallel"),
            vmem_limit_bytes=_vmem_limit(nbytes)),
        name="cross_attention",
    )(a, ssq, wq, k, v)


def kernel(x, mem, norm_mix_g, w_in, gla_gate_w2, gla_gate_b, gla_norm_g, mlstm_conv_w, mlstm_conv_b, mlstm_igate_b, mlstm_fgate_b, mlstm_norm_g, w_out, norm_cross_g, norm_mem_g, wq_c, wk_c, wv_c, wo_c, norm_ffn_g, w_gate, w_up, w_down, norm_final_g):
    batch, seq, d = x.shape
    mem_tokens = mem.shape[1]
    t = batch * seq
    depth = w_in.shape[0]
    gla_cols = 2 * GLA_HEADS * GLA_DK + 2 * GLA_HEADS * GLA_DV
    ml_cols = 2 * MLSTM_HEADS * MLSTM_DK + 2 * MLSTM_HEADS * MLSTM_DV
    ml_start = gla_cols + GLA_GATE_RANK

    h = x.reshape(t, d)
    mem2 = mem.reshape(batch * mem_tokens, d)
    for l in range(depth):
        w_in16 = w_in[l].astype(BF16)
        w_ml = w_in16[:, ml_start:ml_start + ml_cols]
        w_small = jnp.concatenate(
            [w_in[l][:, gla_cols:ml_start], w_in[l][:, ml_start + ml_cols:]], axis=1)
        w_small = jnp.pad(w_small, ((0, 0), (0, SMALL_COLS - w_small.shape[1]))).astype(BF16)
        w2 = gla_gate_w2[l].reshape(GLA_GATE_RANK, GLA_HEADS, GLA_DK).transpose(1, 0, 2)
        w2 = jnp.pad(w2, ((0, 0), (0, SMALL_COLS - GLA_GATE_RANK), (0, 0)))
        gate_b = gla_gate_b[l].reshape(GLA_HEADS, 1, GLA_DK)
        gla_g = gla_norm_g[l].reshape(GLA_HEADS, 1, GLA_DV)
        ml_g = mlstm_norm_g[l].reshape(MLSTM_HEADS, 1, MLSTM_DV)
        gate_bias = jnp.concatenate([mlstm_igate_b[l], mlstm_fgate_b[l]]).astype(F32)
        conv_b = mlstm_conv_b[l].reshape(1, -1)

        n1 = _rmsnorm(h, norm_mix_g[l], BF16)
        proj_gla = _matmul([n1], w_in16, BF16, n_cols=gla_cols, tm=1024, tn=1024, name="in_proj_gla")
        proj_ml, small = _matmul([n1], w_ml, BF16, w_narrow=w_small, tm=1024, tn=1024, name="in_proj_mlstm")
        ig_t = small[:, IG_COL:IG_COL + 2 * MLSTM_HEADS].reshape(
            t // CHUNK, CHUNK, 2 * MLSTM_HEADS).transpose(0, 2, 1)
        o_gla, o_ml = _mixer_scan(
            _gla(proj_gla, small, w2, gate_b, gla_g, batch=batch, seq=seq),
            _mlstm(proj_ml, small, ig_t, gate_bias, mlstm_conv_w[l], conv_b, ml_g, batch=batch, seq=seq),
            batch=batch, seq=seq)
        h, hb, ssq = _matmul([o_gla, o_ml], w_out[l].astype(BF16), F32, resid=h,
                             next_gain=norm_cross_g[l], tm=1024, tn=512, name="out_proj")

        mem_n = _rmsnorm(mem2, norm_mem_g[l], BF16)
        k_mem, v_mem = _short_proj(mem_n, wk_c[l], wv_c[l], tn=512)
        attn = _xattn(hb, ssq, wq_c[l].astype(BF16), k_mem, v_mem, batch=batch, seq=seq, tq=1024)
        h, hb, ssq = _matmul([attn], wo_c[l].astype(BF16), F32, resid=h,
                             next_gain=norm_ffn_g[l], tm=1024, tn=512, name="xattn_o")

        hid = _swiglu_up(hb, ssq, w_gate[l], w_up[l], tm=2048, tn=FFN_TN)
        h = _matmul([hid], w_down[l].astype(BF16), F32, resid=h, w_outer=True, tm=512, tn=512,
                    name="ffn_down")

    y = _rmsnorm(h, norm_final_g, x.dtype)
    return y.reshape(batch, seq, d)
```

```python
import functools
import math

import jax
import jax.numpy as jnp
import numpy as np
from jax import lax
from jax.experimental import pallas as pl
from jax.experimental.pallas import tpu as pltpu

F32 = jnp.float32
BF16 = jnp.bfloat16

D_MODEL = 4096
GROUP_WIDTH = D_MODEL // 2
GLA_HEADS = 4
GLA_DV = GROUP_WIDTH // GLA_HEADS
GLA_DK = GLA_DV // 2
GLA_GATE_RANK = 16
GLA_TAU = 16.0
MLSTM_HEADS = 4
MLSTM_DV = GROUP_WIDTH // MLSTM_HEADS
MLSTM_DK = MLSTM_DV // 2
CONV_WIDTH = 4
XATTN_HEADS = 4
XATTN_HEAD_DIM = D_MODEL // XATTN_HEADS
FFN_HIDDEN = 256 * math.ceil(8 * D_MODEL / (3 * 256))
EPS = 1e-6

V7X_VMEM_LIMIT_BYTES = 56 * 1024 * 1024
DEFAULT_SCOPED_VMEM_BYTES = 32 * 1024 * 1024
VMEM_UNMODELLED_BYTES = 8 * 1024 * 1024
LANES = 128
SUBLANES = 8

CHUNK = 64
GLA_LEVELS = (32, 16, 8, 4, 2, 1)
LOG2_E = math.log2(math.e)
SEQ_BLOCK = 1024
SCAN_STAGES = 6
NORM_ROWS = 512
FFN_TN = 256
XATTN_ROWS = 256
SWIGLU_PASS_ROWS = 1024
SMALL_COLS = LANES
IG_COL = GLA_GATE_RANK
FG_COL = GLA_GATE_RANK + MLSTM_HEADS


def _vmem_limit(nbytes):
    return int(min(V7X_VMEM_LIMIT_BYTES, max(DEFAULT_SCOPED_VMEM_BYTES, nbytes + VMEM_UNMODELLED_BYTES)))


def _block_bytes(specs, arrays):
    total = 0
    for spec, arr in zip(specs, arrays):
        if spec.block_shape is not None:
            total += math.prod(d for d in spec.block_shape if d is not None) * jnp.dtype(arr.dtype).itemsize
    return total


def _dot(a, b):
    return jnp.dot(a, b, preferred_element_type=F32)


def _dot_nt(a, b):
    return lax.dot_general(a, b, (((1,), (1,)), ((), ())), preferred_element_type=F32)


def _dot_tn(a, b):
    return lax.dot_general(a, b, (((0,), (0,)), ((), ())), preferred_element_type=F32)


def _split3(x):
    hi = x.astype(BF16)
    r = x - hi.astype(F32)
    mid = r.astype(BF16)
    lo = (r - mid.astype(F32)).astype(BF16)
    return hi, mid, lo


def _log_sigmoid(z):
    return jnp.minimum(z, 0.0) - jnp.log(1.0 + jnp.exp(-jnp.abs(z)))


def _sigmoid(z):
    return 1.0 / (1.0 + jnp.exp(-z))


def _rmsnorm_kernel(x_ref, g_ref, o_ref):
    x = x_ref[...].astype(F32)
    ms = jnp.mean(x * x, axis=-1, keepdims=True)
    o_ref[...] = (x * lax.rsqrt(ms + EPS) * g_ref[...]).astype(o_ref.dtype)


def _rmsnorm(x, g, out_dtype):
    rows, d = x.shape
    tr = min(NORM_ROWS, rows)
    nbytes = 2 * tr * d * (x.dtype.itemsize + jnp.dtype(out_dtype).itemsize) + 3 * tr * d * 4
    return pl.pallas_call(
        _rmsnorm_kernel,
        out_shape=jax.ShapeDtypeStruct((rows, d), out_dtype),
        grid=(rows // tr,),
        in_specs=[pl.BlockSpec((tr, d), lambda i: (i, 0)),
                  pl.BlockSpec((1, d), lambda i: (0, 0))],
        out_specs=pl.BlockSpec((tr, d), lambda i: (i, 0)),
        compiler_params=pltpu.CompilerParams(
            dimension_semantics=("parallel",), vmem_limit_bytes=_vmem_limit(nbytes)),
        name="rmsnorm",
    )(x, g.reshape(1, d).astype(F32))


def _mm_kernel(*refs, n_a, has_resid, has_norm, has_narrow):
    a_refs = refs[:n_a]
    w_ref = refs[n_a]
    pos = n_a + 1
    r_ref = refs[pos] if has_resid else None
    pos += int(has_resid)
    g_ref = refs[pos] if has_norm else None
    pos += int(has_norm)
    wn_ref = refs[pos] if has_narrow else None
    pos += int(has_narrow)
    o_ref = refs[pos]

    if has_narrow:
        @pl.when(pl.program_id(1) == 0)
        def _():
            refs[-1][...] = _dot(a_refs[0][...], wn_ref[...])

    kp = a_refs[0].shape[1]
    acc = _dot(a_refs[0][...], w_ref[0:kp, :])
    for p in range(1, n_a):
        acc = acc + _dot(a_refs[p][...], w_ref[p * kp:(p + 1) * kp, :])
    if has_resid:
        acc = r_ref[...] + acc
    o_ref[...] = acc.astype(o_ref.dtype)

    if has_norm:
        hb_ref, ssq_ref = refs[pos + 1], refs[pos + 2]
        hb_ref[...] = (acc * g_ref[...]).astype(hb_ref.dtype)
        part = jnp.broadcast_to(jnp.sum(acc * acc, axis=1, keepdims=True), ssq_ref.shape)
        j = pl.program_id(1)

        @pl.when(j == 0)
        def _():
            ssq_ref[...] = part

        @pl.when(j > 0)
        def _():
            ssq_ref[...] += part


def _matmul(a_list, w, out_dtype, *, resid=None, next_gain=None, w_narrow=None, n_cols=None, w_outer=False,
            tm, tn, name):
    m, kp = a_list[0].shape
    kdim = w.shape[0]
    n = w.shape[1] if n_cols is None else n_cols
    n_a = len(a_list)
    assert m % tm == 0 and n % tn == 0 and kdim == n_a * kp
    assert not (w_outer and (next_gain is not None or w_narrow is not None))
    assert w_narrow is None or (n_a == 1 and w_narrow.shape == (kdim, LANES))

    def spec(shape, f):
        return pl.BlockSpec(shape, (lambda j, i: f(i, j)) if w_outer else f)

    in_specs = ([spec((tm, kp), lambda i, j: (i, 0)) for _ in a_list]
                + [spec((kdim, tn), lambda i, j: (0, j))])
    args = list(a_list) + [w]
    out_shape = [jax.ShapeDtypeStruct((m, n), out_dtype)]
    out_specs = [spec((tm, tn), lambda i, j: (i, j))]
    nbytes = 2 * (tm * kdim + kdim * tn) * 2 + 2 * tm * tn * jnp.dtype(out_dtype).itemsize + 2 * tm * tn * 4
    if resid is not None:
        in_specs.append(spec((tm, tn), lambda i, j: (i, j)))
        args.append(resid)
        nbytes += 2 * tm * tn * 4
    if next_gain is not None:
        in_specs.append(spec((1, tn), lambda i, j: (0, j)))
        args.append(next_gain.reshape(1, n).astype(F32))
        out_shape += [jax.ShapeDtypeStruct((m, n), BF16), jax.ShapeDtypeStruct((m, LANES), F32)]
        out_specs += [spec((tm, tn), lambda i, j: (i, j)),
                      spec((tm, LANES), lambda i, j: (i, 0))]
        nbytes += 2 * tm * tn * 2 + 2 * tm * LANES * 4 + tm * tn * 4
    if w_narrow is not None:
        in_specs.append(spec((kdim, LANES), lambda i, j: (0, 0)))
        args.append(w_narrow)
        out_shape.append(jax.ShapeDtypeStruct((m, LANES), F32))
        out_specs.append(spec((tm, LANES), lambda i, j: (i, 0)))
        nbytes += 2 * kdim * LANES * 2 + 2 * tm * LANES * 4
    outs = pl.pallas_call(
        functools.partial(_mm_kernel, n_a=n_a, has_resid=resid is not None,
                          has_norm=next_gain is not None, has_narrow=w_narrow is not None),
        out_shape=out_shape,
        grid=(n // tn, m // tm) if w_outer else (m // tm, n // tn),
        in_specs=in_specs,
        out_specs=out_specs,
        compiler_params=pltpu.CompilerParams(
            dimension_semantics=("parallel", "arbitrary"),
            vmem_limit_bytes=_vmem_limit(nbytes)),
        name=name,
    )(*args)
    return outs if len(outs) > 1 else outs[0]


def _short_proj_kernel(a_ref, wk_ref, wv_ref, k_ref, v_ref):
    a = a_ref[...]
    k_ref[...] = _dot(a, wk_ref[...].astype(BF16)).astype(k_ref.dtype)
    v_ref[...] = _dot(a, wv_ref[...].astype(BF16)).astype(v_ref.dtype)


def _short_proj(a, wk, wv, *, tn):
    m, kdim = a.shape
    n = wk.shape[1]
    assert n % tn == 0
    nbytes = 2 * m * kdim * 2 + 2 * kdim * tn * (2 * 4 + 2) + 4 * m * tn * 2 + 2 * m * tn * 4
    return pl.pallas_call(
        _short_proj_kernel,
        out_shape=[jax.ShapeDtypeStruct((m, n), BF16)] * 2,
        grid=(n // tn,),
        in_specs=[pl.BlockSpec((m, kdim), lambda j: (0, 0)),
                  pl.BlockSpec((kdim, tn), lambda j: (0, j)),
                  pl.BlockSpec((kdim, tn), lambda j: (0, j))],
        out_specs=[pl.BlockSpec((m, tn), lambda j: (0, j))] * 2,
        compiler_params=pltpu.CompilerParams(
            dimension_semantics=("parallel",), vmem_limit_bytes=_vmem_limit(nbytes)),
        name="xattn_kv",
    )(a, wk, wv)


def _swiglu_up_kernel(a_ref, ssq_ref, wg_ref, wu_ref, o_ref):
    wg = wg_ref[...].astype(BF16)
    wu = wu_ref[...].astype(BF16)
    kdim = a_ref.shape[1]
    for p in range(a_ref.shape[0] // SWIGLU_PASS_ROWS):
        rows = slice(p * SWIGLU_PASS_ROWS, (p + 1) * SWIGLU_PASS_ROWS)
        a = a_ref[rows, :]
        r = lax.rsqrt(ssq_ref[rows, 0:1] * (1.0 / kdim) + EPS)
        gate = r * _dot(a, wg)
        up = r * _dot(a, wu)
        o_ref[rows, :] = (gate * _sigmoid(gate) * up).astype(o_ref.dtype)


def _swiglu_up(a, ssq, wg, wu, *, tm, tn):
    m, kdim = a.shape
    n = wg.shape[1]
    assert m % tm == 0 and n % tn == 0 and tm % SWIGLU_PASS_ROWS == 0
    nbytes = (tm * kdim * 2 + 2 * kdim * tn * (2 * 4 + 2) + 2 * tm * tn * 2 + 2 * tm * LANES * 4
              + 3 * SWIGLU_PASS_ROWS * tn * 4)
    return pl.pallas_call(
        _swiglu_up_kernel,
        out_shape=jax.ShapeDtypeStruct((m, n), BF16),
        grid=(m // tm, n // tn),
        in_specs=[pl.BlockSpec((tm, kdim), lambda i, j: (i, 0), pipeline_mode=pl.Buffered(1)),
                  pl.BlockSpec((tm, LANES), lambda i, j: (i, 0)),
                  pl.BlockSpec((kdim, tn), lambda i, j: (0, j)),
                  pl.BlockSpec((kdim, tn), lambda i, j: (0, j))],
        out_specs=pl.BlockSpec((tm, tn), lambda i, j: (i, j)),
        compiler_params=pltpu.CompilerParams(
            dimension_semantics=("parallel", "arbitrary"),
            vmem_limit_bytes=_vmem_limit(nbytes)),
        name="swiglu_up",
    )(a, ssq, wg, wu)


def _gla_tables():
    c = CHUNK
    row = np.arange(c)[:, None]
    u = np.arange(c)[None, :]
    sel = [u <= row, u > row]
    masks = [row == u]
    for bsz in GLA_LEVELS:
        r = (row // (2 * bsz)) * (2 * bsz) + bsz - 1
        sel.append(np.where(row <= r, (u > row) & (u <= r), (u > r) & (u <= row)))
        same_pair = (row // (2 * bsz)) == (u // (2 * bsz))
        masks.append(same_pair & ((row % (2 * bsz)) >= bsz) & ((u % (2 * bsz)) < bsz))
    sel = np.concatenate(sel, axis=0).astype(np.float32)
    return (jnp.asarray(np.concatenate([sel] * 3, axis=1), BF16),
            jnp.asarray(np.stack(masks).astype(np.float32)))


def _gla_stream(q_ref, k_ref, v_ref, g_ref, sm_ref, w2_ref, gb_ref, ng_ref, sel_ref, mask_ref,
                o_ref, st_ref):
    c_len = CHUNK
    norm_g = ng_ref[...]
    w2 = w2_ref[...].astype(BF16)
    row_k = lax.broadcasted_iota(jnp.int32, (c_len, GLA_DK), 0)
    state = [st_ref[...]]
    tmp = [dict() for _ in range(SEQ_BLOCK // c_len)]

    def rows_of(c):
        return slice(c * c_len, (c + 1) * c_len)

    def gates(c):
        z = _dot(sm_ref[rows_of(c), :].astype(BF16), w2) + gb_ref[...]
        tmp[c]["la3"] = jnp.concatenate(_split3(_log_sigmoid(z) * (LOG2_E / GLA_TAU)), axis=0)

    def decay_table(c):
        tmp[c]["e"] = jnp.exp2(_dot(sel_ref[...], tmp[c].pop("la3")))

    def scores(c):
        rows = rows_of(c)
        e = tmp[c].pop("e")
        q = q_ref[rows, :].astype(F32) * (GLA_DK ** -0.5)
        k = k_ref[rows, :].astype(F32)
        k_dec = (k * e[c_len:2 * c_len, :]).astype(BF16)
        attn = _dot_nt(q.astype(BF16), k.astype(BF16)) * mask_ref[0]
        for lv, bsz in enumerate(GLA_LEVELS):
            in_right = (row_k & bsz) != 0
            x = (jnp.where(in_right, q, k) * e[(2 + lv) * c_len:(3 + lv) * c_len, :]).astype(BF16)
            attn = attn + _dot_nt(x, x) * mask_ref[1 + lv]
        tmp[c].update(q_dec=(q * e[0:c_len, :]).astype(BF16), k_dec=k_dec,
                      decay=e[c_len - 1:c_len, :],
                      intra=_dot(attn.astype(BF16), v_ref[rows, :]))

    def recurrence(c):
        tmp[c]["st_in"] = state[0].astype(BF16)
        state[0] = state[0] * tmp[c].pop("decay") + _dot_tn(v_ref[rows_of(c), :], tmp[c].pop("k_dec"))

    def output(c):
        rows = rows_of(c)
        out = tmp[c].pop("intra") + _dot_nt(tmp[c].pop("q_dec"), tmp[c].pop("st_in"))
        ms = jnp.mean(out * out, axis=-1, keepdims=True)
        y = out * lax.rsqrt(ms + EPS) * norm_g
        gg = g_ref[rows, :].astype(F32)
        o_ref[rows, :] = (y * (gg * _sigmoid(gg))).astype(o_ref.dtype)

    def finish():
        st_ref[...] = state[0]

    return [gates, decay_table, scores, recurrence, output], finish


def _gla(proj, small, w2, gate_b, norm_g, *, batch, seq):
    decay_sel, pair_mask = _gla_tables()
    t = proj.shape[0]
    lb = SEQ_BLOCK
    nsb = seq // lb
    h_ = GLA_HEADS
    qb = GLA_DK
    vb = GLA_DV

    def rows(b, h, i):
        return b * nsb + i

    in_specs = [
        pl.BlockSpec((lb, qb), lambda b, h, i: (rows(b, h, i), h)),
        pl.BlockSpec((lb, qb), lambda b, h, i: (rows(b, h, i), h_ + h)),
        pl.BlockSpec((lb, vb), lambda b, h, i: (rows(b, h, i), (2 * h_ * qb) // vb + h)),
        pl.BlockSpec((lb, vb), lambda b, h, i: (rows(b, h, i), (2 * h_ * qb) // vb + h_ + h)),
        pl.BlockSpec((lb, SMALL_COLS), lambda b, h, i: (rows(b, h, i), 0)),
        pl.BlockSpec((None, SMALL_COLS, qb), lambda b, h, i: (h, 0, 0)),
        pl.BlockSpec((None, 1, qb), lambda b, h, i: (h, 0, 0)),
        pl.BlockSpec((None, 1, vb), lambda b, h, i: (h, 0, 0)),
        pl.BlockSpec(decay_sel.shape, lambda b, h, i: (0, 0)),
        pl.BlockSpec(pair_mask.shape, lambda b, h, i: (0, 0, 0)),
    ]
    return dict(
        in_specs=in_specs,
        args=[proj, proj, proj, proj, small, w2, gate_b, norm_g, decay_sel, pair_mask],
        out_shape=jax.ShapeDtypeStruct((t, h_ * vb), BF16),
        out_spec=pl.BlockSpec((lb, vb), lambda b, h, i: (rows(b, h, i), h)),
        scratch=[pltpu.VMEM((vb, qb), F32)])


def _mlstm_tables():
    lb, c = CHUNK, CHUNK
    t = np.arange(lb)[:, None]
    u = np.arange(lb)[None, :]
    shift = np.concatenate([(u == t - d) for d in range(1, CONV_WIDTH)], axis=0).astype(np.float32)
    tc = np.arange(c)[:, None]
    uc = np.arange(c)[None, :]
    cum = np.concatenate([uc <= tc, uc > tc], axis=0).astype(np.float32)
    col = np.concatenate([tc > uc, np.ones((c, c), bool)], axis=1).astype(np.float32)
    return (jnp.asarray(shift, BF16), jnp.asarray(np.concatenate([cum] * 3, axis=1), BF16),
            jnp.asarray(col))


def _mlstm_stream(gbias_ref, q_ref, k_ref, v_ref, og_ref, sm_ref, igt_ref, cwq_ref, cwk_ref,
                  cbq_ref, cbk_ref, ng_ref, shift_ref, cum_ref, col_ref, o_ref,
                  ct_ref, n_ref, m_ref, hist_ref):
    c_len = CHUNK
    lb = SEQ_BLOCK
    taps = CONV_WIDTH
    h = pl.program_id(1)

    w = jnp.concatenate([cwq_ref[...], cwk_ref[...]], axis=1)
    bias = jnp.concatenate([cbq_ref[...], cbk_ref[...]], axis=1)
    ib = gbias_ref[h]
    fb = gbias_ref[MLSTM_HEADS + h]
    lane_s = lax.broadcasted_iota(jnp.int32, (c_len, SMALL_COLS), 1)
    norm_g = ng_ref[...]
    row_c = lax.broadcasted_iota(jnp.int32, (c_len, c_len), 0)
    col_c = lax.broadcasted_iota(jnp.int32, (c_len, c_len), 1)
    causal = row_c >= col_c

    state = [ct_ref[...], n_ref[...], m_ref[...]]
    tmp = [dict() for _ in range(lb // c_len)]

    def rows_of(c):
        return slice(c * c_len, (c + 1) * c_len)

    def prepare(c):
        rows = rows_of(c)
        x16 = jnp.concatenate([q_ref[rows, :], k_ref[rows, :]], axis=1)
        x = x16.astype(F32)
        shifted = _dot(shift_ref[...], x16)
        tail = hist_ref[...] if c == 0 else tmp[c - 1]["x_tail"]
        head = jnp.concatenate([tail, x[0:SUBLANES, :]], axis=0)
        y = bias + w[taps - 1:taps, :] * x
        for d in range(1, taps):
            sh = jnp.concatenate([head[SUBLANES - d:2 * SUBLANES - d, :],
                                  shifted[(d - 1) * c_len + SUBLANES:d * c_len, :]], axis=0)
            y = y + w[taps - 1 - d:taps - d, :] * sh
        act = y * _sigmoid(y)
        sm = sm_ref[rows, :]
        i_col = jnp.sum(jnp.where(lane_s == IG_COL + h, sm, 0.0), axis=1, keepdims=True) + ib
        f_col = jnp.sum(jnp.where(lane_s == FG_COL + h, sm, 0.0), axis=1, keepdims=True) + fb
        tmp[c].update(x_tail=x[c_len - SUBLANES:c_len, :], q=act[:, 0:MLSTM_DK],
                      k=act[:, MLSTM_DK:2 * MLSTM_DK] * (MLSTM_DK ** -0.5),
                      i_col=i_col, lf_col=_log_sigmoid(f_col))

    def gate_sums(c):
        d = tmp[c]
        g_hi, g_mid, g_lo = _split3(d.pop("lf_col") * col_ref[...])
        sums = _dot(cum_ref[...], jnp.concatenate([g_hi, g_mid, g_lo], axis=0))
        i_row = igt_ref[c, pl.ds(h, 1), :] + ib
        log_d = jnp.where(causal, sums[0:c_len, 0:c_len] + i_row, -jnp.inf)
        bb_col = sums[0:c_len, c_len:c_len + 1]
        log_w = sums[c_len:2 * c_len, c_len:c_len + 1] + d.pop("i_col")
        d.update(log_d=log_d, bb_col=bb_col, log_w=log_w,
                 d_max=jnp.max(log_d, axis=1, keepdims=True),
                 w_max=jnp.max(log_w, axis=0, keepdims=True),
                 qk=_dot_nt(d["q"].astype(BF16), d["k"].astype(BF16)))

    def stabiliser(c):
        d = tmp[c]
        m_prev = state[2]
        bb_col = d.pop("bb_col")
        b_last = bb_col[c_len - 1:c_len, :]
        inter = bb_col + m_prev
        m = jnp.maximum(inter, d.pop("d_max"))
        m_new = jnp.maximum(b_last + m_prev, d.pop("w_max"))
        d.update(m=m, g_inter=jnp.exp(inter - m), w_col=jnp.exp(d.pop("log_w") - m_new),
                 decay=jnp.exp(b_last + m_prev - m_new))
        state[2] = m_new

    def scores(c):
        rows = rows_of(c)
        d = tmp[c]
        s = d.pop("qk") * jnp.exp(d.pop("log_d") - d["m"])
        kw = d.pop("k") * d.pop("w_col")
        d.update(s_sum=jnp.sum(s, axis=1, keepdims=True),
                 intra=_dot(s.astype(BF16), v_ref[rows, :]),
                 kw=kw.astype(BF16),
                 k_sum=jnp.sum(kw, axis=0, keepdims=True))

    def recurrence(c):
        d = tmp[c]
        decay = d.pop("decay")
        d.update(ct_in=state[0].astype(BF16), n_in=state[1])
        state[0] = decay * state[0] + _dot_tn(v_ref[rows_of(c), :], d.pop("kw"))
        state[1] = decay * state[1] + d.pop("k_sum")

    def output(c):
        rows = rows_of(c)
        d = tmp[c]
        q = d.pop("q")
        g_inter = d.pop("g_inter")
        num = d.pop("intra") + g_inter * _dot_nt(q.astype(BF16), d.pop("ct_in"))
        den = d.pop("s_sum") + g_inter * jnp.sum(q * d.pop("n_in"), axis=1, keepdims=True)
        hid = num / jnp.maximum(jnp.abs(den), jnp.exp(-d.pop("m")))
        ms = jnp.mean(hid * hid, axis=-1, keepdims=True)
        y_out = hid * lax.rsqrt(ms + EPS) * norm_g
        og = og_ref[rows, :].astype(F32)
        o_ref[rows, :] = (_sigmoid(og) * y_out).astype(o_ref.dtype)

    def finish():
        ct_ref[...], n_ref[...], m_ref[...] = state
        hist_ref[...] = tmp[-1]["x_tail"]

    return [prepare, gate_sums, stabiliser, scores, recurrence, output], finish


def _mlstm(proj, small, ig_t, gate_bias, conv_w, conv_b, norm_g, *, batch, seq):
    shift, cum_sel, col_sel = _mlstm_tables()
    t = proj.shape[0]
    lb = SEQ_BLOCK
    nsb = seq // lb
    h_ = MLSTM_HEADS
    qb = MLSTM_DK
    vb = MLSTM_DV
    q0 = 0
    v0 = (2 * h_ * qb) // vb

    def rows(b, h, i):
        return b * nsb + i

    in_specs = [
        pl.BlockSpec(memory_space=pltpu.SMEM),
        pl.BlockSpec((lb, qb), lambda b, h, i: (rows(b, h, i), q0 + h)),
        pl.BlockSpec((lb, qb), lambda b, h, i: (rows(b, h, i), q0 + h_ + h)),
        pl.BlockSpec((lb, vb), lambda b, h, i: (rows(b, h, i), v0 + h)),
        pl.BlockSpec((lb, vb), lambda b, h, i: (rows(b, h, i), v0 + h_ + h)),
        pl.BlockSpec((lb, SMALL_COLS), lambda b, h, i: (rows(b, h, i), 0)),
        pl.BlockSpec((lb // CHUNK, 2 * h_, CHUNK), lambda b, h, i: (rows(b, h, i), 0, 0)),
        pl.BlockSpec((CONV_WIDTH, qb), lambda b, h, i: (0, h)),
        pl.BlockSpec((CONV_WIDTH, qb), lambda b, h, i: (0, h_ + h)),
        pl.BlockSpec((1, qb), lambda b, h, i: (0, h)),
        pl.BlockSpec((1, qb), lambda b, h, i: (0, h_ + h)),
        pl.BlockSpec((None, 1, vb), lambda b, h, i: (h, 0, 0)),
        pl.BlockSpec(shift.shape, lambda b, h, i: (0, 0)),
        pl.BlockSpec(cum_sel.shape, lambda b, h, i: (0, 0)),
        pl.BlockSpec(col_sel.shape, lambda b, h, i: (0, 0)),
    ]
    scratch = [
        pltpu.VMEM((vb, qb), F32),
        pltpu.VMEM((1, qb), F32),
        pltpu.VMEM((1, 1), F32),
        pltpu.VMEM((SUBLANES, 2 * qb), F32),
    ]
    return dict(
        in_specs=in_specs,
        args=[gate_bias, proj, proj, proj, proj, small, ig_t, conv_w, conv_w, conv_b, conv_b, norm_g,
              shift, cum_sel, col_sel],
        out_shape=jax.ShapeDtypeStruct((t, h_ * vb), BF16),
        out_spec=pl.BlockSpec((lb, vb), lambda b, h, i: (rows(b, h, i), h)),
        scratch=scratch)


def _mixer_scan_kernel(*refs, n_gla_in, n_ml_in, n_gla_scratch):
    gla_in = refs[:n_gla_in]
    ml_in = refs[n_gla_in:n_gla_in + n_ml_in]
    o_gla, o_ml = refs[n_gla_in + n_ml_in:n_gla_in + n_ml_in + 2]
    scratch = refs[n_gla_in + n_ml_in + 2:]

    @pl.when(pl.program_id(2) == 0)
    def _():
        for ref in scratch:
            ref[...] = jnp.zeros_like(ref)

    gla_stages, gla_finish = _gla_stream(*gla_in, o_gla, *scratch[:n_gla_scratch])
    ml_stages, ml_finish = _mlstm_stream(*ml_in, o_ml, *scratch[n_gla_scratch:])
    n_chunks = SEQ_BLOCK // CHUNK
    assert len(ml_stages) == SCAN_STAGES >= len(gla_stages)
    lead = len(ml_stages) - len(gla_stages)
    for t in range(n_chunks + len(ml_stages) - 1):
        for k in reversed(range(len(ml_stages))):
            c = t - k
            if 0 <= c < n_chunks:
                ml_stages[k](c)
                if k >= lead:
                    gla_stages[k - lead](c)
    gla_finish()
    ml_finish()


def _mixer_scan(gla, ml, *, batch, seq):
    assert GLA_HEADS == MLSTM_HEADS
    in_specs = gla["in_specs"] + ml["in_specs"]
    out_specs = [gla["out_spec"], ml["out_spec"]]
    out_shape = [gla["out_shape"], ml["out_shape"]]
    state_bytes = 2 * (GLA_DV * GLA_DK + MLSTM_DV * MLSTM_DK) * 4
    wave_bytes = SCAN_STAGES * 2 * 8 * CHUNK * GLA_DK * 4
    nbytes = (2 * _block_bytes(in_specs, gla["args"] + ml["args"]) + 2 * _block_bytes(out_specs, out_shape)
              + state_bytes + wave_bytes)
    outs = pl.pallas_call(
        functools.partial(_mixer_scan_kernel, n_gla_in=len(gla["args"]), n_ml_in=len(ml["args"]),
                          n_gla_scratch=len(gla["scratch"])),
        out_shape=out_shape,
        grid=(batch, GLA_HEADS, seq // SEQ_BLOCK),
        in_specs=in_specs,
        out_specs=out_specs,
        scratch_shapes=gla["scratch"] + ml["scratch"],
        compiler_params=pltpu.CompilerParams(
            dimension_semantics=("parallel", "parallel", "arbitrary"),
            vmem_limit_bytes=_vmem_limit(nbytes)),
        name="mixer_scan",
    )(*gla["args"], *ml["args"])
    return outs[0], outs[1]


def _xattn_kernel(a_ref, ssq_ref, wq_ref, k_ref, v_ref, o_ref):
    a = a_ref[...]
    r = lax.rsqrt(ssq_ref[:, 0:1] * (1.0 / a.shape[1]) + EPS)
    q = (r * _dot(a, wq_ref[...])).astype(BF16)
    groups = [slice(g * XATTN_ROWS, (g + 1) * XATTN_ROWS) for g in range(a.shape[0] // XATTN_ROWS)]
    scores = [_dot_nt(q[rows, :], k_ref[...]) * (XATTN_HEAD_DIM ** -0.5) for rows in groups]
    probs = []
    for s in scores:
        p = jnp.exp(s - jnp.max(s, axis=-1, keepdims=True))
        probs.append((p / jnp.sum(p, axis=-1, keepdims=True)).astype(BF16))
    for rows, p in zip(groups, probs):
        o_ref[rows, :] = _dot(p, v_ref[...]).astype(o_ref.dtype)


def _xattn(a, ssq, wq, k, v, *, batch, seq, tq):
    t, d = a.shape
    mem = k.shape[0] // batch
    hd = XATTN_HEAD_DIM
    nq = seq // tq
    nbytes = 2 * (tq * d + d * hd + 2 * mem * hd + tq * hd) * 2 + 2 * tq * LANES * 4 + 3 * tq * hd * 4
    return pl.pallas_call(
        _xattn_kernel,
        out_shape=jax.ShapeDtypeStruct((t, d), BF16),
        grid=(batch * nq, XATTN_HEADS),
        in_specs=[pl.BlockSpec((tq, d), lambda i, h: (i, 0)),
                  pl.BlockSpec((tq, LANES), lambda i, h: (i, 0)),
                  pl.BlockSpec((d, hd), lambda i, h: (0, h)),
                  pl.BlockSpec((mem, hd), lambda i, h: (i // nq, h)),
                  pl.BlockSpec((mem, hd), lambda i, h: (i // nq, h))],
        out_specs=pl.BlockSpec((tq, hd), lambda i, h: (i, h)),
        compiler_params=pltpu.CompilerParams(
            dimension_semantics=("parallel", "parallel"),
            vmem_limit_bytes=_vmem_limit(nbytes)),
        name="cross_attention",
    )(a, ssq, wq, k, v)


def kernel(x, mem, norm_mix_g, w_in, gla_gate_w2, gla_gate_b, gla_norm_g, mlstm_conv_w, mlstm_conv_b, mlstm_igate_b, mlstm_fgate_b, mlstm_norm_g, w_out, norm_cross_g, norm_mem_g, wq_c, wk_c, wv_c, wo_c, norm_ffn_g, w_gate, w_up, w_down, norm_final_g):
    batch, seq, d = x.shape
    mem_tokens = mem.shape[1]
    t = batch * seq
    depth = w_in.shape[0]
    gla_cols = 2 * GLA_HEADS * GLA_DK + 2 * GLA_HEADS * GLA_DV
    ml_cols = 2 * MLSTM_HEADS * MLSTM_DK + 2 * MLSTM_HEADS * MLSTM_DV
    ml_start = gla_cols + GLA_GATE_RANK

    h = x.reshape(t, d)
    mem2 = mem.reshape(batch * mem_tokens, d)
    for l in range(depth):
        w_in16 = w_in[l].astype(BF16)
        w_ml = w_in16[:, ml_start:ml_start + ml_cols]
        w_small = jnp.concatenate(
            [w_in[l][:, gla_cols:ml_start], w_in[l][:, ml_start + ml_cols:]], axis=1)
        w_small = jnp.pad(w_small, ((0, 0), (0, SMALL_COLS - w_small.shape[1]))).astype(BF16)
        w2 = gla_gate_w2[l].reshape(GLA_GATE_RANK, GLA_HEADS, GLA_DK).transpose(1, 0, 2)
        w2 = jnp.pad(w2, ((0, 0), (0, SMALL_COLS - GLA_GATE_RANK), (0, 0)))
        gate_b = gla_gate_b[l].reshape(GLA_HEADS, 1, GLA_DK)
        gla_g = gla_norm_g[l].reshape(GLA_HEADS, 1, GLA_DV)
        ml_g = mlstm_norm_g[l].reshape(MLSTM_HEADS, 1, MLSTM_DV)
        gate_bias = jnp.concatenate([mlstm_igate_b[l], mlstm_fgate_b[l]]).astype(F32)
        conv_b = mlstm_conv_b[l].reshape(1, -1)

        n1 = _rmsnorm(h, norm_mix_g[l], BF16)
        proj_gla = _matmul([n1], w_in16, BF16, n_cols=gla_cols, tm=1024, tn=1024, name="in_proj_gla")
        proj_ml, small = _matmul([n1], w_ml, BF16, w_narrow=w_small, tm=1024, tn=1024, name="in_proj_mlstm")
        ig_t = small[:, IG_COL:IG_COL + 2 * MLSTM_HEADS].reshape(
            t // CHUNK, CHUNK, 2 * MLSTM_HEADS).transpose(0, 2, 1)
        o_gla, o_ml = _mixer_scan(
            _gla(proj_gla, small, w2, gate_b, gla_g, batch=batch, seq=seq),
            _mlstm(proj_ml, small, ig_t, gate_bias, mlstm_conv_w[l], conv_b, ml_g, batch=batch, seq=seq),
            batch=batch, seq=seq)
        h, hb, ssq = _matmul([o_gla, o_ml], w_out[l].astype(BF16), F32, resid=h,
                             next_gain=norm_cross_g[l], tm=1024, tn=512, name="out_proj")

        mem_n = _rmsnorm(mem2, norm_mem_g[l], BF16)
        k_mem, v_mem = _short_proj(mem_n, wk_c[l], wv_c[l], tn=512)
        attn = _xattn(hb, ssq, wq_c[l].astype(BF16), k_mem, v_mem, batch=batch, seq=seq, tq=1024)
        h, hb, ssq = _matmul([attn], wo_c[l].astype(BF16), F32, resid=h,
                             next_gain=norm_ffn_g[l], tm=1024, tn=512, name="xattn_o")

        hid = _swiglu_up(hb, ssq, w_gate[l], w_up[l], tm=2048, tn=FFN_TN)
        h = _matmul([hid], w_down[l].astype(BF16), F32, resid=h, w_outer=True, tm=512, tn=512,
                    name="ffn_down")

    y = _rmsnorm(h, norm_final_g, x.dtype)
    return y.reshape(batch, seq, d)
```

```python
import functools
import math

import jax
import jax.numpy as jnp
import numpy as np
from jax import lax
from jax.experimental import pallas as pl
from jax.experimental.pallas import tpu as pltpu

F32 = jnp.float32
BF16 = jnp.bfloat16

D_MODEL = 4096
GROUP_WIDTH = D_MODEL // 2
GLA_HEADS = 4
GLA_DV = GROUP_WIDTH // GLA_HEADS
GLA_DK = GLA_DV // 2
GLA_GATE_RANK = 16
GLA_TAU = 16.0
MLSTM_HEADS = 4
MLSTM_DV = GROUP_WIDTH // MLSTM_HEADS
MLSTM_DK = MLSTM_DV // 2
CONV_WIDTH = 4
XATTN_HEADS = 4
XATTN_HEAD_DIM = D_MODEL // XATTN_HEADS
FFN_HIDDEN = 256 * math.ceil(8 * D_MODEL / (3 * 256))
EPS = 1e-6

V7X_VMEM_LIMIT_BYTES = 62 * 1024 * 1024
DEFAULT_SCOPED_VMEM_BYTES = 32 * 1024 * 1024
VMEM_UNMODELLED_BYTES = 8 * 1024 * 1024
LANES = 128
SUBLANES = 8

CHUNK = 64
GLA_LEVELS = (32, 16, 8, 4, 2, 1)
LOG2_E = math.log2(math.e)
SEQ_BLOCK = 1024
SCAN_STAGES = 6
NORM_ROWS = 512
FFN_TN = 256
XATTN_ROWS = 256
SWIGLU_PASS_ROWS = 1024
SMALL_COLS = LANES
IG_COL = GLA_GATE_RANK
FG_COL = GLA_GATE_RANK + MLSTM_HEADS


def _vmem_limit(nbytes):
    return int(min(V7X_VMEM_LIMIT_BYTES, max(DEFAULT_SCOPED_VMEM_BYTES, nbytes + VMEM_UNMODELLED_BYTES)))


def _block_bytes(specs, arrays):
    total = 0
    for spec, arr in zip(specs, arrays):
        if spec.block_shape is not None:
            total += math.prod(d for d in spec.block_shape if d is not None) * jnp.dtype(arr.dtype).itemsize
    return total


def _dot(a, b):
    return jnp.dot(a, b, preferred_element_type=F32)


def _dot_nt(a, b):
    return lax.dot_general(a, b, (((1,), (1,)), ((), ())), preferred_element_type=F32)


def _dot_tn(a, b):
    return lax.dot_general(a, b, (((0,), (0,)), ((), ())), preferred_element_type=F32)


def _split3(x):
    hi = x.astype(BF16)
    r = x - hi.astype(F32)
    mid = r.astype(BF16)
    lo = (r - mid.astype(F32)).astype(BF16)
    return hi, mid, lo


def _log_sigmoid(z):
    return jnp.minimum(z, 0.0) - jnp.log(1.0 + jnp.exp(-jnp.abs(z)))


def _sigmoid(z):
    return 1.0 / (1.0 + jnp.exp(-z))


def _rmsnorm_kernel(x_ref, g_ref, o_ref):
    x = x_ref[...].astype(F32)
    ms = jnp.mean(x * x, axis=-1, keepdims=True)
    o_ref[...] = (x * lax.rsqrt(ms + EPS) * g_ref[...]).astype(o_ref.dtype)


def _rmsnorm(x, g, out_dtype):
    rows, d = x.shape
    tr = min(NORM_ROWS, rows)
    nbytes = 2 * tr * d * (x.dtype.itemsize + jnp.dtype(out_dtype).itemsize) + 3 * tr * d * 4
    return pl.pallas_call(
        _rmsnorm_kernel,
        out_shape=jax.ShapeDtypeStruct((rows, d), out_dtype),
        grid=(rows // tr,),
        in_specs=[pl.BlockSpec((tr, d), lambda i: (i, 0)),
                  pl.BlockSpec((1, d), lambda i: (0, 0))],
        out_specs=pl.BlockSpec((tr, d), lambda i: (i, 0)),
        compiler_params=pltpu.CompilerParams(
            dimension_semantics=("parallel",), vmem_limit_bytes=_vmem_limit(nbytes)),
        name="rmsnorm",
    )(x, g.reshape(1, d).astype(F32))


def _mm_kernel(*refs, n_a, has_resid, has_norm, has_narrow):
    a_refs = refs[:n_a]
    w_ref = refs[n_a]
    pos = n_a + 1
    r_ref = refs[pos] if has_resid else None
    pos += int(has_resid)
    g_ref = refs[pos] if has_norm else None
    pos += int(has_norm)
    wn_ref = refs[pos] if has_narrow else None
    pos += int(has_narrow)
    o_ref = refs[pos]

    if has_narrow:
        @pl.when(pl.program_id(1) == 0)
        def _():
            refs[-1][...] = _dot(a_refs[0][...], wn_ref[...])

    kp = a_refs[0].shape[1]
    acc = _dot(a_refs[0][...], w_ref[0:kp, :])
    for p in range(1, n_a):
        acc = acc + _dot(a_refs[p][...], w_ref[p * kp:(p + 1) * kp, :])
    if has_resid:
        acc = r_ref[...] + acc
    o_ref[...] = acc.astype(o_ref.dtype)

    if has_norm:
        hb_ref, ssq_ref = refs[pos + 1], refs[pos + 2]
        hb_ref[...] = (acc * g_ref[...]).astype(hb_ref.dtype)
        part = jnp.broadcast_to(jnp.sum(acc * acc, axis=1, keepdims=True), ssq_ref.shape)
        j = pl.program_id(1)

        @pl.when(j == 0)
        def _():
            ssq_ref[...] = part

        @pl.when(j > 0)
        def _():
            ssq_ref[...] += part


def _matmul(a_list, w, out_dtype, *, resid=None, next_gain=None, w_narrow=None, n_cols=None, w_outer=False,
            tm, tn, name):
    m, kp = a_list[0].shape
    kdim = w.shape[0]
    n = w.shape[1] if n_cols is None else n_cols
    n_a = len(a_list)
    assert m % tm == 0 and n % tn == 0 and kdim == n_a * kp
    assert not (w_outer and (next_gain is not None or w_narrow is not None))
    assert w_narrow is None or (n_a == 1 and w_narrow.shape == (kdim, LANES))

    def spec(shape, f):
        return pl.BlockSpec(shape, (lambda j, i: f(i, j)) if w_outer else f)

    in_specs = ([spec((tm, kp), lambda i, j: (i, 0)) for _ in a_list]
                + [spec((kdim, tn), lambda i, j: (0, j))])
    args = list(a_list) + [w]
    out_shape = [jax.ShapeDtypeStruct((m, n), out_dtype)]
    out_specs = [spec((tm, tn), lambda i, j: (i, j))]
    nbytes = 2 * (tm * kdim + kdim * tn) * 2 + 2 * tm * tn * jnp.dtype(out_dtype).itemsize + 2 * tm * tn * 4
    if resid is not None:
        in_specs.append(spec((tm, tn), lambda i, j: (i, j)))
        args.append(resid)
        nbytes += 2 * tm * tn * 4
    if next_gain is not None:
        in_specs.append(spec((1, tn), lambda i, j: (0, j)))
        args.append(next_gain.reshape(1, n).astype(F32))
        out_shape += [jax.ShapeDtypeStruct((m, n), BF16), jax.ShapeDtypeStruct((m, LANES), F32)]
        out_specs += [spec((tm, tn), lambda i, j: (i, j)),
                      spec((tm, LANES), lambda i, j: (i, 0))]
        nbytes += 2 * tm * tn * 2 + 2 * tm * LANES * 4 + tm * tn * 4
    if w_narrow is not None:
        in_specs.append(spec((kdim, LANES), lambda i, j: (0, 0)))
        args.append(w_narrow)
        out_shape.append(jax.ShapeDtypeStruct((m, LANES), F32))
        out_specs.append(spec((tm, LANES), lambda i, j: (i, 0)))
        nbytes += 2 * kdim * LANES * 2 + 2 * tm * LANES * 4
    outs = pl.pallas_call(
        functools.partial(_mm_kernel, n_a=n_a, has_resid=resid is not None,
                          has_norm=next_gain is not None, has_narrow=w_narrow is not None),
        out_shape=out_shape,
        grid=(n // tn, m // tm) if w_outer else (m // tm, n // tn),
        in_specs=in_specs,
        out_specs=out_specs,
        compiler_params=pltpu.CompilerParams(
            dimension_semantics=("parallel", "arbitrary"),
            vmem_limit_bytes=_vmem_limit(nbytes)),
        name=name,
    )(*args)
    return outs if len(outs) > 1 else outs[0]


def _short_proj_kernel(a_ref, wk_ref, wv_ref, k_ref, v_ref):
    a = a_ref[...]
    k_ref[...] = _dot(a, wk_ref[...].astype(BF16)).astype(k_ref.dtype)
    v_ref[...] = _dot(a, wv_ref[...].astype(BF16)).astype(v_ref.dtype)


def _short_proj(a, wk, wv, *, tn):
    m, kdim = a.shape
    n = wk.shape[1]
    assert n % tn == 0
    nbytes = 2 * m * kdim * 2 + 2 * kdim * tn * (2 * 4 + 2) + 4 * m * tn * 2 + 2 * m * tn * 4
    return pl.pallas_call(
        _short_proj_kernel,
        out_shape=[jax.ShapeDtypeStruct((m, n), BF16)] * 2,
        grid=(n // tn,),
        in_specs=[pl.BlockSpec((m, kdim), lambda j: (0, 0)),
                  pl.BlockSpec((kdim, tn), lambda j: (0, j)),
                  pl.BlockSpec((kdim, tn), lambda j: (0, j))],
        out_specs=[pl.BlockSpec((m, tn), lambda j: (0, j))] * 2,
        compiler_params=pltpu.CompilerParams(
            dimension_semantics=("parallel",), vmem_limit_bytes=_vmem_limit(nbytes)),
        name="xattn_kv",
    )(a, wk, wv)


def _swiglu_up_kernel(a_ref, ssq_ref, wg_ref, wu_ref, o_ref):
    wg = wg_ref[...].astype(BF16)
    wu = wu_ref[...].astype(BF16)
    kdim = a_ref.shape[1]
    for p in range(a_ref.shape[0] // SWIGLU_PASS_ROWS):
        rows = slice(p * SWIGLU_PASS_ROWS, (p + 1) * SWIGLU_PASS_ROWS)
        a = a_ref[rows, :]
        r = lax.rsqrt(ssq_ref[rows, 0:1] * (1.0 / kdim) + EPS)
        gate = r * _dot(a, wg)
        up = r * _dot(a, wu)
        o_ref[rows, :] = (gate * _sigmoid(gate) * up).astype(o_ref.dtype)


def _swiglu_up(a, ssq, wg, wu, *, tm, tn):
    m, kdim = a.shape
    n = wg.shape[1]
    assert m % tm == 0 and n % tn == 0 and tm % SWIGLU_PASS_ROWS == 0
    nbytes = (2 * tm * kdim * 2 + 2 * kdim * tn * (2 * 4 + 2) + 2 * tm * tn * 2 + 2 * tm * LANES * 4
              + 3 * SWIGLU_PASS_ROWS * tn * 4)
    return pl.pallas_call(
        _swiglu_up_kernel,
        out_shape=jax.ShapeDtypeStruct((m, n), BF16),
        grid=(m // tm, n // tn),
        in_specs=[pl.BlockSpec((tm, kdim), lambda i, j: (i, 0)),
                  pl.BlockSpec((tm, LANES), lambda i, j: (i, 0)),
                  pl.BlockSpec((kdim, tn), lambda i, j: (0, j)),
                  pl.BlockSpec((kdim, tn), lambda i, j: (0, j))],
        out_specs=pl.BlockSpec((tm, tn), lambda i, j: (i, j)),
        compiler_params=pltpu.CompilerParams(
            dimension_semantics=("parallel", "arbitrary"),
            vmem_limit_bytes=_vmem_limit(nbytes)),
        name="swiglu_up",
    )(a, ssq, wg, wu)


def _gla_tables():
    c = CHUNK
    row = np.arange(c)[:, None]
    u = np.arange(c)[None, :]
    sel = [u <= row, u > row]
    masks = [row == u]
    for bsz in GLA_LEVELS:
        r = (row // (2 * bsz)) * (2 * bsz) + bsz - 1
        sel.append(np.where(row <= r, (u > row) & (u <= r), (u > r) & (u <= row)))
        same_pair = (row // (2 * bsz)) == (u // (2 * bsz))
        masks.append(same_pair & ((row % (2 * bsz)) >= bsz) & ((u % (2 * bsz)) < bsz))
    sel = np.concatenate(sel, axis=0).astype(np.float32)
    return (jnp.asarray(np.concatenate([sel] * 3, axis=1), BF16),
            jnp.asarray(np.stack(masks).astype(np.float32)))


def _gla_stream(q_ref, k_ref, v_ref, g_ref, sm_ref, w2_ref, gb_ref, ng_ref, sel_ref, mask_ref,
                o_ref, st_ref):
    c_len = CHUNK
    norm_g = ng_ref[...]
    w2 = w2_ref[...].astype(BF16)
    row_k = lax.broadcasted_iota(jnp.int32, (c_len, GLA_DK), 0)
    state = [st_ref[...]]
    tmp = [dict() for _ in range(SEQ_BLOCK // c_len)]

    def rows_of(c):
        return slice(c * c_len, (c + 1) * c_len)

    def gates(c):
        z = _dot(sm_ref[rows_of(c), :].astype(BF16), w2) + gb_ref[...]
        tmp[c]["la3"] = jnp.concatenate(_split3(_log_sigmoid(z) * (LOG2_E / GLA_TAU)), axis=0)

    def decay_table(c):
        tmp[c]["e"] = jnp.exp2(_dot(sel_ref[...], tmp[c].pop("la3")))

    def scores(c):
        rows = rows_of(c)
        e = tmp[c].pop("e")
        q = q_ref[rows, :].astype(F32) * (GLA_DK ** -0.5)
        k = k_ref[rows, :].astype(F32)
        k_dec = (k * e[c_len:2 * c_len, :]).astype(BF16)
        attn = _dot_nt(q.astype(BF16), k.astype(BF16)) * mask_ref[0]
        for lv, bsz in enumerate(GLA_LEVELS):
            in_right = (row_k & bsz) != 0
            x = (jnp.where(in_right, q, k) * e[(2 + lv) * c_len:(3 + lv) * c_len, :]).astype(BF16)
            attn = attn + _dot_nt(x, x) * mask_ref[1 + lv]
        tmp[c].update(q_dec=(q * e[0:c_len, :]).astype(BF16), k_dec=k_dec,
                      decay=e[c_len - 1:c_len, :],
                      intra=_dot(attn.astype(BF16), v_ref[rows, :]))

    def recurrence(c):
        tmp[c]["st_in"] = state[0].astype(BF16)
        state[0] = state[0] * tmp[c].pop("decay") + _dot_tn(v_ref[rows_of(c), :], tmp[c].pop("k_dec"))

    def output(c):
        rows = rows_of(c)
        out = tmp[c].pop("intra") + _dot_nt(tmp[c].pop("q_dec"), tmp[c].pop("st_in"))
        ms = jnp.mean(out * out, axis=-1, keepdims=True)
        y = out * lax.rsqrt(ms + EPS) * norm_g
        gg = g_ref[rows, :].astype(F32)
        o_ref[rows, :] = (y * (gg * _sigmoid(gg))).astype(o_ref.dtype)

    def finish():
        st_ref[...] = state[0]

    return [gates, decay_table, scores, recurrence, output], finish


def _gla(proj, small, w2, gate_b, norm_g, *, batch, seq):
    decay_sel, pair_mask = _gla_tables()
    t = proj.shape[0]
    lb = SEQ_BLOCK
    nsb = seq // lb
    h_ = GLA_HEADS
    qb = GLA_DK
    vb = GLA_DV

    def rows(b, h, i):
        return b * nsb + i

    in_specs = [
        pl.BlockSpec((lb, qb), lambda b, h, i: (rows(b, h, i), h)),
        pl.BlockSpec((lb, qb), lambda b, h, i: (rows(b, h, i), h_ + h)),
        pl.BlockSpec((lb, vb), lambda b, h, i: (rows(b, h, i), (2 * h_ * qb) // vb + h)),
        pl.BlockSpec((lb, vb), lambda b, h, i: (rows(b, h, i), (2 * h_ * qb) // vb + h_ + h)),
        pl.BlockSpec((lb, SMALL_COLS), lambda b, h, i: (rows(b, h, i), 0)),
        pl.BlockSpec((None, SMALL_COLS, qb), lambda b, h, i: (h, 0, 0)),
        pl.BlockSpec((None, 1, qb), lambda b, h, i: (h, 0, 0)),
        pl.BlockSpec((None, 1, vb), lambda b, h, i: (h, 0, 0)),
        pl.BlockSpec(decay_sel.shape, lambda b, h, i: (0, 0)),
        pl.BlockSpec(pair_mask.shape, lambda b, h, i: (0, 0, 0)),
    ]
    return dict(
        in_specs=in_specs,
        args=[proj, proj, proj, proj, small, w2, gate_b, norm_g, decay_sel, pair_mask],
        out_shape=jax.ShapeDtypeStruct((t, h_ * vb), BF16),
        out_spec=pl.BlockSpec((lb, vb), lambda b, h, i: (rows(b, h, i), h)),
        scratch=[pltpu.VMEM((vb, qb), F32)])


def _mlstm_tables():
    lb, c = CHUNK, CHUNK
    t = np.arange(lb)[:, None]
    u = np.arange(lb)[None, :]
    shift = np.concatenate([(u == t - d) for d in range(1, CONV_WIDTH)], axis=0).astype(np.float32)
    tc = np.arange(c)[:, None]
    uc = np.arange(c)[None, :]
    cum = np.concatenate([uc <= tc, uc > tc], axis=0).astype(np.float32)
    col = np.concatenate([tc > uc, np.ones((c, c), bool)], axis=1).astype(np.float32)
    return (jnp.asarray(shift, BF16), jnp.asarray(np.concatenate([cum] * 3, axis=1), BF16),
            jnp.asarray(col))


def _mlstm_stream(gbias_ref, q_ref, k_ref, v_ref, og_ref, sm_ref, igt_ref, cwq_ref, cwk_ref,
                  cbq_ref, cbk_ref, ng_ref, shift_ref, cum_ref, col_ref, o_ref,
                  ct_ref, n_ref, m_ref, hist_ref):
    c_len = CHUNK
    lb = SEQ_BLOCK
    taps = CONV_WIDTH
    h = pl.program_id(1)

    w = jnp.concatenate([cwq_ref[...], cwk_ref[...]], axis=1)
    bias = jnp.concatenate([cbq_ref[...], cbk_ref[...]], axis=1)
    ib = gbias_ref[h]
    fb = gbias_ref[MLSTM_HEADS + h]
    lane_s = lax.broadcasted_iota(jnp.int32, (c_len, SMALL_COLS), 1)
    norm_g = ng_ref[...]
    row_c = lax.broadcasted_iota(jnp.int32, (c_len, c_len), 0)
    col_c = lax.broadcasted_iota(jnp.int32, (c_len, c_len), 1)
    causal = row_c >= col_c

    state = [ct_ref[...], n_ref[...], m_ref[...]]
    tmp = [dict() for _ in range(lb // c_len)]

    def rows_of(c):
        return slice(c * c_len, (c + 1) * c_len)

    def prepare(c):
        rows = rows_of(c)
        x16 = jnp.concatenate([q_ref[rows, :], k_ref[rows, :]], axis=1)
        x = x16.astype(F32)
        shifted = _dot(shift_ref[...], x16)
        tail = hist_ref[...] if c == 0 else tmp[c - 1]["x_tail"]
        head = jnp.concatenate([tail, x[0:SUBLANES, :]], axis=0)
        y = bias + w[taps - 1:taps, :] * x
        for d in range(1, taps):
            sh = jnp.concatenate([head[SUBLANES - d:2 * SUBLANES - d, :],
                                  shifted[(d - 1) * c_len + SUBLANES:d * c_len, :]], axis=0)
            y = y + w[taps - 1 - d:taps - d, :] * sh
        act = y * _sigmoid(y)
        sm = sm_ref[rows, :]
        i_col = jnp.sum(jnp.where(lane_s == IG_COL + h, sm, 0.0), axis=1, keepdims=True) + ib
        f_col = jnp.sum(jnp.where(lane_s == FG_COL + h, sm, 0.0), axis=1, keepdims=True) + fb
        tmp[c].update(x_tail=x[c_len - SUBLANES:c_len, :], q=act[:, 0:MLSTM_DK],
                      k=act[:, MLSTM_DK:2 * MLSTM_DK] * (MLSTM_DK ** -0.5),
                      i_col=i_col, lf_col=_log_sigmoid(f_col))

    def gate_sums(c):
        d = tmp[c]
        g_hi, g_mid, g_lo = _split3(d.pop("lf_col") * col_ref[...])
        sums = _dot(cum_ref[...], jnp.concatenate([g_hi, g_mid, g_lo], axis=0))
        i_row = igt_ref[c, pl.ds(h, 1), :] + ib
        log_d = jnp.where(causal, sums[0:c_len, 0:c_len] + i_row, -jnp.inf)
        bb_col = sums[0:c_len, c_len:c_len + 1]
        log_w = sums[c_len:2 * c_len, c_len:c_len + 1] + d.pop("i_col")
        d.update(log_d=log_d, bb_col=bb_col, log_w=log_w,
                 d_max=jnp.max(log_d, axis=1, keepdims=True),
                 w_max=jnp.max(log_w, axis=0, keepdims=True),
                 qk=_dot_nt(d["q"].astype(BF16), d["k"].astype(BF16)))

    def stabiliser(c):
        d = tmp[c]
        m_prev = state[2]
        bb_col = d.pop("bb_col")
        b_last = bb_col[c_len - 1:c_len, :]
        inter = bb_col + m_prev
        m = jnp.maximum(inter, d.pop("d_max"))
        m_new = jnp.maximum(b_last + m_prev, d.pop("w_max"))
        d.update(m=m, g_inter=jnp.exp(inter - m), w_col=jnp.exp(d.pop("log_w") - m_new),
                 decay=jnp.exp(b_last + m_prev - m_new))
        state[2] = m_new

    def scores(c):
        rows = rows_of(c)
        d = tmp[c]
        s = d.pop("qk") * jnp.exp(d.pop("log_d") - d["m"])
        kw = d.pop("k") * d.pop("w_col")
        d.update(s_sum=jnp.sum(s, axis=1, keepdims=True),
                 intra=_dot(s.astype(BF16), v_ref[rows, :]),
                 kw=kw.astype(BF16),
                 k_sum=jnp.sum(kw, axis=0, keepdims=True))

    def recurrence(c):
        d = tmp[c]
        decay = d.pop("decay")
        d.update(ct_in=state[0].astype(BF16), n_in=state[1])
        state[0] = decay * state[0] + _dot_tn(v_ref[rows_of(c), :], d.pop("kw"))
        state[1] = decay * state[1] + d.pop("k_sum")

    def output(c):
        rows = rows_of(c)
        d = tmp[c]
        q = d.pop("q")
        g_inter = d.pop("g_inter")
        num = d.pop("intra") + g_inter * _dot_nt(q.astype(BF16), d.pop("ct_in"))
        den = d.pop("s_sum") + g_inter * jnp.sum(q * d.pop("n_in"), axis=1, keepdims=True)
        hid = num / jnp.maximum(jnp.abs(den), jnp.exp(-d.pop("m")))
        ms = jnp.mean(hid * hid, axis=-1, keepdims=True)
        y_out = hid * lax.rsqrt(ms + EPS) * norm_g
        og = og_ref[rows, :].astype(F32)
        o_ref[rows, :] = (_sigmoid(og) * y_out).astype(o_ref.dtype)

    def finish():
        ct_ref[...], n_ref[...], m_ref[...] = state
        hist_ref[...] = tmp[-1]["x_tail"]

    return [prepare, gate_sums, stabiliser, scores, recurrence, output], finish


def _mlstm(proj, small, ig_t, gate_bias, conv_w, conv_b, norm_g, *, batch, seq):
    shift, cum_sel, col_sel = _mlstm_tables()
    t = proj.shape[0]
    lb = SEQ_BLOCK
    nsb = seq // lb
    h_ = MLSTM_HEADS
    qb = MLSTM_DK
    vb = MLSTM_DV
    q0 = 0
    v0 = (2 * h_ * qb) // vb

    def rows(b, h, i):
        return b * nsb + i

    in_specs = [
        pl.BlockSpec(memory_space=pltpu.SMEM),
        pl.BlockSpec((lb, qb), lambda b, h, i: (rows(b, h, i), q0 + h)),
        pl.BlockSpec((lb, qb), lambda b, h, i: (rows(b, h, i), q0 + h_ + h)),
        pl.BlockSpec((lb, vb), lambda b, h, i: (rows(b, h, i), v0 + h)),
        pl.BlockSpec((lb, vb), lambda b, h, i: (rows(b, h, i), v0 + h_ + h)),
        pl.BlockSpec((lb, SMALL_COLS), lambda b, h, i: (rows(b, h, i), 0)),
        pl.BlockSpec((lb // CHUNK, 2 * h_, CHUNK), lambda b, h, i: (rows(b, h, i), 0, 0)),
        pl.BlockSpec((CONV_WIDTH, qb), lambda b, h, i: (0, h)),
        pl.BlockSpec((CONV_WIDTH, qb), lambda b, h, i: (0, h_ + h)),
        pl.BlockSpec((1, qb), lambda b, h, i: (0, h)),
        pl.BlockSpec((1, qb), lambda b, h, i: (0, h_ + h)),
        pl.BlockSpec((None, 1, vb), lambda b, h, i: (h, 0, 0)),
        pl.BlockSpec(shift.shape, lambda b, h, i: (0, 0)),
        pl.BlockSpec(cum_sel.shape, lambda b, h, i: (0, 0)),
        pl.BlockSpec(col_sel.shape, lambda b, h, i: (0, 0)),
    ]
    scratch = [
        pltpu.VMEM((vb, qb), F32),
        pltpu.VMEM((1, qb), F32),
        pltpu.VMEM((1, 1), F32),
        pltpu.VMEM((SUBLANES, 2 * qb), F32),
    ]
    return dict(
        in_specs=in_specs,
        args=[gate_bias, proj, proj, proj, proj, small, ig_t, conv_w, conv_w, conv_b, conv_b, norm_g,
              shift, cum_sel, col_sel],
        out_shape=jax.ShapeDtypeStruct((t, h_ * vb), BF16),
        out_spec=pl.BlockSpec((lb, vb), lambda b, h, i: (rows(b, h, i), h)),
        scratch=scratch)


def _mixer_scan_kernel(*refs, n_gla_in, n_ml_in, n_gla_scratch):
    gla_in = refs[:n_gla_in]
    ml_in = refs[n_gla_in:n_gla_in + n_ml_in]
    o_gla, o_ml = refs[n_gla_in + n_ml_in:n_gla_in + n_ml_in + 2]
    scratch = refs[n_gla_in + n_ml_in + 2:]

    @pl.when(pl.program_id(2) == 0)
    def _():
        for ref in scratch:
            ref[...] = jnp.zeros_like(ref)

    gla_stages, gla_finish = _gla_stream(*gla_in, o_gla, *scratch[:n_gla_scratch])
    ml_stages, ml_finish = _mlstm_stream(*ml_in, o_ml, *scratch[n_gla_scratch:])
    n_chunks = SEQ_BLOCK // CHUNK
    assert len(ml_stages) == SCAN_STAGES >= len(gla_stages)
    lead = len(ml_stages) - len(gla_stages)
    for t in range(n_chunks + len(ml_stages) - 1):
        for k in reversed(range(len(ml_stages))):
            c = t - k
            if 0 <= c < n_chunks:
                ml_stages[k](c)
                if k >= lead:
                    gla_stages[k - lead](c)
    gla_finish()
    ml_finish()


def _mixer_scan(gla, ml, *, batch, seq):
    assert GLA_HEADS == MLSTM_HEADS
    in_specs = gla["in_specs"] + ml["in_specs"]
    out_specs = [gla["out_spec"], ml["out_spec"]]
    out_shape = [gla["out_shape"], ml["out_shape"]]
    state_bytes = 2 * (GLA_DV * GLA_DK + MLSTM_DV * MLSTM_DK) * 4
    wave_bytes = SCAN_STAGES * 2 * 8 * CHUNK * GLA_DK * 4
    nbytes = (2 * _block_bytes(in_specs, gla["args"] + ml["args"]) + 2 * _block_bytes(out_specs, out_shape)
              + state_bytes + wave_bytes)
    outs = pl.pallas_call(
        functools.partial(_mixer_scan_kernel, n_gla_in=len(gla["args"]), n_ml_in=len(ml["args"]),
                          n_gla_scratch=len(gla["scratch"])),
        out_shape=out_shape,
        grid=(batch, GLA_HEADS, seq // SEQ_BLOCK),
        in_specs=in_specs,
        out_specs=out_specs,
        scratch_shapes=gla["scratch"] + ml["scratch"],
        compiler_params=pltpu.CompilerParams(
            dimension_semantics=("parallel", "parallel", "arbitrary"),
            vmem_limit_bytes=_vmem_limit(nbytes)),
        name="mixer_scan",
    )(*gla["args"], *ml["args"])
    return outs[0], outs[1]


def _xattn_kernel(a_ref, ssq_ref, wq_ref, k_ref, v_ref, o_ref):
    a = a_ref[...]
    r = lax.rsqrt(ssq_ref[:, 0:1] * (1.0 / a.shape[1]) + EPS)
    q = (r * _dot(a, wq_ref[...])).astype(BF16)
    groups = [slice(g * XATTN_ROWS, (g + 1) * XATTN_ROWS) for g in range(a.shape[0] // XATTN_ROWS)]
    scores = [_dot_nt(q[rows, :], k_ref[...]) * (XATTN_HEAD_DIM ** -0.5) for rows in groups]
    probs = []
    for s in scores:
        p = jnp.exp(s - jnp.max(s, axis=-1, keepdims=True))
        probs.append((p / jnp.sum(p, axis=-1, keepdims=True)).astype(BF16))
    for rows, p in zip(groups, probs):
        o_ref[rows, :] = _dot(p, v_ref[...]).astype(o_ref.dtype)


def _xattn(a, ssq, wq, k, v, *, batch, seq, tq):
    t, d = a.shape
    mem = k.shape[0] // batch
    hd = XATTN_HEAD_DIM
    nq = seq // tq
    nbytes = 2 * (tq * d + d * hd + 2 * mem * hd + tq * hd) * 2 + 2 * tq * LANES * 4 + 3 * tq * hd * 4
    return pl.pallas_call(
        _xattn_kernel,
        out_shape=jax.ShapeDtypeStruct((t, d), BF16),
        grid=(batch * nq, XATTN_HEADS),
        in_specs=[pl.BlockSpec((tq, d), lambda i, h: (i, 0)),
                  pl.BlockSpec((tq, LANES), lambda i, h: (i, 0)),
                  pl.BlockSpec((d, hd), lambda i, h: (0, h)),
                  pl.BlockSpec((mem, hd), lambda i, h: (i // nq, h)),
                  pl.BlockSpec((mem, hd), lambda i, h: (i // nq, h))],
        out_specs=pl.BlockSpec((tq, hd), lambda i, h: (i, h)),
        compiler_params=pltpu.CompilerParams(
            dimension_semantics=("parallel", "parallel"),
            vmem_limit_bytes=_vmem_limit(nbytes)),
        name="cross_attention",
    )(a, ssq, wq, k, v)


def kernel(x, mem, norm_mix_g, w_in, gla_gate_w2, gla_gate_b, gla_norm_g, mlstm_conv_w, mlstm_conv_b, mlstm_igate_b, mlstm_fgate_b, mlstm_norm_g, w_out, norm_cross_g, norm_mem_g, wq_c, wk_c, wv_c, wo_c, norm_ffn_g, w_gate, w_up, w_down, norm_final_g):
    batch, seq, d = x.shape
    mem_tokens = mem.shape[1]
    t = batch * seq
    depth = w_in.shape[0]
    gla_cols = 2 * GLA_HEADS * GLA_DK + 2 * GLA_HEADS * GLA_DV
    ml_cols = 2 * MLSTM_HEADS * MLSTM_DK + 2 * MLSTM_HEADS * MLSTM_DV
    ml_start = gla_cols + GLA_GATE_RANK

    h = x.reshape(t, d)
    mem2 = mem.reshape(batch * mem_tokens, d)
    for l in range(depth):
        w_in16 = w_in[l].astype(BF16)
        w_ml = w_in16[:, ml_start:ml_start + ml_cols]
        w_small = jnp.concatenate(
            [w_in[l][:, gla_cols:ml_start], w_in[l][:, ml_start + ml_cols:]], axis=1)
        w_small = jnp.pad(w_small, ((0, 0), (0, SMALL_COLS - w_small.shape[1]))).astype(BF16)
        w2 = gla_gate_w2[l].reshape(GLA_GATE_RANK, GLA_HEADS, GLA_DK).transpose(1, 0, 2)
        w2 = jnp.pad(w2, ((0, 0), (0, SMALL_COLS - GLA_GATE_RANK), (0, 0)))
        gate_b = gla_gate_b[l].reshape(GLA_HEADS, 1, GLA_DK)
        gla_g = gla_norm_g[l].reshape(GLA_HEADS, 1, GLA_DV)
        ml_g = mlstm_norm_g[l].reshape(MLSTM_HEADS, 1, MLSTM_DV)
        gate_bias = jnp.concatenate([mlstm_igate_b[l], mlstm_fgate_b[l]]).astype(F32)
        conv_b = mlstm_conv_b[l].reshape(1, -1)

        n1 = _rmsnorm(h, norm_mix_g[l], BF16)
        proj_gla = _matmul([n1], w_in16, BF16, n_cols=gla_cols, tm=1024, tn=1024, name="in_proj_gla")
        proj_ml, small = _matmul([n1], w_ml, BF16, w_narrow=w_small, tm=1024, tn=1024, name="in_proj_mlstm")
        ig_t = small[:, IG_COL:IG_COL + 2 * MLSTM_HEADS].reshape(
            t // CHUNK, CHUNK, 2 * MLSTM_HEADS).transpose(0, 2, 1)
        o_gla, o_ml = _mixer_scan(
            _gla(proj_gla, small, w2, gate_b, gla_g, batch=batch, seq=seq),
            _mlstm(proj_ml, small, ig_t, gate_bias, mlstm_conv_w[l], conv_b, ml_g, batch=batch, seq=seq),
            batch=batch, seq=seq)
        h, hb, ssq = _matmul([o_gla, o_ml], w_out[l].astype(BF16), F32, resid=h,
                             next_gain=norm_cross_g[l], tm=1024, tn=1024, name="out_proj")

        mem_n = _rmsnorm(mem2, norm_mem_g[l], BF16)
        k_mem, v_mem = _short_proj(mem_n, wk_c[l], wv_c[l], tn=512)
        attn = _xattn(hb, ssq, wq_c[l].astype(BF16), k_mem, v_mem, batch=batch, seq=seq, tq=1024)
        h, hb, ssq = _matmul([attn], wo_c[l].astype(BF16), F32, resid=h,
                             next_gain=norm_ffn_g[l], tm=1024, tn=1024, name="xattn_o")

        hid = _swiglu_up(hb, ssq, w_gate[l], w_up[l], tm=2048, tn=FFN_TN)
        h = _matmul([hid], w_down[l].astype(BF16), F32, resid=h, w_outer=True, tm=512, tn=512,
                    name="ffn_down")

    y = _rmsnorm(h, norm_final_g, x.dtype)
    return y.reshape(batch, seq, d)
```

```python
import functools
import math

import jax
import jax.numpy as jnp
import numpy as np
from jax import lax
from jax.experimental import pallas as pl
from jax.experimental.pallas import tpu as pltpu

F32 = jnp.float32
BF16 = jnp.bfloat16

D_MODEL = 4096
GROUP_WIDTH = D_MODEL // 2
GLA_HEADS = 4
GLA_DV = GROUP_WIDTH // GLA_HEADS
GLA_DK = GLA_DV // 2
GLA_GATE_RANK = 16
GLA_TAU = 16.0
MLSTM_HEADS = 4
MLSTM_DV = GROUP_WIDTH // MLSTM_HEADS
MLSTM_DK = MLSTM_DV // 2
CONV_WIDTH = 4
XATTN_HEADS = 4
XATTN_HEAD_DIM = D_MODEL // XATTN_HEADS
FFN_HIDDEN = 256 * math.ceil(8 * D_MODEL / (3 * 256))
EPS = 1e-6

V7X_VMEM_LIMIT_BYTES = 62 * 1024 * 1024
DEFAULT_SCOPED_VMEM_BYTES = 32 * 1024 * 1024
VMEM_UNMODELLED_BYTES = 8 * 1024 * 1024
LANES = 128
SUBLANES = 8

CHUNK = 64
GLA_LEVELS = (32, 16, 8, 4, 2, 1)
LOG2_E = math.log2(math.e)
SEQ_BLOCK = 1024
SCAN_STAGES = 6
NORM_ROWS = 512
FFN_TN = 256
XATTN_ROWS = 256
SWIGLU_PASS_ROWS = 1024
SMALL_COLS = LANES
IG_COL = GLA_GATE_RANK
FG_COL = GLA_GATE_RANK + MLSTM_HEADS


def _vmem_limit(nbytes):
    return int(min(V7X_VMEM_LIMIT_BYTES, max(DEFAULT_SCOPED_VMEM_BYTES, nbytes + VMEM_UNMODELLED_BYTES)))


def _block_bytes(specs, arrays):
    total = 0
    for spec, arr in zip(specs, arrays):
        if spec.block_shape is not None:
            total += math.prod(d for d in spec.block_shape if d is not None) * jnp.dtype(arr.dtype).itemsize
    return total


def _dot(a, b):
    return jnp.dot(a, b, preferred_element_type=F32)


def _dot_nt(a, b):
    return lax.dot_general(a, b, (((1,), (1,)), ((), ())), preferred_element_type=F32)


def _dot_tn(a, b):
    return lax.dot_general(a, b, (((0,), (0,)), ((), ())), preferred_element_type=F32)


def _split3(x):
    hi = x.astype(BF16)
    r = x - hi.astype(F32)
    mid = r.astype(BF16)
    lo = (r - mid.astype(F32)).astype(BF16)
    return hi, mid, lo


def _log_sigmoid(z):
    return jnp.minimum(z, 0.0) - jnp.log(1.0 + jnp.exp(-jnp.abs(z)))


def _sigmoid(z):
    return 1.0 / (1.0 + jnp.exp(-z))


def _rmsnorm_kernel(x_ref, g_ref, o_ref):
    x = x_ref[...].astype(F32)
    ms = jnp.mean(x * x, axis=-1, keepdims=True)
    o_ref[...] = (x * lax.rsqrt(ms + EPS) * g_ref[...]).astype(o_ref.dtype)


def _rmsnorm(x, g, out_dtype):
    rows, d = x.shape
    tr = min(NORM_ROWS, rows)
    nbytes = 2 * tr * d * (x.dtype.itemsize + jnp.dtype(out_dtype).itemsize) + 3 * tr * d * 4
    return pl.pallas_call(
        _rmsnorm_kernel,
        out_shape=jax.ShapeDtypeStruct((rows, d), out_dtype),
        grid=(rows // tr,),
        in_specs=[pl.BlockSpec((tr, d), lambda i: (i, 0)),
                  pl.BlockSpec((1, d), lambda i: (0, 0))],
        out_specs=pl.BlockSpec((tr, d), lambda i: (i, 0)),
        compiler_params=pltpu.CompilerParams(
            dimension_semantics=("parallel",), vmem_limit_bytes=_vmem_limit(nbytes)),
        name="rmsnorm",
    )(x, g.reshape(1, d).astype(F32))


def _mm_kernel(*refs, n_a, has_resid, has_norm, has_narrow, w_rows):
    a_refs = refs[:n_a]
    w_ref = refs[n_a]
    pos = n_a + 1
    r_ref = refs[pos] if has_resid else None
    pos += int(has_resid)
    g_ref = refs[pos] if has_norm else None
    pos += int(has_norm)
    wn_ref = refs[pos] if has_narrow else None
    pos += int(has_narrow)
    o_ref = refs[pos]

    if has_narrow:
        @pl.when(pl.program_id(1) == 0)
        def _():
            refs[-1][...] = _dot(a_refs[0][...], wn_ref[...])

    kp = a_refs[0].shape[1]
    if w_rows:
        acc = _dot_nt(a_refs[0][...], w_ref[...])
    else:
        acc = _dot(a_refs[0][...], w_ref[0:kp, :])
        for p in range(1, n_a):
            acc = acc + _dot(a_refs[p][...], w_ref[p * kp:(p + 1) * kp, :])
    if has_resid:
        acc = r_ref[...] + acc
    o_ref[...] = acc.astype(o_ref.dtype)

    if has_norm:
        hb_ref, ssq_ref = refs[pos + 1], refs[pos + 2]
        hb_ref[...] = (acc * g_ref[...]).astype(hb_ref.dtype)
        part = jnp.broadcast_to(jnp.sum(acc * acc, axis=1, keepdims=True), ssq_ref.shape)
        j = pl.program_id(1)

        @pl.when(j == 0)
        def _():
            ssq_ref[...] = part

        @pl.when(j > 0)
        def _():
            ssq_ref[...] += part


def _matmul(a_list, w, out_dtype, *, resid=None, next_gain=None, w_narrow=None, n_cols=None, w_outer=False,
            w_rows=False, tm, tn, name):
    m, kp = a_list[0].shape
    kdim = w.shape[1 if w_rows else 0]
    n = w.shape[0 if w_rows else 1] if n_cols is None else n_cols
    n_a = len(a_list)
    assert m % tm == 0 and n % tn == 0 and kdim == n_a * kp and not (w_rows and n_a > 1)
    assert not (w_outer and (next_gain is not None or w_narrow is not None))
    assert w_narrow is None or (n_a == 1 and w_narrow.shape == (kdim, LANES))

    def spec(shape, f):
        return pl.BlockSpec(shape, (lambda j, i: f(i, j)) if w_outer else f)

    in_specs = ([spec((tm, kp), lambda i, j: (i, 0)) for _ in a_list]
                + [spec((tn, kdim), lambda i, j: (j, 0)) if w_rows else spec((kdim, tn), lambda i, j: (0, j))])
    args = list(a_list) + [w]
    out_shape = [jax.ShapeDtypeStruct((m, n), out_dtype)]
    out_specs = [spec((tm, tn), lambda i, j: (i, j))]
    nbytes = 2 * (tm * kdim + kdim * tn) * 2 + 2 * tm * tn * jnp.dtype(out_dtype).itemsize + 2 * tm * tn * 4
    if resid is not None:
        in_specs.append(spec((tm, tn), lambda i, j: (i, j)))
        args.append(resid)
        nbytes += 2 * tm * tn * 4
    if next_gain is not None:
        in_specs.append(spec((1, tn), lambda i, j: (0, j)))
        args.append(next_gain.reshape(1, n).astype(F32))
        out_shape += [jax.ShapeDtypeStruct((m, n), BF16), jax.ShapeDtypeStruct((m, LANES), F32)]
        out_specs += [spec((tm, tn), lambda i, j: (i, j)),
                      spec((tm, LANES), lambda i, j: (i, 0))]
        nbytes += 2 * tm * tn * 2 + 2 * tm * LANES * 4 + tm * tn * 4
    if w_narrow is not None:
        in_specs.append(spec((kdim, LANES), lambda i, j: (0, 0)))
        args.append(w_narrow)
        out_shape.append(jax.ShapeDtypeStruct((m, LANES), F32))
        out_specs.append(spec((tm, LANES), lambda i, j: (i, 0)))
        nbytes += 2 * kdim * LANES * 2 + 2 * tm * LANES * 4
    outs = pl.pallas_call(
        functools.partial(_mm_kernel, n_a=n_a, has_resid=resid is not None,
                          has_norm=next_gain is not None, has_narrow=w_narrow is not None, w_rows=w_rows),
        out_shape=out_shape,
        grid=(n // tn, m // tm) if w_outer else (m // tm, n // tn),
        in_specs=in_specs,
        out_specs=out_specs,
        compiler_params=pltpu.CompilerParams(
            dimension_semantics=("parallel", "arbitrary"),
            vmem_limit_bytes=_vmem_limit(nbytes)),
        name=name,
    )(*args)
    return outs if len(outs) > 1 else outs[0]


def _short_proj_kernel(a_ref, wk_ref, wv_ref, k_ref, v_ref):
    a = a_ref[...]
    k_ref[...] = _dot(a, wk_ref[...].astype(BF16)).astype(k_ref.dtype)
    v_ref[...] = _dot(a, wv_ref[...].astype(BF16)).astype(v_ref.dtype)


def _short_proj(a, wk, wv, *, tn):
    m, kdim = a.shape
    n = wk.shape[1]
    assert n % tn == 0
    nbytes = 2 * m * kdim * 2 + 2 * kdim * tn * (2 * 4 + 2) + 4 * m * tn * 2 + 2 * m * tn * 4
    return pl.pallas_call(
        _short_proj_kernel,
        out_shape=[jax.ShapeDtypeStruct((m, n), BF16)] * 2,
        grid=(n // tn,),
        in_specs=[pl.BlockSpec((m, kdim), lambda j: (0, 0)),
                  pl.BlockSpec((kdim, tn), lambda j: (0, j)),
                  pl.BlockSpec((kdim, tn), lambda j: (0, j))],
        out_specs=[pl.BlockSpec((m, tn), lambda j: (0, j))] * 2,
        compiler_params=pltpu.CompilerParams(
            dimension_semantics=("parallel",), vmem_limit_bytes=_vmem_limit(nbytes)),
        name="xattn_kv",
    )(a, wk, wv)


def _swiglu_up_kernel(a_ref, ssq_ref, wg_ref, wu_ref, o_ref):
    wg = wg_ref[...].astype(BF16)
    wu = wu_ref[...].astype(BF16)
    kdim = a_ref.shape[1]
    for p in range(a_ref.shape[0] // SWIGLU_PASS_ROWS):
        rows = slice(p * SWIGLU_PASS_ROWS, (p + 1) * SWIGLU_PASS_ROWS)
        a = a_ref[rows, :]
        r = lax.rsqrt(ssq_ref[rows, 0:1] * (1.0 / kdim) + EPS)
        gate = r * _dot(a, wg)
        up = r * _dot(a, wu)
        o_ref[rows, :] = (gate * _sigmoid(gate) * up).astype(o_ref.dtype)


def _swiglu_up(a, ssq, wg, wu, *, tm, tn):
    m, kdim = a.shape
    n = wg.shape[1]
    assert m % tm == 0 and n % tn == 0 and tm % SWIGLU_PASS_ROWS == 0
    nbytes = (2 * tm * kdim * 2 + 2 * kdim * tn * (2 * 4 + 2) + 2 * tm * tn * 2 + 2 * tm * LANES * 4
              + 3 * SWIGLU_PASS_ROWS * tn * 4)
    return pl.pallas_call(
        _swiglu_up_kernel,
        out_shape=jax.ShapeDtypeStruct((m, n), BF16),
        grid=(m // tm, n // tn),
        in_specs=[pl.BlockSpec((tm, kdim), lambda i, j: (i, 0)),
                  pl.BlockSpec((tm, LANES), lambda i, j: (i, 0)),
                  pl.BlockSpec((kdim, tn), lambda i, j: (0, j)),
                  pl.BlockSpec((kdim, tn), lambda i, j: (0, j))],
        out_specs=pl.BlockSpec((tm, tn), lambda i, j: (i, j)),
        compiler_params=pltpu.CompilerParams(
            dimension_semantics=("parallel", "arbitrary"),
            vmem_limit_bytes=_vmem_limit(nbytes)),
        name="swiglu_up",
    )(a, ssq, wg, wu)


def _gla_tables():
    c = CHUNK
    row = np.arange(c)[:, None]
    u = np.arange(c)[None, :]
    sel = [u <= row, u > row]
    masks = [row == u]
    for bsz in GLA_LEVELS:
        r = (row // (2 * bsz)) * (2 * bsz) + bsz - 1
        sel.append(np.where(row <= r, (u > row) & (u <= r), (u > r) & (u <= row)))
        same_pair = (row // (2 * bsz)) == (u // (2 * bsz))
        masks.append(same_pair & ((row % (2 * bsz)) >= bsz) & ((u % (2 * bsz)) < bsz))
    sel = np.concatenate(sel, axis=0).astype(np.float32)
    return (jnp.asarray(np.concatenate([sel] * 3, axis=1), BF16),
            jnp.asarray(np.stack(masks).astype(np.float32)))


def _gla_stream(q_ref, k_ref, v_ref, g_ref, sm_ref, w2_ref, gb_ref, ng_ref, sel_ref, mask_ref,
                o_ref, st_ref):
    c_len = CHUNK
    norm_g = ng_ref[...]
    w2 = w2_ref[...].astype(BF16)
    row_k = lax.broadcasted_iota(jnp.int32, (c_len, GLA_DK), 0)
    state = [st_ref[...]]
    tmp = [dict() for _ in range(SEQ_BLOCK // c_len)]

    def rows_of(c):
        return slice(c * c_len, (c + 1) * c_len)

    def gates(c):
        z = _dot(sm_ref[rows_of(c), :].astype(BF16), w2) + gb_ref[...]
        tmp[c]["la3"] = jnp.concatenate(_split3(_log_sigmoid(z) * (LOG2_E / GLA_TAU)), axis=0)

    def decay_table(c):
        tmp[c]["e"] = jnp.exp2(_dot(sel_ref[...], tmp[c].pop("la3")))

    def scores(c):
        rows = rows_of(c)
        e = tmp[c].pop("e")
        q = q_ref[rows, :].astype(F32) * (GLA_DK ** -0.5)
        k = k_ref[rows, :].astype(F32)
        k_dec = (k * e[c_len:2 * c_len, :]).astype(BF16)
        attn = _dot_nt(q.astype(BF16), k.astype(BF16)) * mask_ref[0]
        for lv, bsz in enumerate(GLA_LEVELS):
            in_right = (row_k & bsz) != 0
            x = (jnp.where(in_right, q, k) * e[(2 + lv) * c_len:(3 + lv) * c_len, :]).astype(BF16)
            attn = attn + _dot_nt(x, x) * mask_ref[1 + lv]
        tmp[c].update(q_dec=(q * e[0:c_len, :]).astype(BF16), k_dec=k_dec,
                      decay=e[c_len - 1:c_len, :],
                      intra=_dot(attn.astype(BF16), v_ref[rows, :]))

    def recurrence(c):
        tmp[c]["st_in"] = state[0].astype(BF16)
        state[0] = state[0] * tmp[c].pop("decay") + _dot_tn(v_ref[rows_of(c), :], tmp[c].pop("k_dec"))

    def output(c):
        rows = rows_of(c)
        out = tmp[c].pop("intra") + _dot_nt(tmp[c].pop("q_dec"), tmp[c].pop("st_in"))
        ms = jnp.mean(out * out, axis=-1, keepdims=True)
        y = out * lax.rsqrt(ms + EPS) * norm_g
        gg = g_ref[rows, :].astype(F32)
        o_ref[rows, :] = (y * (gg * _sigmoid(gg))).astype(o_ref.dtype)

    def finish():
        st_ref[...] = state[0]

    return [gates, decay_table, scores, recurrence, output], finish


def _gla(proj, small, w2, gate_b, norm_g, *, batch, seq):
    decay_sel, pair_mask = _gla_tables()
    t = proj.shape[0]
    lb = SEQ_BLOCK
    nsb = seq // lb
    h_ = GLA_HEADS
    qb = GLA_DK
    vb = GLA_DV

    def rows(b, h, i):
        return b * nsb + i

    in_specs = [
        pl.BlockSpec((lb, qb), lambda b, h, i: (rows(b, h, i), h)),
        pl.BlockSpec((lb, qb), lambda b, h, i: (rows(b, h, i), h_ + h)),
        pl.BlockSpec((lb, vb), lambda b, h, i: (rows(b, h, i), (2 * h_ * qb) // vb + h)),
        pl.BlockSpec((lb, vb), lambda b, h, i: (rows(b, h, i), (2 * h_ * qb) // vb + h_ + h)),
        pl.BlockSpec((lb, SMALL_COLS), lambda b, h, i: (rows(b, h, i), 0)),
        pl.BlockSpec((None, SMALL_COLS, qb), lambda b, h, i: (h, 0, 0)),
        pl.BlockSpec((None, 1, qb), lambda b, h, i: (h, 0, 0)),
        pl.BlockSpec((None, 1, vb), lambda b, h, i: (h, 0, 0)),
        pl.BlockSpec(decay_sel.shape, lambda b, h, i: (0, 0)),
        pl.BlockSpec(pair_mask.shape, lambda b, h, i: (0, 0, 0)),
    ]
    return dict(
        in_specs=in_specs,
        args=[proj, proj, proj, proj, small, w2, gate_b, norm_g, decay_sel, pair_mask],
        out_shape=jax.ShapeDtypeStruct((t, h_ * vb), BF16),
        out_spec=pl.BlockSpec((lb, vb), lambda b, h, i: (rows(b, h, i), h)),
        scratch=[pltpu.VMEM((vb, qb), F32)])


def _mlstm_tables():
    lb, c = CHUNK, CHUNK
    t = np.arange(lb)[:, None]
    u = np.arange(lb)[None, :]
    shift = np.concatenate([(u == t - d) for d in range(1, CONV_WIDTH)], axis=0).astype(np.float32)
    tc = np.arange(c)[:, None]
    uc = np.arange(c)[None, :]
    cum = np.concatenate([uc <= tc, uc > tc], axis=0).astype(np.float32)
    col = np.concatenate([tc > uc, np.ones((c, c), bool)], axis=1).astype(np.float32)
    return (jnp.asarray(shift, BF16), jnp.asarray(np.concatenate([cum] * 3, axis=1), BF16),
            jnp.asarray(col))


def _mlstm_stream(gbias_ref, q_ref, k_ref, v_ref, og_ref, sm_ref, igt_ref, cwq_ref, cwk_ref,
                  cbq_ref, cbk_ref, ng_ref, shift_ref, cum_ref, col_ref, o_ref,
                  ct_ref, n_ref, m_ref, hist_ref):
    c_len = CHUNK
    lb = SEQ_BLOCK
    taps = CONV_WIDTH
    h = pl.program_id(1)

    w = jnp.concatenate([cwq_ref[...], cwk_ref[...]], axis=1)
    bias = jnp.concatenate([cbq_ref[...], cbk_ref[...]], axis=1)
    ib = gbias_ref[h]
    fb = gbias_ref[MLSTM_HEADS + h]
    lane_s = lax.broadcasted_iota(jnp.int32, (c_len, SMALL_COLS), 1)
    norm_g = ng_ref[...]
    row_c = lax.broadcasted_iota(jnp.int32, (c_len, c_len), 0)
    col_c = lax.broadcasted_iota(jnp.int32, (c_len, c_len), 1)
    causal = row_c >= col_c

    state = [ct_ref[...], n_ref[...], m_ref[...]]
    tmp = [dict() for _ in range(lb // c_len)]

    def rows_of(c):
        return slice(c * c_len, (c + 1) * c_len)

    def prepare(c):
        rows = rows_of(c)
        x16 = jnp.concatenate([q_ref[rows, :], k_ref[rows, :]], axis=1)
        x = x16.astype(F32)
        shifted = _dot(shift_ref[...], x16)
        tail = hist_ref[...] if c == 0 else tmp[c - 1]["x_tail"]
        head = jnp.concatenate([tail, x[0:SUBLANES, :]], axis=0)
        y = bias + w[taps - 1:taps, :] * x
        for d in range(1, taps):
            sh = jnp.concatenate([head[SUBLANES - d:2 * SUBLANES - d, :],
                                  shifted[(d - 1) * c_len + SUBLANES:d * c_len, :]], axis=0)
            y = y + w[taps - 1 - d:taps - d, :] * sh
        act = y * _sigmoid(y)
        sm = sm_ref[rows, :]
        i_col = jnp.sum(jnp.where(lane_s == IG_COL + h, sm, 0.0), axis=1, keepdims=True) + ib
        f_col = jnp.sum(jnp.where(lane_s == FG_COL + h, sm, 0.0), axis=1, keepdims=True) + fb
        tmp[c].update(x_tail=x[c_len - SUBLANES:c_len, :], q=act[:, 0:MLSTM_DK],
                      k=act[:, MLSTM_DK:2 * MLSTM_DK] * (MLSTM_DK ** -0.5),
                      i_col=i_col, lf_col=_log_sigmoid(f_col))

    def gate_sums(c):
        d = tmp[c]
        g_hi, g_mid, g_lo = _split3(d.pop("lf_col") * col_ref[...])
        sums = _dot(cum_ref[...], jnp.concatenate([g_hi, g_mid, g_lo], axis=0))
        i_row = igt_ref[c, pl.ds(h, 1), :] + ib
        log_d = jnp.where(causal, sums[0:c_len, 0:c_len] + i_row, -jnp.inf)
        bb_col = sums[0:c_len, c_len:c_len + 1]
        log_w = sums[c_len:2 * c_len, c_len:c_len + 1] + d.pop("i_col")
        d.update(log_d=log_d, bb_col=bb_col, log_w=log_w,
                 d_max=jnp.max(log_d, axis=1, keepdims=True),
                 w_max=jnp.max(log_w, axis=0, keepdims=True),
                 qk=_dot_nt(d["q"].astype(BF16), d["k"].astype(BF16)))

    def stabiliser(c):
        d = tmp[c]
        m_prev = state[2]
        bb_col = d.pop("bb_col")
        b_last = bb_col[c_len - 1:c_len, :]
        inter = bb_col + m_prev
        m = jnp.maximum(inter, d.pop("d_max"))
        m_new = jnp.maximum(b_last + m_prev, d.pop("w_max"))
        d.update(m=m, g_inter=jnp.exp(inter - m), w_col=jnp.exp(d.pop("log_w") - m_new),
                 decay=jnp.exp(b_last + m_prev - m_new))
        state[2] = m_new

    def scores(c):
        rows = rows_of(c)
        d = tmp[c]
        s = d.pop("qk") * jnp.exp(d.pop("log_d") - d["m"])
        kw = d.pop("k") * d.pop("w_col")
        d.update(s_sum=jnp.sum(s, axis=1, keepdims=True),
                 intra=_dot(s.astype(BF16), v_ref[rows, :]),
                 kw=kw.astype(BF16),
                 k_sum=jnp.sum(kw, axis=0, keepdims=True))

    def recurrence(c):
        d = tmp[c]
        decay = d.pop("decay")
        d.update(ct_in=state[0].astype(BF16), n_in=state[1])
        state[0] = decay * state[0] + _dot_tn(v_ref[rows_of(c), :], d.pop("kw"))
        state[1] = decay * state[1] + d.pop("k_sum")

    def output(c):
        rows = rows_of(c)
        d = tmp[c]
        q = d.pop("q")
        g_inter = d.pop("g_inter")
        num = d.pop("intra") + g_inter * _dot_nt(q.astype(BF16), d.pop("ct_in"))
        den = d.pop("s_sum") + g_inter * jnp.sum(q * d.pop("n_in"), axis=1, keepdims=True)
        hid = num / jnp.maximum(jnp.abs(den), jnp.exp(-d.pop("m")))
        ms = jnp.mean(hid * hid, axis=-1, keepdims=True)
        y_out = hid * lax.rsqrt(ms + EPS) * norm_g
        og = og_ref[rows, :].astype(F32)
        o_ref[rows, :] = (_sigmoid(og) * y_out).astype(o_ref.dtype)

    def finish():
        ct_ref[...], n_ref[...], m_ref[...] = state
        hist_ref[...] = tmp[-1]["x_tail"]

    return [prepare, gate_sums, stabiliser, scores, recurrence, output], finish


def _mlstm(proj, small, ig_t, gate_bias, conv_w, conv_b, norm_g, *, batch, seq):
    shift, cum_sel, col_sel = _mlstm_tables()
    t = proj.shape[0]
    lb = SEQ_BLOCK
    nsb = seq // lb
    h_ = MLSTM_HEADS
    qb = MLSTM_DK
    vb = MLSTM_DV
    q0 = 0
    v0 = (2 * h_ * qb) // vb

    def rows(b, h, i):
        return b * nsb + i

    in_specs = [
        pl.BlockSpec(memory_space=pltpu.SMEM),
        pl.BlockSpec((lb, qb), lambda b, h, i: (rows(b, h, i), q0 + h)),
        pl.BlockSpec((lb, qb), lambda b, h, i: (rows(b, h, i), q0 + h_ + h)),
        pl.BlockSpec((lb, vb), lambda b, h, i: (rows(b, h, i), v0 + h)),
        pl.BlockSpec((lb, vb), lambda b, h, i: (rows(b, h, i), v0 + h_ + h)),
        pl.BlockSpec((lb, SMALL_COLS), lambda b, h, i: (rows(b, h, i), 0)),
        pl.BlockSpec((lb // CHUNK, 2 * h_, CHUNK), lambda b, h, i: (rows(b, h, i), 0, 0)),
        pl.BlockSpec((CONV_WIDTH, qb), lambda b, h, i: (0, h)),
        pl.BlockSpec((CONV_WIDTH, qb), lambda b, h, i: (0, h_ + h)),
        pl.BlockSpec((1, qb), lambda b, h, i: (0, h)),
        pl.BlockSpec((1, qb), lambda b, h, i: (0, h_ + h)),
        pl.BlockSpec((None, 1, vb), lambda b, h, i: (h, 0, 0)),
        pl.BlockSpec(shift.shape, lambda b, h, i: (0, 0)),
        pl.BlockSpec(cum_sel.shape, lambda b, h, i: (0, 0)),
        pl.BlockSpec(col_sel.shape, lambda b, h, i: (0, 0)),
    ]
    scratch = [
        pltpu.VMEM((vb, qb), F32),
        pltpu.VMEM((1, qb), F32),
        pltpu.VMEM((1, 1), F32),
        pltpu.VMEM((SUBLANES, 2 * qb), F32),
    ]
    return dict(
        in_specs=in_specs,
        args=[gate_bias, proj, proj, proj, proj, small, ig_t, conv_w, conv_w, conv_b, conv_b, norm_g,
              shift, cum_sel, col_sel],
        out_shape=jax.ShapeDtypeStruct((t, h_ * vb), BF16),
        out_spec=pl.BlockSpec((lb, vb), lambda b, h, i: (rows(b, h, i), h)),
        scratch=scratch)


def _mixer_scan_kernel(*refs, n_gla_in, n_ml_in, n_gla_scratch):
    gla_in = refs[:n_gla_in]
    ml_in = refs[n_gla_in:n_gla_in + n_ml_in]
    o_gla, o_ml = refs[n_gla_in + n_ml_in:n_gla_in + n_ml_in + 2]
    scratch = refs[n_gla_in + n_ml_in + 2:]

    @pl.when(pl.program_id(2) == 0)
    def _():
        for ref in scratch:
            ref[...] = jnp.zeros_like(ref)

    gla_stages, gla_finish = _gla_stream(*gla_in, o_gla, *scratch[:n_gla_scratch])
    ml_stages, ml_finish = _mlstm_stream(*ml_in, o_ml, *scratch[n_gla_scratch:])
    n_chunks = SEQ_BLOCK // CHUNK
    assert len(ml_stages) == SCAN_STAGES >= len(gla_stages)
    lead = len(ml_stages) - len(gla_stages)
    for t in range(n_chunks + len(ml_stages) - 1):
        for k in reversed(range(len(ml_stages))):
            c = t - k
            if 0 <= c < n_chunks:
                ml_stages[k](c)
                if k >= lead:
                    gla_stages[k - lead](c)
    gla_finish()
    ml_finish()


def _mixer_scan(gla, ml, *, batch, seq):
    assert GLA_HEADS == MLSTM_HEADS
    in_specs = gla["in_specs"] + ml["in_specs"]
    out_specs = [gla["out_spec"], ml["out_spec"]]
    out_shape = [gla["out_shape"], ml["out_shape"]]
    state_bytes = 2 * (GLA_DV * GLA_DK + MLSTM_DV * MLSTM_DK) * 4
    wave_bytes = SCAN_STAGES * 2 * 8 * CHUNK * GLA_DK * 4
    nbytes = (2 * _block_bytes(in_specs, gla["args"] + ml["args"]) + 2 * _block_bytes(out_specs, out_shape)
              + state_bytes + wave_bytes)
    outs = pl.pallas_call(
        functools.partial(_mixer_scan_kernel, n_gla_in=len(gla["args"]), n_ml_in=len(ml["args"]),
                          n_gla_scratch=len(gla["scratch"])),
        out_shape=out_shape,
        grid=(batch, GLA_HEADS, seq // SEQ_BLOCK),
        in_specs=in_specs,
        out_specs=out_specs,
        scratch_shapes=gla["scratch"] + ml["scratch"],
        compiler_params=pltpu.CompilerParams(
            dimension_semantics=("parallel", "parallel", "arbitrary"),
            vmem_limit_bytes=_vmem_limit(nbytes)),
        name="mixer_scan",
    )(*gla["args"], *ml["args"])
    return outs[0], outs[1]


def _xattn_kernel(a_ref, ssq_ref, wq_ref, k_ref, v_ref, o_ref):
    a = a_ref[...]
    r = lax.rsqrt(ssq_ref[:, 0:1] * (1.0 / a.shape[1]) + EPS)
    q = (r * _dot(a, wq_ref[...])).astype(BF16)
    groups = [slice(g * XATTN_ROWS, (g + 1) * XATTN_ROWS) for g in range(a.shape[0] // XATTN_ROWS)]
    scores = [_dot_nt(q[rows, :], k_ref[...]) * (XATTN_HEAD_DIM ** -0.5) for rows in groups]
    probs = []
    for s in scores:
        p = jnp.exp(s - jnp.max(s, axis=-1, keepdims=True))
        probs.append((p / jnp.sum(p, axis=-1, keepdims=True)).astype(BF16))
    for rows, p in zip(groups, probs):
        o_ref[rows, :] = _dot(p, v_ref[...]).astype(o_ref.dtype)


def _xattn(a, ssq, wq, k, v, *, batch, seq, tq):
    t, d = a.shape
    mem = k.shape[0] // batch
    hd = XATTN_HEAD_DIM
    nq = seq // tq
    nbytes = 2 * (tq * d + d * hd + 2 * mem * hd + tq * hd) * 2 + 2 * tq * LANES * 4 + 3 * tq * hd * 4
    return pl.pallas_call(
        _xattn_kernel,
        out_shape=jax.ShapeDtypeStruct((t, d), BF16),
        grid=(batch * nq, XATTN_HEADS),
        in_specs=[pl.BlockSpec((tq, d), lambda i, h: (i, 0)),
                  pl.BlockSpec((tq, LANES), lambda i, h: (i, 0)),
                  pl.BlockSpec((d, hd), lambda i, h: (0, h)),
                  pl.BlockSpec((mem, hd), lambda i, h: (i // nq, h)),
                  pl.BlockSpec((mem, hd), lambda i, h: (i // nq, h))],
        out_specs=pl.BlockSpec((tq, hd), lambda i, h: (i, h)),
        compiler_params=pltpu.CompilerParams(
            dimension_semantics=("parallel", "parallel"),
            vmem_limit_bytes=_vmem_limit(nbytes)),
        name="cross_attention",
    )(a, ssq, wq, k, v)


def kernel(x, mem, norm_mix_g, w_in, gla_gate_w2, gla_gate_b, gla_norm_g, mlstm_conv_w, mlstm_conv_b, mlstm_igate_b, mlstm_fgate_b, mlstm_norm_g, w_out, norm_cross_g, norm_mem_g, wq_c, wk_c, wv_c, wo_c, norm_ffn_g, w_gate, w_up, w_down, norm_final_g):
    batch, seq, d = x.shape
    mem_tokens = mem.shape[1]
    t = batch * seq
    depth = w_in.shape[0]
    gla_cols = 2 * GLA_HEADS * GLA_DK + 2 * GLA_HEADS * GLA_DV
    ml_cols = 2 * MLSTM_HEADS * MLSTM_DK + 2 * MLSTM_HEADS * MLSTM_DV
    ml_start = gla_cols + GLA_GATE_RANK

    h = x.reshape(t, d)
    mem2 = mem.reshape(batch * mem_tokens, d)
    for l in range(depth):
        w_in16 = w_in[l].T.astype(BF16)
        w_ml = w_in16[ml_start:ml_start + ml_cols, :]
        w_small = jnp.concatenate(
            [w_in[l][:, gla_cols:ml_start], w_in[l][:, ml_start + ml_cols:]], axis=1)
        w_small = jnp.pad(w_small, ((0, 0), (0, SMALL_COLS - w_small.shape[1]))).astype(BF16)
        w2 = gla_gate_w2[l].reshape(GLA_GATE_RANK, GLA_HEADS, GLA_DK).transpose(1, 0, 2)
        w2 = jnp.pad(w2, ((0, 0), (0, SMALL_COLS - GLA_GATE_RANK), (0, 0)))
        gate_b = gla_gate_b[l].reshape(GLA_HEADS, 1, GLA_DK)
        gla_g = gla_norm_g[l].reshape(GLA_HEADS, 1, GLA_DV)
        ml_g = mlstm_norm_g[l].reshape(MLSTM_HEADS, 1, MLSTM_DV)
        gate_bias = jnp.concatenate([mlstm_igate_b[l], mlstm_fgate_b[l]]).astype(F32)
        conv_b = mlstm_conv_b[l].reshape(1, -1)

        n1 = _rmsnorm(h, norm_mix_g[l], BF16)
        proj_gla = _matmul([n1], w_in16, BF16, n_cols=gla_cols, w_rows=True, tm=1024, tn=1024,
                           name="in_proj_gla")
        proj_ml, small = _matmul([n1], w_ml, BF16, w_narrow=w_small, w_rows=True, tm=1024, tn=1024,
                                 name="in_proj_mlstm")
        ig_t = small[:, IG_COL:IG_COL + 2 * MLSTM_HEADS].reshape(
            t // CHUNK, CHUNK, 2 * MLSTM_HEADS).transpose(0, 2, 1)
        o_gla, o_ml = _mixer_scan(
            _gla(proj_gla, small, w2, gate_b, gla_g, batch=batch, seq=seq),
            _mlstm(proj_ml, small, ig_t, gate_bias, mlstm_conv_w[l], conv_b, ml_g, batch=batch, seq=seq),
            batch=batch, seq=seq)
        h, hb, ssq = _matmul([o_gla, o_ml], w_out[l].astype(BF16), F32, resid=h,
                             next_gain=norm_cross_g[l], tm=1024, tn=1024, name="out_proj")

        mem_n = _rmsnorm(mem2, norm_mem_g[l], BF16)
        k_mem, v_mem = _short_proj(mem_n, wk_c[l], wv_c[l], tn=512)
        attn = _xattn(hb, ssq, wq_c[l].astype(BF16), k_mem, v_mem, batch=batch, seq=seq, tq=1024)
        h, hb, ssq = _matmul([attn], wo_c[l].astype(BF16), F32, resid=h,
                             next_gain=norm_ffn_g[l], tm=1024, tn=1024, name="xattn_o")

        hid = _swiglu_up(hb, ssq, w_gate[l], w_up[l], tm=2048, tn=FFN_TN)
        h = _matmul([hid], w_down[l].astype(BF16), F32, resid=h, w_outer=True, tm=512, tn=512,
                    name="ffn_down")

    y = _rmsnorm(h, norm_final_g, x.dtype)
    return y.reshape(batch, seq, d)
```
